```python
import math
import jax, jax.numpy as jnp
from jax import lax
import numpy as np

D_MODEL = 1024
BATCH = 16
SEQ = 4096
DEPTH = 4

MEM_LEN = 256
EPS = 1e-6
BRANCH_WIDTH = 512
N_BRANCH = 3

A_WIDTH = BRANCH_WIDTH
A_HEADS = 8
A_HEAD_DIM = A_WIDTH // A_HEADS
A_CONV = 4
RG_C = 8.0

B_HEADS = 4
B_DK = BRANCH_WIDTH // B_HEADS
B_DV = BRANCH_WIDTH // B_HEADS
B_CONV = 4
B_CHUNK = 64

C_WIDTH = BRANCH_WIDTH
C_GROUP = 16
C_GROUPS = C_WIDTH // C_GROUP
C_STATE = 64

X_HEADS = 4
X_HEAD_DIM = D_MODEL // X_HEADS

D_FF = 3 * D_MODEL
FFN_CONV = 3

IN_SPLITS = (A_WIDTH, A_WIDTH,
             B_HEADS * B_DK, B_HEADS * B_DK,
             B_HEADS * B_DV, B_HEADS * B_DV,
             B_HEADS, B_HEADS,
             C_WIDTH,
             N_BRANCH * D_MODEL)
D_IN = sum(IN_SPLITS)

kernel_name = "hybrid_rglru_deltanet_s5_block"


def rmsnorm(x, g):
    xf = x.astype(jnp.float32)
    var = jnp.mean(xf * xf, axis=-1, keepdims=True)
    return (xf * lax.rsqrt(var + EPS) * g.astype(jnp.float32)).astype(x.dtype)


def l2norm(x):
    return x * lax.rsqrt(jnp.sum(x * x, axis=-1, keepdims=True) + EPS)


def causal_dwconv(x, w):
    k, ch = w.shape
    return lax.conv_general_dilated(x, w[:, None, :].astype(x.dtype), (1,), [(k - 1, 0)],
                                    dimension_numbers=('NWC', 'WIO', 'NWC'),
                                    feature_group_count=ch)


def _linear_combine(e1, e2):
    a1, b1 = e1
    a2, b2 = e2
    return a1 * a2, a2 * b1 + b2


def _complex_linear_combine(e1, e2):
    ar1, ai1, br1, bi1 = e1
    ar2, ai2, br2, bi2 = e2
    return (ar1 * ar2 - ai1 * ai2,
            ar1 * ai2 + ai1 * ar2,
            ar2 * br1 - ai2 * bi1 + br2,
            ar2 * bi1 + ai2 * br1 + bi2)


def rg_lru(x, w_r, b_r, w_i, b_i, lam):
    bsz, s, _ = x.shape
    xf = x.astype(jnp.float32)
    xh = xf.reshape(bsz, s, A_HEADS, A_HEAD_DIM)
    r = jax.nn.sigmoid(jnp.einsum('bshi,hij->bshj', xh, w_r.astype(jnp.float32)).reshape(bsz, s, A_WIDTH) + b_r.astype(jnp.float32))
    ig = jax.nn.sigmoid(jnp.einsum('bshi,hij->bshj', xh, w_i.astype(jnp.float32)).reshape(bsz, s, A_WIDTH) + b_i.astype(jnp.float32))
    log_a = -RG_C * r * jax.nn.softplus(-lam.astype(jnp.float32))
    a = jnp.exp(log_a)
    first = (jnp.arange(s) == 0)[None, :, None]
    mult = jnp.where(first, 1.0, jnp.sqrt(-jnp.expm1(2.0 * log_a)))
    b = mult * ig * xf
    _, h = lax.associative_scan(_linear_combine, (a, b), axis=1)
    return h.astype(x.dtype)


def chunk_gated_delta_rule(q, k, v, g, beta):
    bsz, s, h, dk = q.shape
    dv = v.shape[-1]
    c = B_CHUNK
    n = s // c

    def chunks(t):
        return t.reshape(bsz, n, c, h, t.shape[-1]).transpose(0, 3, 1, 2, 4)

    qc = chunks(q * dk ** -0.5)
    kc = chunks(k)
    vc = chunks(v)
    gc = jnp.cumsum(g.reshape(bsz, n, c, h).transpose(0, 3, 1, 2), axis=-1)
    bc = beta.reshape(bsz, n, c, h).transpose(0, 3, 1, 2)
    incl = jnp.tril(jnp.ones((c, c), dtype=bool))
    strict = jnp.tril(jnp.ones((c, c), dtype=bool), k=-1)
    decay = jnp.exp(jnp.where(incl, gc[..., :, None] - gc[..., None, :], -jnp.inf))
    kb = kc * bc[..., None]
    a_mat = jnp.where(strict, jnp.einsum('bhnik,bhnjk->bhnij', kb, kc) * decay, 0.0)
    lower = a_mat + jnp.eye(c, dtype=a_mat.dtype)
    rhs = jnp.concatenate([vc * bc[..., None], kb * jnp.exp(gc)[..., None]], axis=-1)
    sol = lax.linalg.triangular_solve(lower, rhs, left_side=True, lower=True, unit_diagonal=True)
    u, w = sol[..., :dv], sol[..., dv:]
    qk = jnp.where(incl, jnp.einsum('bhnik,bhnjk->bhnij', qc, kc) * decay, 0.0)
    q_dec = qc * jnp.exp(gc)[..., None]
    k_dec = kc * jnp.exp(gc[..., -1:] - gc)[..., None]
    g_tot = jnp.exp(gc[..., -1])
    xs = tuple(jnp.moveaxis(t, 2, 0) for t in (u, w, qk, q_dec, k_dec, g_tot))

    def step(state, inp):
        u_n, w_n, qk_n, qd_n, kd_n, gt_n = inp
        v_new = u_n - jnp.einsum('bhck,bhkv->bhcv', w_n, state)
        o = jnp.einsum('bhck,bhkv->bhcv', qd_n, state) + jnp.einsum('bhij,bhjv->bhiv', qk_n, v_new)
        state = state * gt_n[..., None, None] + jnp.einsum('bhck,bhcv->bhkv', kd_n, v_new)
        return state, o

    s0 = jnp.zeros((bsz, h, dk, dv), jnp.float32)
    _, o = lax.scan(step, s0, xs)
    return o.transpose(1, 0, 3, 2, 4).reshape(bsz, s, h, dv)


def s5_layer(u, lam_re, lam_im, log_dt, b_re, b_im, c_re, c_im, d):
    bsz, s, _ = u.shape
    f32 = jnp.float32
    uf = u.astype(f32)
    ug = uf.reshape(bsz, s, C_GROUPS, C_GROUP)
    lr, li = lam_re.astype(f32), lam_im.astype(f32)
    dt = jnp.exp(log_dt.astype(f32))[:, None]
    mag = jnp.exp(lr * dt)
    ar, ai = mag * jnp.cos(li * dt), mag * jnp.sin(li * dt)
    den = lr * lr + li * li
    fr = ((ar - 1.0) * lr + ai * li) / den
    fi = (ai * lr - (ar - 1.0) * li) / den
    br, bi = b_re.astype(f32), b_im.astype(f32)
    bbr = fr[..., None] * br - fi[..., None] * bi
    bbi = fr[..., None] * bi + fi[..., None] * br
    bu_r = jnp.einsum('bsgc,gpc->bsgp', ug, bbr)
    bu_i = jnp.einsum('bsgc,gpc->bsgp', ug, bbi)
    a_r = jnp.broadcast_to(ar, (1, s, C_GROUPS, C_STATE))
    a_i = jnp.broadcast_to(ai, (1, s, C_GROUPS, C_STATE))
    _, _, hr, hi = lax.associative_scan(_complex_linear_combine, (a_r, a_i, bu_r, bu_i), axis=1)
    y = (jnp.einsum('bsgp,gcp->bsgc', hr, c_re.astype(f32))
         - jnp.einsum('bsgp,gcp->bsgc', hi, c_im.astype(f32))).reshape(bsz, s, C_WIDTH)
    return y + d.astype(f32) * uf


def hybrid_mixer(h, w_in, b_gate, a_conv_w, a_conv_b, a_w_r, a_b_r, a_w_i, a_b_i, a_lam,
                 b_conv_w, b_a_log, b_dt_bias, b_norm,
                 c_lam_re, c_lam_im, c_log_dt, c_b_re, c_b_im, c_c_re, c_c_im, c_d, c_glu_w, c_glu_b,
                 w_branch, w_out):
    bsz, s, _ = h.shape
    f32 = jnp.float32
    offs = np.cumsum(IN_SPLITS)[:-1].tolist()
    xa, ga, q, k, v, z, beta_raw, alpha_raw, uc, gates = jnp.split(h @ w_in, offs, axis=-1)

    ya = rg_lru(causal_dwconv(xa, a_conv_w) + a_conv_b, a_w_r, a_b_r, a_w_i, a_b_i, a_lam) * jax.nn.gelu(ga)

    qkv = jax.nn.silu(causal_dwconv(jnp.concatenate([q, k, v], axis=-1), b_conv_w)).astype(f32)
    q, k, v = jnp.split(qkv, [B_HEADS * B_DK, 2 * B_HEADS * B_DK], axis=-1)
    q = l2norm(q.reshape(bsz, s, B_HEADS, B_DK))
    k = l2norm(k.reshape(bsz, s, B_HEADS, B_DK))
    v = v.reshape(bsz, s, B_HEADS, B_DV)
    beta = jax.nn.sigmoid(beta_raw.astype(f32))
    g = -jnp.exp(b_a_log.astype(f32)) * jax.nn.softplus(alpha_raw.astype(f32) + b_dt_bias.astype(f32))
    o = chunk_gated_delta_rule(q, k, v, g, beta)
    o = rmsnorm(o, b_norm) * jax.nn.silu(z.astype(f32).reshape(bsz, s, B_HEADS, B_DV))
    yb = o.reshape(bsz, s, B_HEADS * B_DV).astype(h.dtype)

    yc = jax.nn.gelu(s5_layer(uc, c_lam_re, c_lam_im, c_log_dt, c_b_re, c_b_im, c_c_re, c_c_im, c_d))
    yc = (yc * jax.nn.sigmoid(yc @ c_glu_w.astype(f32) + c_glu_b.astype(f32))).astype(h.dtype)

    branches = jnp.stack([ya, yb, yc], axis=2)
    proj = jnp.einsum('bskc,kcd->bskd', branches, w_branch)
    gate = jax.nn.sigmoid((gates + b_gate).reshape(bsz, s, N_BRANCH, D_MODEL))
    return jnp.sum(gate * proj, axis=2) @ w_out


def cross_attention(h, mem_n, w_q, w_kv, w_o):
    bsz, s, _ = h.shape
    m = mem_n.shape[1]
    q = (h @ w_q).reshape(bsz, s, X_HEADS, X_HEAD_DIM)
    k, v = jnp.split(mem_n @ w_kv, 2, axis=-1)
    k = k.reshape(bsz, m, X_HEADS, X_HEAD_DIM)
    v = v.reshape(bsz, m, X_HEADS, X_HEAD_DIM)
    sc = jnp.einsum('bshd,bmhd->bhsm', q, k).astype(jnp.float32) * (X_HEAD_DIM ** -0.5)
    p = jax.nn.softmax(sc, axis=-1).astype(v.dtype)
    o = jnp.einsum('bhsm,bmhd->bshd', p, v).reshape(bsz, s, D_MODEL)
    return o @ w_o


def conv_ffn(h, w_up, conv_w, conv_b, w_down):
    u = causal_dwconv(h @ w_up, conv_w) + conv_b
    gate, val = jnp.split(u, 2, axis=-1)
    return (jax.nn.gelu(gate) * val) @ w_down


def _fwd_setup_inputs(seed: int = 0) -> dict:
    key = jax.random.key(seed)
    ks = iter(jax.random.split(key, 64))
    L = DEPTH
    f32 = jnp.float32

    def nrm(shape, scale):
        return jax.random.normal(next(ks), shape, f32) * scale

    def gain(shape):
        return 1.0 + 0.02 * jax.random.normal(next(ks), shape, f32)

    def unif(shape, lo, hi):
        return jax.random.uniform(next(ks), shape, f32, minval=lo, maxval=hi)

    x = nrm((BATCH, SEQ, D_MODEL), 1.0)
    mem = nrm((BATCH, MEM_LEN, D_MODEL), 1.0)
    mix_norm = gain((L, D_MODEL))
    w_in = nrm((L, D_MODEL, D_IN), D_MODEL ** -0.5)
    b_gate = nrm((L, N_BRANCH * D_MODEL), 0.01)
    a_conv_w = nrm((L, A_CONV, A_WIDTH), A_CONV ** -0.5)
    a_conv_b = nrm((L, A_WIDTH), 0.01)
    a_w_r = nrm((L, A_HEADS, A_HEAD_DIM, A_HEAD_DIM), A_HEAD_DIM ** -0.5)
    a_b_r = nrm((L, A_WIDTH), 0.01)
    a_w_i = nrm((L, A_HEADS, A_HEAD_DIM, A_HEAD_DIM), A_HEAD_DIM ** -0.5)
    a_b_i = nrm((L, A_WIDTH), 0.01)
    a_pow = unif((L, A_WIDTH), 0.9, 0.999) ** (1.0 / RG_C)
    a_lam = jnp.log(a_pow) - jnp.log1p(-a_pow)
    b_conv_w = nrm((L, B_CONV, 2 * B_HEADS * B_DK + B_HEADS * B_DV), B_CONV ** -0.5)
    b_a_log = jnp.log(unif((L, B_HEADS), 1.0, 16.0))
    dt = jnp.exp(unif((L, B_HEADS), math.log(0.001), math.log(0.1)))
    b_dt_bias = dt + jnp.log(-jnp.expm1(-dt))
    b_norm = gain((L, B_DV))
    c_lam_re = -0.5 + 0.01 * jax.random.normal(next(ks), (L, C_GROUPS, C_STATE), f32)
    c_lam_im = math.pi * jnp.arange(C_STATE, dtype=f32) + 0.01 * jax.random.normal(next(ks), (L, C_GROUPS, C_STATE), f32)
    c_log_dt = unif((L, C_GROUPS), math.log(0.001), math.log(0.1))
    c_b_re = nrm((L, C_GROUPS, C_STATE, C_GROUP), (2 * C_GROUP) ** -0.5)
    c_b_im = nrm((L, C_GROUPS, C_STATE, C_GROUP), (2 * C_GROUP) ** -0.5)
    c_c_re = nrm((L, C_GROUPS, C_GROUP, C_STATE), C_STATE ** -0.5)
    c_c_im = nrm((L, C_GROUPS, C_GROUP, C_STATE), C_STATE ** -0.5)
    c_d = nrm((L, C_WIDTH), 1.0)
    c_glu_w = nrm((L, C_WIDTH, C_WIDTH), C_WIDTH ** -0.5)
    c_glu_b = nrm((L, C_WIDTH), 0.01)
    w_branch = nrm((L, N_BRANCH, BRANCH_WIDTH, D_MODEL), BRANCH_WIDTH ** -0.5)
    w_out = nrm((L, D_MODEL, D_MODEL), D_MODEL ** -0.5)
    xa_norm = gain((L, D_MODEL))
    mem_norm = gain((L, D_MODEL))
    xa_w_q = nrm((L, D_MODEL, D_MODEL), D_MODEL ** -0.5)
    xa_w_kv = nrm((L, D_MODEL, 2 * D_MODEL), D_MODEL ** -0.5)
    xa_w_o = nrm((L, D_MODEL, D_MODEL), D_MODEL ** -0.5)
    ffn_norm = gain((L, D_MODEL))
    ffn_w_up = nrm((L, D_MODEL, 2 * D_FF), D_MODEL ** -0.5)
    ffn_conv_w = nrm((L, FFN_CONV, 2 * D_FF), FFN_CONV ** -0.5)
    ffn_conv_b = nrm((L, 2 * D_FF), 0.01)
    ffn_w_down = nrm((L, D_FF, D_MODEL), D_FF ** -0.5)
    final_norm = gain((D_MODEL,))
    return {"x": x, "mem": mem, "mix_norm": mix_norm, "w_in": w_in, "b_gate": b_gate,
            "a_conv_w": a_conv_w, "a_conv_b": a_conv_b, "a_w_r": a_w_r, "a_b_r": a_b_r,
            "a_w_i": a_w_i, "a_b_i": a_b_i, "a_lam": a_lam,
            "b_conv_w": b_conv_w, "b_a_log": b_a_log, "b_dt_bias": b_dt_bias, "b_norm": b_norm,
            "c_lam_re": c_lam_re, "c_lam_im": c_lam_im, "c_log_dt": c_log_dt,
            "c_b_re": c_b_re, "c_b_im": c_b_im, "c_c_re": c_c_re, "c_c_im": c_c_im, "c_d": c_d,
            "c_glu_w": c_glu_w, "c_glu_b": c_glu_b, "w_branch": w_branch, "w_out": w_out,
            "xa_norm": xa_norm, "mem_norm": mem_norm, "xa_w_q": xa_w_q, "xa_w_kv": xa_w_kv, "xa_w_o": xa_w_o,
            "ffn_norm": ffn_norm, "ffn_w_up": ffn_w_up, "ffn_conv_w": ffn_conv_w, "ffn_conv_b": ffn_conv_b,
            "ffn_w_down": ffn_w_down, "final_norm": final_norm}


def _fwd_reference(x, mem, mix_norm, w_in, b_gate, a_conv_w, a_conv_b, a_w_r, a_b_r, a_w_i, a_b_i, a_lam,
              b_conv_w, b_a_log, b_dt_bias, b_norm, c_lam_re, c_lam_im, c_log_dt, c_b_re, c_b_im,
              c_c_re, c_c_im, c_d, c_glu_w, c_glu_b, w_branch, w_out, xa_norm, mem_norm, xa_w_q,
              xa_w_kv, xa_w_o, ffn_norm, ffn_w_up, ffn_conv_w, ffn_conv_b, ffn_w_down, final_norm):
    for l in range(DEPTH):
        h = rmsnorm(x, mix_norm[l])
        x = x + hybrid_mixer(h, w_in[l], b_gate[l], a_conv_w[l], a_conv_b[l], a_w_r[l], a_b_r[l],
                             a_w_i[l], a_b_i[l], a_lam[l], b_conv_w[l], b_a_log[l], b_dt_bias[l], b_norm[l],
                             c_lam_re[l], c_lam_im[l], c_log_dt[l], c_b_re[l], c_b_im[l], c_c_re[l],
                             c_c_im[l], c_d[l], c_glu_w[l], c_glu_b[l], w_branch[l], w_out[l])
        h = rmsnorm(x, xa_norm[l])
        x = x + cross_attention(h, rmsnorm(mem, mem_norm[l]), xa_w_q[l], xa_w_kv[l], xa_w_o[l])
        h = rmsnorm(x, ffn_norm[l])
        x = x + conv_ffn(h, ffn_w_up[l], ffn_conv_w[l], ffn_conv_b[l], ffn_w_down[l])
    return rmsnorm(x, final_norm)


import jax as _jax
import jax.numpy as _jnp

TWIN_FORMAT = 'train_step'
FWD_PARAMS = ['x', 'mem', 'mix_norm', 'w_in', 'b_gate', 'a_conv_w', 'a_conv_b', 'a_w_r', 'a_b_r', 'a_w_i', 'a_b_i', 'a_lam', 'b_conv_w', 'b_a_log', 'b_dt_bias', 'b_norm', 'c_lam_re', 'c_lam_im', 'c_log_dt', 'c_b_re', 'c_b_im', 'c_c_re', 'c_c_im', 'c_d', 'c_glu_w', 'c_glu_b', 'w_branch', 'w_out', 'xa_norm', 'mem_norm', 'xa_w_q', 'xa_w_kv', 'xa_w_o', 'ffn_norm', 'ffn_w_up', 'ffn_conv_w', 'ffn_conv_b', 'ffn_w_down', 'final_norm']
TWIN_WEIGHTS = ['mix_norm', 'w_in', 'b_gate', 'a_conv_w', 'a_conv_b', 'a_w_r', 'a_b_r', 'a_w_i', 'a_b_i', 'a_lam', 'b_conv_w', 'b_a_log', 'b_dt_bias', 'b_norm', 'c_lam_re', 'c_lam_im', 'c_log_dt', 'c_b_re', 'c_b_im', 'c_c_re', 'c_c_im', 'c_d', 'c_glu_w', 'c_glu_b', 'w_branch', 'w_out', 'xa_norm', 'mem_norm', 'xa_w_q', 'xa_w_kv', 'xa_w_o', 'ffn_norm', 'ffn_w_up', 'ffn_conv_w', 'ffn_conv_b', 'ffn_w_down', 'final_norm']
TWIN_DIFF_INPUT = 'x'
TWIN_INPUTS = ['x', 'mem', 'mix_norm', 'w_in', 'b_gate', 'a_conv_w', 'a_conv_b', 'a_w_r', 'a_b_r', 'a_w_i', 'a_b_i', 'a_lam', 'b_conv_w', 'b_a_log', 'b_dt_bias', 'b_norm', 'c_lam_re', 'c_lam_im', 'c_log_dt', 'c_b_re', 'c_b_im', 'c_c_re', 'c_c_im', 'c_d', 'c_glu_w', 'c_glu_b', 'w_branch', 'w_out', 'xa_norm', 'mem_norm', 'xa_w_q', 'xa_w_kv', 'xa_w_o', 'ffn_norm', 'ffn_w_up', 'ffn_conv_w', 'ffn_conv_b', 'ffn_w_down', 'final_norm', 'loss_target', 'm_mix_norm', 'm_w_in', 'm_b_gate', 'm_a_conv_w', 'm_a_conv_b', 'm_a_w_r', 'm_a_b_r', 'm_a_w_i', 'm_a_b_i', 'm_a_lam', 'm_b_conv_w', 'm_b_a_log', 'm_b_dt_bias', 'm_b_norm', 'm_c_lam_re', 'm_c_lam_im', 'm_c_log_dt', 'm_c_b_re', 'm_c_b_im', 'm_c_c_re', 'm_c_c_im', 'm_c_d', 'm_c_glu_w', 'm_c_glu_b', 'm_w_branch', 'm_w_out', 'm_xa_norm', 'm_mem_norm', 'm_xa_w_q', 'm_xa_w_kv', 'm_xa_w_o', 'm_ffn_norm', 'm_ffn_w_up', 'm_ffn_conv_w', 'm_ffn_conv_b', 'm_ffn_w_down', 'm_final_norm', 'v_mix_norm', 'v_w_in', 'v_b_gate', 'v_a_conv_w', 'v_a_conv_b', 'v_a_w_r', 'v_a_b_r', 'v_a_w_i', 'v_a_b_i', 'v_a_lam', 'v_b_conv_w', 'v_b_a_log', 'v_b_dt_bias', 'v_b_norm', 'v_c_lam_re', 'v_c_lam_im', 'v_c_log_dt', 'v_c_b_re', 'v_c_b_im', 'v_c_c_re', 'v_c_c_im', 'v_c_d', 'v_c_glu_w', 'v_c_glu_b', 'v_w_branch', 'v_w_out', 'v_xa_norm', 'v_mem_norm', 'v_xa_w_q', 'v_xa_w_kv', 'v_xa_w_o', 'v_ffn_norm', 'v_ffn_w_up', 'v_ffn_conv_w', 'v_ffn_conv_b', 'v_ffn_w_down', 'v_final_norm']
TWIN_OUTPUTS = ['loss', 'grad_x', 'grad_mix_norm', 'grad_w_in', 'grad_b_gate', 'grad_a_conv_w', 'grad_a_conv_b', 'grad_a_w_r', 'grad_a_b_r', 'grad_a_w_i', 'grad_a_b_i', 'grad_a_lam', 'grad_b_conv_w', 'grad_b_a_log', 'grad_b_dt_bias', 'grad_b_norm', 'grad_c_lam_re', 'grad_c_lam_im', 'grad_c_log_dt', 'grad_c_b_re', 'grad_c_b_im', 'grad_c_c_re', 'grad_c_c_im', 'grad_c_d', 'grad_c_glu_w', 'grad_c_glu_b', 'grad_w_branch', 'grad_w_out', 'grad_xa_norm', 'grad_mem_norm', 'grad_xa_w_q', 'grad_xa_w_kv', 'grad_xa_w_o', 'grad_ffn_norm', 'grad_ffn_w_up', 'grad_ffn_conv_w', 'grad_ffn_conv_b', 'grad_ffn_w_down', 'grad_final_norm', 'delta_mix_norm', 'delta_w_in', 'delta_b_gate', 'delta_a_conv_w', 'delta_a_conv_b', 'delta_a_w_r', 'delta_a_b_r', 'delta_a_w_i', 'delta_a_b_i', 'delta_a_lam', 'delta_b_conv_w', 'delta_b_a_log', 'delta_b_dt_bias', 'delta_b_norm', 'delta_c_lam_re', 'delta_c_lam_im', 'delta_c_log_dt', 'delta_c_b_re', 'delta_c_b_im', 'delta_c_c_re', 'delta_c_c_im', 'delta_c_d', 'delta_c_glu_w', 'delta_c_glu_b', 'delta_w_branch', 'delta_w_out', 'delta_xa_norm', 'delta_mem_norm', 'delta_xa_w_q', 'delta_xa_w_kv', 'delta_xa_w_o', 'delta_ffn_norm', 'delta_ffn_w_up', 'delta_ffn_conv_w', 'delta_ffn_conv_b', 'delta_ffn_w_down', 'delta_final_norm', 'new_m_mix_norm', 'new_m_w_in', 'new_m_b_gate', 'new_m_a_conv_w', 'new_m_a_conv_b', 'new_m_a_w_r', 'new_m_a_b_r', 'new_m_a_w_i', 'new_m_a_b_i', 'new_m_a_lam', 'new_m_b_conv_w', 'new_m_b_a_log', 'new_m_b_dt_bias', 'new_m_b_norm', 'new_m_c_lam_re', 'new_m_c_lam_im', 'new_m_c_log_dt', 'new_m_c_b_re', 'new_m_c_b_im', 'new_m_c_c_re', 'new_m_c_c_im', 'new_m_c_d', 'new_m_c_glu_w', 'new_m_c_glu_b', 'new_m_w_branch', 'new_m_w_out', 'new_m_xa_norm', 'new_m_mem_norm', 'new_m_xa_w_q', 'new_m_xa_w_kv', 'new_m_xa_w_o', 'new_m_ffn_norm', 'new_m_ffn_w_up', 'new_m_ffn_conv_w', 'new_m_ffn_conv_b', 'new_m_ffn_w_down', 'new_m_final_norm', 'new_v_mix_norm', 'new_v_w_in', 'new_v_b_gate', 'new_v_a_conv_w', 'new_v_a_conv_b', 'new_v_a_w_r', 'new_v_a_b_r', 'new_v_a_w_i', 'new_v_a_b_i', 'new_v_a_lam', 'new_v_b_conv_w', 'new_v_b_a_log', 'new_v_b_dt_bias', 'new_v_b_norm', 'new_v_c_lam_re', 'new_v_c_lam_im', 'new_v_c_log_dt', 'new_v_c_b_re', 'new_v_c_b_im', 'new_v_c_c_re', 'new_v_c_c_im', 'new_v_c_d', 'new_v_c_glu_w', 'new_v_c_glu_b', 'new_v_w_branch', 'new_v_w_out', 'new_v_xa_norm', 'new_v_mem_norm', 'new_v_xa_w_q', 'new_v_xa_w_kv', 'new_v_xa_w_o', 'new_v_ffn_norm', 'new_v_ffn_w_up', 'new_v_ffn_conv_w', 'new_v_ffn_conv_b', 'new_v_ffn_w_down', 'new_v_final_norm']
TWIN_LEAF_KINDS = {'loss': 'loss', 'grad_x': 'grad_x', 'grad_mix_norm': 'grad_w', 'grad_w_in': 'grad_w', 'grad_b_gate': 'grad_w', 'grad_a_conv_w': 'grad_w', 'grad_a_conv_b': 'grad_w', 'grad_a_w_r': 'grad_w', 'grad_a_b_r': 'grad_w', 'grad_a_w_i': 'grad_w', 'grad_a_b_i': 'grad_w', 'grad_a_lam': 'grad_w', 'grad_b_conv_w': 'grad_w', 'grad_b_a_log': 'grad_w', 'grad_b_dt_bias': 'grad_w', 'grad_b_norm': 'grad_w', 'grad_c_lam_re': 'grad_w', 'grad_c_lam_im': 'grad_w', 'grad_c_log_dt': 'grad_w', 'grad_c_b_re': 'grad_w', 'grad_c_b_im': 'grad_w', 'grad_c_c_re': 'grad_w', 'grad_c_c_im': 'grad_w', 'grad_c_d': 'grad_w', 'grad_c_glu_w': 'grad_w', 'grad_c_glu_b': 'grad_w', 'grad_w_branch': 'grad_w', 'grad_w_out': 'grad_w', 'grad_xa_norm': 'grad_w', 'grad_mem_norm': 'grad_w', 'grad_xa_w_q': 'grad_w', 'grad_xa_w_kv': 'grad_w', 'grad_xa_w_o': 'grad_w', 'grad_ffn_norm': 'grad_w', 'grad_ffn_w_up': 'grad_w', 'grad_ffn_conv_w': 'grad_w', 'grad_ffn_conv_b': 'grad_w', 'grad_ffn_w_down': 'grad_w', 'grad_final_norm': 'grad_w', 'delta_mix_norm': 'delta_w', 'delta_w_in': 'delta_w', 'delta_b_gate': 'delta_w', 'delta_a_conv_w': 'delta_w', 'delta_a_conv_b': 'delta_w', 'delta_a_w_r': 'delta_w', 'delta_a_b_r': 'delta_w', 'delta_a_w_i': 'delta_w', 'delta_a_b_i': 'delta_w', 'delta_a_lam': 'delta_w', 'delta_b_conv_w': 'delta_w', 'delta_b_a_log': 'delta_w', 'delta_b_dt_bias': 'delta_w', 'delta_b_norm': 'delta_w', 'delta_c_lam_re': 'delta_w', 'delta_c_lam_im': 'delta_w', 'delta_c_log_dt': 'delta_w', 'delta_c_b_re': 'delta_w', 'delta_c_b_im': 'delta_w', 'delta_c_c_re': 'delta_w', 'delta_c_c_im': 'delta_w', 'delta_c_d': 'delta_w', 'delta_c_glu_w': 'delta_w', 'delta_c_glu_b': 'delta_w', 'delta_w_branch': 'delta_w', 'delta_w_out': 'delta_w', 'delta_xa_norm': 'delta_w', 'delta_mem_norm': 'delta_w', 'delta_xa_w_q': 'delta_w', 'delta_xa_w_kv': 'delta_w', 'delta_xa_w_o': 'delta_w', 'delta_ffn_norm': 'delta_w', 'delta_ffn_w_up': 'delta_w', 'delta_ffn_conv_w': 'delta_w', 'delta_ffn_conv_b': 'delta_w', 'delta_ffn_w_down': 'delta_w', 'delta_final_norm': 'delta_w', 'new_m_mix_norm': 'new_m', 'new_m_w_in': 'new_m', 'new_m_b_gate': 'new_m', 'new_m_a_conv_w': 'new_m', 'new_m_a_conv_b': 'new_m', 'new_m_a_w_r': 'new_m', 'new_m_a_b_r': 'new_m', 'new_m_a_w_i': 'new_m', 'new_m_a_b_i': 'new_m', 'new_m_a_lam': 'new_m', 'new_m_b_conv_w': 'new_m', 'new_m_b_a_log': 'new_m', 'new_m_b_dt_bias': 'new_m', 'new_m_b_norm': 'new_m', 'new_m_c_lam_re': 'new_m', 'new_m_c_lam_im': 'new_m', 'new_m_c_log_dt': 'new_m', 'new_m_c_b_re': 'new_m', 'new_m_c_b_im': 'new_m', 'new_m_c_c_re': 'new_m', 'new_m_c_c_im': 'new_m', 'new_m_c_d': 'new_m', 'new_m_c_glu_w': 'new_m', 'new_m_c_glu_b': 'new_m', 'new_m_w_branch': 'new_m', 'new_m_w_out': 'new_m', 'new_m_xa_norm': 'new_m', 'new_m_mem_norm': 'new_m', 'new_m_xa_w_q': 'new_m', 'new_m_xa_w_kv': 'new_m', 'new_m_xa_w_o': 'new_m', 'new_m_ffn_norm': 'new_m', 'new_m_ffn_w_up': 'new_m', 'new_m_ffn_conv_w': 'new_m', 'new_m_ffn_conv_b': 'new_m', 'new_m_ffn_w_down': 'new_m', 'new_m_final_norm': 'new_m', 'new_v_mix_norm': 'new_v', 'new_v_w_in': 'new_v', 'new_v_b_gate': 'new_v', 'new_v_a_conv_w': 'new_v', 'new_v_a_conv_b': 'new_v', 'new_v_a_w_r': 'new_v', 'new_v_a_b_r': 'new_v', 'new_v_a_w_i': 'new_v', 'new_v_a_b_i': 'new_v', 'new_v_a_lam': 'new_v', 'new_v_b_conv_w': 'new_v', 'new_v_b_a_log': 'new_v', 'new_v_b_dt_bias': 'new_v', 'new_v_b_norm': 'new_v', 'new_v_c_lam_re': 'new_v', 'new_v_c_lam_im': 'new_v', 'new_v_c_log_dt': 'new_v', 'new_v_c_b_re': 'new_v', 'new_v_c_b_im': 'new_v', 'new_v_c_c_re': 'new_v', 'new_v_c_c_im': 'new_v', 'new_v_c_d': 'new_v', 'new_v_c_glu_w': 'new_v', 'new_v_c_glu_b': 'new_v', 'new_v_w_branch': 'new_v', 'new_v_w_out': 'new_v', 'new_v_xa_norm': 'new_v', 'new_v_mem_norm': 'new_v', 'new_v_xa_w_q': 'new_v', 'new_v_xa_w_kv': 'new_v', 'new_v_xa_w_o': 'new_v', 'new_v_ffn_norm': 'new_v', 'new_v_ffn_w_up': 'new_v', 'new_v_ffn_conv_w': 'new_v', 'new_v_ffn_conv_b': 'new_v', 'new_v_ffn_w_down': 'new_v', 'new_v_final_norm': 'new_v'}


def _forward(args):
    return _fwd_reference(*[args[k] for k in FWD_PARAMS])


def _output_shape():
    out = _jax.eval_shape(lambda: _forward(_fwd_setup_inputs(0)))
    return out.shape, out.dtype

N_MICROBATCH = 1
ADAM_LR = 0.001
ADAM_B1 = 0.9
ADAM_B2 = 0.999
ADAM_EPS = 1e-08
ADAM_WD = 0.01
ADAM_STEP = 10
PER_EXAMPLE_BATCH_AXIS = {'x': 0, 'mem': 0, 'loss_target': 0}
SHARED_INPUTS = []
_WEIGHT_DTYPES = {'mix_norm': _jnp.float32, 'w_in': _jnp.float32, 'b_gate': _jnp.float32, 'a_conv_w': _jnp.float32, 'a_conv_b': _jnp.float32, 'a_w_r': _jnp.float32, 'a_b_r': _jnp.float32, 'a_w_i': _jnp.float32, 'a_b_i': _jnp.float32, 'a_lam': _jnp.float32, 'b_conv_w': _jnp.float32, 'b_a_log': _jnp.float32, 'b_dt_bias': _jnp.float32, 'b_norm': _jnp.float32, 'c_lam_re': _jnp.float32, 'c_lam_im': _jnp.float32, 'c_log_dt': _jnp.float32, 'c_b_re': _jnp.float32, 'c_b_im': _jnp.float32, 'c_c_re': _jnp.float32, 'c_c_im': _jnp.float32, 'c_d': _jnp.float32, 'c_glu_w': _jnp.float32, 'c_glu_b': _jnp.float32, 'w_branch': _jnp.float32, 'w_out': _jnp.float32, 'xa_norm': _jnp.float32, 'mem_norm': _jnp.float32, 'xa_w_q': _jnp.float32, 'xa_w_kv': _jnp.float32, 'xa_w_o': _jnp.float32, 'ffn_norm': _jnp.float32, 'ffn_w_up': _jnp.float32, 'ffn_conv_w': _jnp.float32, 'ffn_conv_b': _jnp.float32, 'ffn_w_down': _jnp.float32, 'final_norm': _jnp.float32}
MOMENT_SCALE = {'mix_norm': 2.073832e-01, 'w_in': 8.066412e-02, 'b_gate': 3.416235e-02, 'a_conv_w': 1.588101e-01, 'a_conv_b': 1.304699e+00, 'a_w_r': 4.849041e-02, 'a_b_r': 4.531801e-02, 'a_w_i': 8.905779e-02, 'a_b_i': 5.944496e-02, 'a_lam': 8.810113e-02, 'b_conv_w': 9.106457e-02, 'b_a_log': 4.095892e-01, 'b_dt_bias': 4.014176e-01, 'b_norm': 2.457990e-01, 'c_lam_re': 7.724974e-03, 'c_lam_im': 7.800092e-03, 'c_log_dt': 6.410093e+00, 'c_b_re': 3.977172e-03, 'c_b_im': 3.864616e-03, 'c_c_re': 5.625222e-03, 'c_c_im': 5.808840e-03, 'c_d': 9.867533e-02, 'c_glu_w': 2.174762e-02, 'c_glu_b': 3.447033e-02, 'w_branch': 8.740426e-02, 'w_out': 1.523595e-01, 'xa_norm': 2.846210e-02, 'mem_norm': 4.444650e-02, 'xa_w_q': 2.722597e-02, 'xa_w_kv': 3.242356e-02, 'xa_w_o': 3.776562e-02, 'ffn_norm': 1.946337e-01, 'ffn_w_up': 7.956238e-02, 'ffn_conv_w': 7.996772e-02, 'ffn_conv_b': 9.289982e-02, 'ffn_w_down': 1.357071e-01, 'final_norm': 6.398687e+01}


def _to_microbatches(a, axis):
    t = _jnp.moveaxis(a, axis, 0)
    t = t.reshape((N_MICROBATCH, t.shape[0] // N_MICROBATCH) + t.shape[1:])
    return _jnp.moveaxis(t, 1, axis + 1)


def setup_inputs(seed: int = 0) -> dict:
    inp = _fwd_setup_inputs(seed)
    key = _jax.random.fold_in(_jax.random.key(seed), 7919)
    shape, _ = _output_shape()
    out = dict(inp)
    out["loss_target"] = _jax.random.normal(_jax.random.fold_in(key, 0), shape, _jnp.float32)
    for i, name in enumerate(TWIN_WEIGHTS):
        w = inp[name].astype(_jnp.float32)
        if MOMENT_SCALE is None:
            s = _jnp.sqrt(_jnp.mean(_jnp.square(w)) + 1e-30)
        else:
            s = MOMENT_SCALE[name]
        km, kv = _jax.random.split(_jax.random.fold_in(key, i + 1))
        out[name] = w
        out["m_" + name] = s * _jax.random.normal(km, w.shape, _jnp.float32)
        out["v_" + name] = (s * s) * _jax.random.uniform(kv, w.shape, _jnp.float32, 0.5, 1.5)
    if N_MICROBATCH > 1:
        for name, axis in PER_EXAMPLE_BATCH_AXIS.items():
            out[name] = _to_microbatches(out[name], axis)
    return {'x': out['x'], 'mem': out['mem'], 'mix_norm': out['mix_norm'], 'w_in': out['w_in'], 'b_gate': out['b_gate'], 'a_conv_w': out['a_conv_w'], 'a_conv_b': out['a_conv_b'], 'a_w_r': out['a_w_r'], 'a_b_r': out['a_b_r'], 'a_w_i': out['a_w_i'], 'a_b_i': out['a_b_i'], 'a_lam': out['a_lam'], 'b_conv_w': out['b_conv_w'], 'b_a_log': out['b_a_log'], 'b_dt_bias': out['b_dt_bias'], 'b_norm': out['b_norm'], 'c_lam_re': out['c_lam_re'], 'c_lam_im': out['c_lam_im'], 'c_log_dt': out['c_log_dt'], 'c_b_re': out['c_b_re'], 'c_b_im': out['c_b_im'], 'c_c_re': out['c_c_re'], 'c_c_im': out['c_c_im'], 'c_d': out['c_d'], 'c_glu_w': out['c_glu_w'], 'c_glu_b': out['c_glu_b'], 'w_branch': out['w_branch'], 'w_out': out['w_out'], 'xa_norm': out['xa_norm'], 'mem_norm': out['mem_norm'], 'xa_w_q': out['xa_w_q'], 'xa_w_kv': out['xa_w_kv'], 'xa_w_o': out['xa_w_o'], 'ffn_norm': out['ffn_norm'], 'ffn_w_up': out['ffn_w_up'], 'ffn_conv_w': out['ffn_conv_w'], 'ffn_conv_b': out['ffn_conv_b'], 'ffn_w_down': out['ffn_w_down'], 'final_norm': out['final_norm'], 'loss_target': out['loss_target'], 'm_mix_norm': out['m_mix_norm'], 'm_w_in': out['m_w_in'], 'm_b_gate': out['m_b_gate'], 'm_a_conv_w': out['m_a_conv_w'], 'm_a_conv_b': out['m_a_conv_b'], 'm_a_w_r': out['m_a_w_r'], 'm_a_b_r': out['m_a_b_r'], 'm_a_w_i': out['m_a_w_i'], 'm_a_b_i': out['m_a_b_i'], 'm_a_lam': out['m_a_lam'], 'm_b_conv_w': out['m_b_conv_w'], 'm_b_a_log': out['m_b_a_log'], 'm_b_dt_bias': out['m_b_dt_bias'], 'm_b_norm': out['m_b_norm'], 'm_c_lam_re': out['m_c_lam_re'], 'm_c_lam_im': out['m_c_lam_im'], 'm_c_log_dt': out['m_c_log_dt'], 'm_c_b_re': out['m_c_b_re'], 'm_c_b_im': out['m_c_b_im'], 'm_c_c_re': out['m_c_c_re'], 'm_c_c_im': out['m_c_c_im'], 'm_c_d': out['m_c_d'], 'm_c_glu_w': out['m_c_glu_w'], 'm_c_glu_b': out['m_c_glu_b'], 'm_w_branch': out['m_w_branch'], 'm_w_out': out['m_w_out'], 'm_xa_norm': out['m_xa_norm'], 'm_mem_norm': out['m_mem_norm'], 'm_xa_w_q': out['m_xa_w_q'], 'm_xa_w_kv': out['m_xa_w_kv'], 'm_xa_w_o': out['m_xa_w_o'], 'm_ffn_norm': out['m_ffn_norm'], 'm_ffn_w_up': out['m_ffn_w_up'], 'm_ffn_conv_w': out['m_ffn_conv_w'], 'm_ffn_conv_b': out['m_ffn_conv_b'], 'm_ffn_w_down': out['m_ffn_w_down'], 'm_final_norm': out['m_final_norm'], 'v_mix_norm': out['v_mix_norm'], 'v_w_in': out['v_w_in'], 'v_b_gate': out['v_b_gate'], 'v_a_conv_w': out['v_a_conv_w'], 'v_a_conv_b': out['v_a_conv_b'], 'v_a_w_r': out['v_a_w_r'], 'v_a_b_r': out['v_a_b_r'], 'v_a_w_i': out['v_a_w_i'], 'v_a_b_i': out['v_a_b_i'], 'v_a_lam': out['v_a_lam'], 'v_b_conv_w': out['v_b_conv_w'], 'v_b_a_log': out['v_b_a_log'], 'v_b_dt_bias': out['v_b_dt_bias'], 'v_b_norm': out['v_b_norm'], 'v_c_lam_re': out['v_c_lam_re'], 'v_c_lam_im': out['v_c_lam_im'], 'v_c_log_dt': out['v_c_log_dt'], 'v_c_b_re': out['v_c_b_re'], 'v_c_b_im': out['v_c_b_im'], 'v_c_c_re': out['v_c_c_re'], 'v_c_c_im': out['v_c_c_im'], 'v_c_d': out['v_c_d'], 'v_c_glu_w': out['v_c_glu_w'], 'v_c_glu_b': out['v_c_glu_b'], 'v_w_branch': out['v_w_branch'], 'v_w_out': out['v_w_out'], 'v_xa_norm': out['v_xa_norm'], 'v_mem_norm': out['v_mem_norm'], 'v_xa_w_q': out['v_xa_w_q'], 'v_xa_w_kv': out['v_xa_w_kv'], 'v_xa_w_o': out['v_xa_w_o'], 'v_ffn_norm': out['v_ffn_norm'], 'v_ffn_w_up': out['v_ffn_w_up'], 'v_ffn_conv_w': out['v_ffn_conv_w'], 'v_ffn_conv_b': out['v_ffn_conv_b'], 'v_ffn_w_down': out['v_ffn_w_down'], 'v_final_norm': out['v_final_norm']}


def _loss(weights, diff, rest, loss_target):
    with _jax.named_scope("forward"):
        args = {**rest, TWIN_DIFF_INPUT: diff, **{k: w.astype(_WEIGHT_DTYPES[k]) for k, w in weights.items()}}
        y = _forward(args)
    with _jax.named_scope("loss_head"):
        err = _jnp.square(y.astype(_jnp.float32) - loss_target)
        return 0.5 * _jnp.sum(_jnp.mean(err, axis=-1)) if err.ndim else 0.5 * err


def _adamw(w, g, m, v):
    m = ADAM_B1 * m + (1.0 - ADAM_B1) * g
    v = ADAM_B2 * v + (1.0 - ADAM_B2) * _jnp.square(g)
    m_hat = m / (1.0 - ADAM_B1 ** ADAM_STEP)
    v_hat = v / (1.0 - ADAM_B2 ** ADAM_STEP)
    delta = -ADAM_LR * (m_hat / (_jnp.sqrt(v_hat) + ADAM_EPS) + ADAM_WD * w)
    return delta, m, v


def reference(x, mem, mix_norm, w_in, b_gate, a_conv_w, a_conv_b, a_w_r, a_b_r, a_w_i, a_b_i, a_lam, b_conv_w, b_a_log, b_dt_bias, b_norm, c_lam_re, c_lam_im, c_log_dt, c_b_re, c_b_im, c_c_re, c_c_im, c_d, c_glu_w, c_glu_b, w_branch, w_out, xa_norm, mem_norm, xa_w_q, xa_w_kv, xa_w_o, ffn_norm, ffn_w_up, ffn_conv_w, ffn_conv_b, ffn_w_down, final_norm, loss_target, m_mix_norm, m_w_in, m_b_gate, m_a_conv_w, m_a_conv_b, m_a_w_r, m_a_b_r, m_a_w_i, m_a_b_i, m_a_lam, m_b_conv_w, m_b_a_log, m_b_dt_bias, m_b_norm, m_c_lam_re, m_c_lam_im, m_c_log_dt, m_c_b_re, m_c_b_im, m_c_c_re, m_c_c_im, m_c_d, m_c_glu_w, m_c_glu_b, m_w_branch, m_w_out, m_xa_norm, m_mem_norm, m_xa_w_q, m_xa_w_kv, m_xa_w_o, m_ffn_norm, m_ffn_w_up, m_ffn_conv_w, m_ffn_conv_b, m_ffn_w_down, m_final_norm, v_mix_norm, v_w_in, v_b_gate, v_a_conv_w, v_a_conv_b, v_a_w_r, v_a_b_r, v_a_w_i, v_a_b_i, v_a_lam, v_b_conv_w, v_b_a_log, v_b_dt_bias, v_b_norm, v_c_lam_re, v_c_lam_im, v_c_log_dt, v_c_b_re, v_c_b_im, v_c_c_re, v_c_c_im, v_c_d, v_c_glu_w, v_c_glu_b, v_w_branch, v_w_out, v_xa_norm, v_mem_norm, v_xa_w_q, v_xa_w_kv, v_xa_w_o, v_ffn_norm, v_ffn_w_up, v_ffn_conv_w, v_ffn_conv_b, v_ffn_w_down, v_final_norm):
    given = dict(x=x, mem=mem, mix_norm=mix_norm, w_in=w_in, b_gate=b_gate, a_conv_w=a_conv_w, a_conv_b=a_conv_b, a_w_r=a_w_r, a_b_r=a_b_r, a_w_i=a_w_i, a_b_i=a_b_i, a_lam=a_lam, b_conv_w=b_conv_w, b_a_log=b_a_log, b_dt_bias=b_dt_bias, b_norm=b_norm, c_lam_re=c_lam_re, c_lam_im=c_lam_im, c_log_dt=c_log_dt, c_b_re=c_b_re, c_b_im=c_b_im, c_c_re=c_c_re, c_c_im=c_c_im, c_d=c_d, c_glu_w=c_glu_w, c_glu_b=c_glu_b, w_branch=w_branch, w_out=w_out, xa_norm=xa_norm, mem_norm=mem_norm, xa_w_q=xa_w_q, xa_w_kv=xa_w_kv, xa_w_o=xa_w_o, ffn_norm=ffn_norm, ffn_w_up=ffn_w_up, ffn_conv_w=ffn_conv_w, ffn_conv_b=ffn_conv_b, ffn_w_down=ffn_w_down, final_norm=final_norm, loss_target=loss_target, m_mix_norm=m_mix_norm, m_w_in=m_w_in, m_b_gate=m_b_gate, m_a_conv_w=m_a_conv_w, m_a_conv_b=m_a_conv_b, m_a_w_r=m_a_w_r, m_a_b_r=m_a_b_r, m_a_w_i=m_a_w_i, m_a_b_i=m_a_b_i, m_a_lam=m_a_lam, m_b_conv_w=m_b_conv_w, m_b_a_log=m_b_a_log, m_b_dt_bias=m_b_dt_bias, m_b_norm=m_b_norm, m_c_lam_re=m_c_lam_re, m_c_lam_im=m_c_lam_im, m_c_log_dt=m_c_log_dt, m_c_b_re=m_c_b_re, m_c_b_im=m_c_b_im, m_c_c_re=m_c_c_re, m_c_c_im=m_c_c_im, m_c_d=m_c_d, m_c_glu_w=m_c_glu_w, m_c_glu_b=m_c_glu_b, m_w_branch=m_w_branch, m_w_out=m_w_out, m_xa_norm=m_xa_norm, m_mem_norm=m_mem_norm, m_xa_w_q=m_xa_w_q, m_xa_w_kv=m_xa_w_kv, m_xa_w_o=m_xa_w_o, m_ffn_norm=m_ffn_norm, m_ffn_w_up=m_ffn_w_up, m_ffn_conv_w=m_ffn_conv_w, m_ffn_conv_b=m_ffn_conv_b, m_ffn_w_down=m_ffn_w_down, m_final_norm=m_final_norm, v_mix_norm=v_mix_norm, v_w_in=v_w_in, v_b_gate=v_b_gate, v_a_conv_w=v_a_conv_w, v_a_conv_b=v_a_conv_b, v_a_w_r=v_a_w_r, v_a_b_r=v_a_b_r, v_a_w_i=v_a_w_i, v_a_b_i=v_a_b_i, v_a_lam=v_a_lam, v_b_conv_w=v_b_conv_w, v_b_a_log=v_b_a_log, v_b_dt_bias=v_b_dt_bias, v_b_norm=v_b_norm, v_c_lam_re=v_c_lam_re, v_c_lam_im=v_c_lam_im, v_c_log_dt=v_c_log_dt, v_c_b_re=v_c_b_re, v_c_b_im=v_c_b_im, v_c_c_re=v_c_c_re, v_c_c_im=v_c_c_im, v_c_d=v_c_d, v_c_glu_w=v_c_glu_w, v_c_glu_b=v_c_glu_b, v_w_branch=v_w_branch, v_w_out=v_w_out, v_xa_norm=v_xa_norm, v_mem_norm=v_mem_norm, v_xa_w_q=v_xa_w_q, v_xa_w_kv=v_xa_w_kv, v_xa_w_o=v_xa_w_o, v_ffn_norm=v_ffn_norm, v_ffn_w_up=v_ffn_w_up, v_ffn_conv_w=v_ffn_conv_w, v_ffn_conv_b=v_ffn_conv_b, v_ffn_w_down=v_ffn_w_down, v_final_norm=v_final_norm)
    weights = {n: given[n] for n in TWIN_WEIGHTS}
    shared = {n: given[n] for n in SHARED_INPUTS}
    per_example = {n: given[n] for n in ['x', 'mem']}
    grad_fn = _jax.value_and_grad(_loss, argnums=(0, 1))

    def one_microbatch(ex, loss_target):
        ex = dict(ex)
        diff = ex.pop(TWIN_DIFF_INPUT)
        return grad_fn(weights, diff, {**shared, **ex}, loss_target)

    if N_MICROBATCH == 1:
        loss, (grad_w, grad_x) = one_microbatch(per_example, given["loss_target"])
    else:
        def body(carry, xs):
            loss_sum, grad_sum = carry
            l_k, (gw_k, gx_k) = one_microbatch(xs[0], xs[1])
            with _jax.named_scope("update"):
                return (loss_sum + l_k, _jax.tree.map(_jnp.add, grad_sum, gw_k)), gx_k

        init = (_jnp.zeros((), _jnp.float32), _jax.tree.map(_jnp.zeros_like, weights))
        (loss, grad_w), grad_x = _jax.lax.scan(body, init, (per_example, given["loss_target"]))
    with _jax.named_scope("update"):
        delta_w, new_m, new_v = {}, {}, {}
        for n in TWIN_WEIGHTS:
            delta_w[n], new_m[n], new_v[n] = _adamw(weights[n], grad_w[n], given["m_" + n], given["v_" + n])
    return (loss, grad_x, *[grad_w[n] for n in TWIN_WEIGHTS], *[delta_w[n] for n in TWIN_WEIGHTS],
            *[new_m[n] for n in TWIN_WEIGHTS], *[new_v[n] for n in TWIN_WEIGHTS])
```

```python
import functools
from typing import Any, NamedTuple

import jax
import jax.numpy as jnp
import numpy as np
from jax import lax
from jax.experimental import pallas as pl
from jax.experimental.pallas import tpu as pltpu

F32 = jnp.float32
BF = jnp.bfloat16
MXU_DTYPE = BF

EPS = 1e-6
RG_C = 8.0
N_DEV = 8
DEPTH = 4
D_MODEL = 1024
BW = 512
A_HEADS, A_HD = 8, 64
B_HEADS, B_DK = 4, 128
B_CHUNK = 64
C_GROUPS, C_GROUP, C_STATE = 32, 16, 64
C_CH = C_GROUPS * C_STATE
S5_CW = 512
X_HEADS, X_HD = 4, 256
D_FF = 3 * D_MODEL
ADAM_LR, ADAM_B1, ADAM_B2, ADAM_EPS, ADAM_WD, ADAM_STEP = 0.001, 0.9, 0.999, 1e-08, 0.01, 10

SUBLANES = 8
VMEM_LIMIT = 56 * 1024 * 1024

WEIGHTS = ['mix_norm', 'w_in', 'b_gate', 'a_conv_w', 'a_conv_b', 'a_w_r', 'a_b_r', 'a_w_i', 'a_b_i', 'a_lam',
           'b_conv_w', 'b_a_log', 'b_dt_bias', 'b_norm', 'c_lam_re', 'c_lam_im', 'c_log_dt', 'c_b_re', 'c_b_im',
           'c_c_re', 'c_c_im', 'c_d', 'c_glu_w', 'c_glu_b', 'w_branch', 'w_out', 'xa_norm', 'mem_norm', 'xa_w_q',
           'xa_w_kv', 'xa_w_o', 'ffn_norm', 'ffn_w_up', 'ffn_conv_w', 'ffn_conv_b', 'ffn_w_down', 'final_norm']
SHARD_AXIS = {'w_in': 2, 'a_conv_w': 2, 'b_conv_w': 2, 'c_glu_w': 1, 'w_branch': 3, 'w_out': 1, 'xa_w_q': 1,
              'xa_w_kv': 2, 'xa_w_o': 1, 'ffn_w_up': 2, 'ffn_conv_w': 2, 'ffn_w_down': 1}
BIG = ['w_in', 'c_glu_w', 'w_branch', 'w_out', 'xa_w_q', 'xa_w_kv', 'xa_w_o', 'ffn_w_up', 'ffn_w_down']
SMALL_SHARDED = ['a_conv_w', 'b_conv_w', 'ffn_conv_w']
SMALL = [n for n in WEIGHTS if n not in BIG]


def _dot(x, y, tx, ty):
    cx = 0 if tx else 1
    cy = 1 if ty else 0
    return lax.dot_general(x.astype(MXU_DTYPE), y.astype(MXU_DTYPE), (((cx,), (cy,)), ((), ())),
                           preferred_element_type=F32)


@functools.partial(jax.custom_vjp, nondiff_argnums=(2, 3))
def bmm(a, b, ta=False, tb=False):
    return _dot(a, b, ta, tb)


def _bmm_fwd(a, b, ta, tb):
    return _dot(a, b, ta, tb), (a, b)


def _bmm_bwd(ta, tb, res, g):
    a, b = res
    da = _dot(b, g, tb, True) if ta else _dot(g, b, False, not tb)
    db = _dot(g, a, True, ta) if tb else _dot(a, g, not ta, False)
    return da.astype(a.dtype), db.astype(b.dtype)


bmm.defvjp(_bmm_fwd, _bmm_bwd)


def _dotx(x, y, tx, ty):
    cx = 0 if tx else 1
    cy = 1 if ty else 0
    return lax.dot_general(x, y, (((cx,), (cy,)), ((), ())), precision=lax.Precision.HIGHEST,
                           preferred_element_type=F32)


@functools.partial(jax.custom_vjp, nondiff_argnums=(2, 3))
def xmm(a, b, ta=False, tb=False):
    return _dotx(a, b, ta, tb)


def _xmm_fwd(a, b, ta, tb):
    return _dotx(a, b, ta, tb), (a, b)


def _xmm_bwd(ta, tb, res, g):
    a, b = res
    da = _dotx(b, g, tb, True) if ta else _dotx(g, b, False, not tb)
    db = _dotx(g, a, True, ta) if tb else _dotx(a, g, not ta, False)
    return da, db


xmm.defvjp(_xmm_fwd, _xmm_bwd)


@functools.partial(jax.custom_vjp, nondiff_argnums=(1,))
def roll_rows(x, s):
    return pltpu.roll(x, s, 0)


def _roll_rows_fwd(x, s):
    return pltpu.roll(x, s, 0), None


def _roll_rows_bwd(s, _, g):
    return (pltpu.roll(g, (g.shape[0] - s) % g.shape[0], 0),)


roll_rows.defvjp(_roll_rows_fwd, _roll_rows_bwd)


def shift_rows(cur_tail, s):
    cur, tail = cur_tail
    if s == 0:
        return cur
    rolled = roll_rows(cur, s)
    row = lax.broadcasted_iota(jnp.int32, tail.shape, 0)
    top = jnp.where(row < s, roll_rows(tail, s), rolled[:SUBLANES])
    if cur.shape[0] == SUBLANES:
        return top
    return jnp.concatenate([top, rolled[SUBLANES:]], axis=0)


def softplus(x):
    return jnp.maximum(x, 0.0) + jnp.log(1.0 + jnp.exp(-jnp.abs(x)))


def expm1(x):
    series = x * (1.0 + x * (0.5 + x * (1.0 / 6.0 + x * (1.0 / 24.0 + x * (1.0 / 120.0)))))
    return jnp.where(jnp.abs(x) < 0.05, series, jnp.exp(x) - 1.0)


def sigmoid(x):
    return 1.0 / (1.0 + jnp.exp(-x))


def silu(x):
    return x * sigmoid(x)


def gelu(x):
    return 0.5 * x * (1.0 + jnp.tanh(0.7978845608028654 * (x + 0.044715 * (x * x * x))))


def rms(x, g):
    var = jnp.mean(x * x, axis=-1, keepdims=True)
    return x * lax.rsqrt(var + EPS) * g


def cumsum_rows(x):
    n = x.shape[0]
    row = lax.broadcasted_iota(jnp.int32, x.shape, 0)
    s = 1
    while s < n:
        x = x + jnp.where(row >= s, roll_rows(x, s), 0.0)
        s *= 2
    return x


def _pick(n, prefs):
    for p in prefs:
        if n % p == 0:
            return p
    return n


def mm(a, b, *, ta=False, tb=False, a_cols=None, b_cols=None, residual=None, out_dtype=F32, name):
    a0, aw = a_cols if a_cols is not None else (0, a.shape[1])
    b0, bw = b_cols if b_cols is not None else (0, b.shape[1])
    if ta:
        K, M = a.shape[0], aw
    else:
        M, K = a.shape[0], aw
    if tb:
        N, Kb = b.shape[0], bw
    else:
        Kb, N = b.shape[0], bw
    assert K == Kb, (name, a.shape, b.shape, ta, tb)
    tm = _pick(M, (1024, 512, 256, 128))
    tn = _pick(N, (512, 256, 128))
    tk = _pick(K, (1024, 512, 256, 128))
    nk = K // tk

    def off(c0, t):
        assert c0 % t == 0, (name, c0, t)
        return c0 // t

    if ta:
        a_spec = pl.BlockSpec((tk, tm), lambda i, j, k, o=off(a0, tm): (k, i + o))
    else:
        a_spec = pl.BlockSpec((tm, tk), lambda i, j, k, o=off(a0, tk): (i, k + o))
    if tb:
        b_spec = pl.BlockSpec((tn, tk), lambda i, j, k, o=off(b0, tk): (j, k + o))
    else:
        b_spec = pl.BlockSpec((tk, tn), lambda i, j, k, o=off(b0, tn): (k, j + o))
    o_spec = pl.BlockSpec((tm, tn), lambda i, j, k: (i, j))
    in_specs = [a_spec, b_spec]
    args = [a, b]
    if residual is not None:
        in_specs.append(o_spec)
        args.append(residual)

    def body(*refs):
        a_ref, b_ref = refs[0], refs[1]
        r_ref = refs[2] if residual is not None else None
        o_ref = refs[3] if residual is not None else refs[2]
        part = _dot(a_ref[...], b_ref[...], ta, tb)

        def finish(acc):
            if r_ref is not None:
                acc = acc + r_ref[...].astype(F32)
            o_ref[...] = acc.astype(out_dtype)

        if nk == 1:
            finish(part)
        else:
            acc_ref = refs[-1]
            k = pl.program_id(2)

            @pl.when(k == 0)
            def _():
                acc_ref[...] = part

            @pl.when(k > 0)
            def _():
                acc_ref[...] += part

            @pl.when(k == nk - 1)
            def _():
                finish(acc_ref[...])

    return pl.pallas_call(
        body, name=name, grid=(M // tm, N // tn, nk),
        in_specs=in_specs, out_specs=o_spec,
        out_shape=jax.ShapeDtypeStruct((M, N), out_dtype),
        scratch_shapes=[pltpu.VMEM((tm, tn), F32)] if nk > 1 else [],
        compiler_params=pltpu.CompilerParams(dimension_semantics=("parallel", "parallel", "arbitrary"),
                                             vmem_limit_bytes=VMEM_LIMIT),
    )(*args)


class Tok(NamedTuple):
    arr: Any
    width: int
    col: int = 0
    halo: bool = False
    grad: Any = None
    add: Any = None


class Par(NamedTuple):
    arr: Any
    kind: str = 'const'
    col: int = 0
    width: int = 0
    grad: bool = True


def _tok_specs(toks, tb, rev, ntile):
    specs, args = [], []
    for t in toks:
        if rev:
            cur = lambda j, s, c=t.col: (ntile - 1 - s, c + j)
            tail = lambda j, s, c=t.col: (jnp.maximum((ntile - 1 - s) * (tb // SUBLANES) - 1, 0), c + j)
        else:
            cur = lambda j, s, c=t.col: (s, c + j)
            tail = lambda j, s, c=t.col: (jnp.maximum(s * (tb // SUBLANES) - 1, 0), c + j)
        specs.append(pl.BlockSpec((tb, t.width), cur))
        args.append(t.arr)
        if t.halo:
            specs.append(pl.BlockSpec((SUBLANES, t.width), tail))
            args.append(t.arr)
    return specs, args


def _par_specs(pars, tpb, rev, ntile):
    specs, args = [], []
    for p in pars:
        if p.kind == 'const':
            specs.append(pl.BlockSpec(p.arr.shape, lambda j, s: (0, 0)))
        elif p.kind == 'col':
            specs.append(pl.BlockSpec((p.arr.shape[0], p.width), lambda j, s, c=p.col: (0, c + j)))
        else:
            if rev:
                specs.append(pl.BlockSpec((None,) + p.arr.shape[1:], lambda j, s: ((ntile - 1 - s) // tpb, 0, 0)))
            else:
                specs.append(pl.BlockSpec((None,) + p.arr.shape[1:], lambda j, s: (s // tpb, 0, 0)))
        args.append(p.arr)
    return specs, args


def _read_toks(toks, refs, t0):
    vals, k = [], 0
    for t in toks:
        cur = refs[k][...]
        k += 1
        if t.halo:
            tail = jnp.where(t0 == 0, jnp.zeros_like(refs[k][...]), refs[k][...])
            k += 1
            vals.append((cur, tail))
        else:
            vals.append(cur)
    return vals, k


def stage_fwd(fn, toks, pars, outs, *, seq, tb, ncol=1, name):
    T = toks[0].arr.shape[0]
    tb = min(tb, seq)
    ntile, tpb = T // tb, seq // tb
    tspecs, targs = _tok_specs(toks, tb, False, ntile)
    pspecs, pargs = _par_specs(pars, tpb, False, ntile)
    n_in = len(tspecs) + len(pspecs)

    def body(*refs):
        s = pl.program_id(1)
        t0 = (s % tpb) * tb
        tvals, k = _read_toks(toks, refs, t0)
        pvals = [r[...] for r in refs[k:n_in]]
        res = fn(t0, *tvals, *pvals)
        for r, v in zip(refs[n_in:], res):
            r[...] = v.astype(r.dtype)

    return pl.pallas_call(
        body, name=name, grid=(ncol, ntile),
        in_specs=tspecs + pspecs,
        out_specs=[pl.BlockSpec((tb, w), lambda j, s: (s, j)) for w, _ in outs],
        out_shape=[jax.ShapeDtypeStruct((T, w * ncol), d) for w, d in outs],
        compiler_params=pltpu.CompilerParams(dimension_semantics=("arbitrary", "arbitrary"),
                                             vmem_limit_bytes=VMEM_LIMIT),
    )(*targs, *pargs)


def stage_bwd(fn, toks, pars, cots, *, cot_fn=None, seq, tb, ncol=1, name):
    T = toks[0].arr.shape[0]
    tb = min(tb, seq)
    ntile, tpb = T // tb, seq // tb
    tspecs, targs = _tok_specs(toks, tb, True, ntile)
    pspecs, pargs = _par_specs(pars, tpb, True, ntile)
    cspecs, cargs = _tok_specs(cots, tb, True, ntile)
    adds = [t for t in toks if t.add is not None]
    aspecs = [pl.BlockSpec((tb, t.width), lambda j, s: (ntile - 1 - s, j)) for t in adds]
    aargs = [t.add for t in adds]
    n_t, n_p, n_c = len(tspecs), len(pspecs), len(cspecs)

    gtoks = [t for t in toks if t.grad is not None]
    gpars = [p for p in pars if p.grad]
    out_specs, out_shape = [], []
    for t in gtoks:
        out_specs.append(pl.BlockSpec((tb, t.width), lambda j, s: (ntile - 1 - s, j)))
        out_shape.append(jax.ShapeDtypeStruct((T, t.width * ncol), t.grad))
    for p in gpars:
        if p.kind == 'const':
            out_specs.append(pl.BlockSpec(p.arr.shape, lambda j, s: (0, 0)))
            out_shape.append(jax.ShapeDtypeStruct(p.arr.shape, F32))
        elif p.kind == 'col':
            out_specs.append(pl.BlockSpec((p.arr.shape[0], p.width), lambda j, s: (0, j)))
            out_shape.append(jax.ShapeDtypeStruct((p.arr.shape[0], p.width * ncol), F32))
        else:
            out_specs.append(pl.BlockSpec((None,) + p.arr.shape[1:], lambda j, s: ((ntile - 1 - s) // tpb, 0, 0)))
            out_shape.append(jax.ShapeDtypeStruct(p.arr.shape, F32))
    carries = [t for t in gtoks if t.halo]
    scratch = [pltpu.VMEM((SUBLANES, t.width), F32) for t in carries]

    def body(*refs):
        j, s = pl.program_id(0), pl.program_id(1)
        i = ntile - 1 - s
        t0 = (i % tpb) * tb
        t_refs = refs[:n_t]
        p_refs = refs[n_t:n_t + n_p]
        c_refs = refs[n_t + n_p:n_t + n_p + n_c]
        a_refs = refs[n_t + n_p + n_c:n_t + n_p + n_c + len(adds)]
        o_refs = refs[n_t + n_p + n_c + len(adds):]
        gt_refs = o_refs[:len(gtoks)]
        gp_refs = o_refs[len(gtoks):len(gtoks) + len(gpars)]
        carry_refs = o_refs[len(gtoks) + len(gpars):]

        tvals, _ = _read_toks(toks, t_refs, t0)
        pvals = [r[...] for r in p_refs]
        cvals, _ = _read_toks(cots, c_refs, t0)

        def f(tv, pv):
            return tuple(fn(t0, *tv, *pv))

        res, vjp = jax.vjp(f, tvals, pvals)
        ct = cot_fn(t0, *cvals) if cot_fn is not None else tuple(cvals)
        ct = tuple(c.astype(r.dtype) for c, r in zip(ct, res))
        dt, dp = vjp(ct)

        ci = 0
        ai = 0
        gi = 0
        for t, d in zip(toks, dt):
            if t.grad is None:
                if t.add is not None:
                    ai += 1
                continue
            ref = gt_refs[gi]
            gi += 1
            if t.halo:
                dcur, dtail = d
                carry = carry_refs[ci]
                ci += 1

                @pl.when(s == 0)
                def _(carry=carry):
                    carry[...] = jnp.zeros_like(carry)

                top = dcur[:tb - SUBLANES] if tb > SUBLANES else None
                bot = dcur[tb - SUBLANES:] + carry[...]
                dcur = bot if top is None else jnp.concatenate([top, bot], axis=0)
                carry[...] = jnp.where(t0 == 0, jnp.zeros_like(dtail), dtail)
            else:
                dcur = d
            if t.add is not None:
                dcur = dcur + a_refs[ai][...].astype(F32)
                ai += 1
            ref[...] = dcur.astype(ref.dtype)

        gi = 0
        for p, d in zip(pars, dp):
            if not p.grad:
                continue
            ref = gp_refs[gi]
            gi += 1
            if p.kind == 'const':
                first = jnp.logical_and(j == 0, s == 0)
            elif p.kind == 'col':
                first = s == 0
            else:
                first = s % tpb == 0

            @pl.when(first)
            def _(ref=ref, d=d):
                ref[...] = d.astype(F32)

            @pl.when(jnp.logical_not(first))
            def _(ref=ref, d=d):
                ref[...] += d.astype(F32)

    res = pl.pallas_call(
        body, name=name, grid=(ncol, ntile),
        in_specs=tspecs + pspecs + cspecs + aspecs,
        out_specs=out_specs, out_shape=out_shape, scratch_shapes=scratch,
        compiler_params=pltpu.CompilerParams(dimension_semantics=("arbitrary", "arbitrary"),
                                             vmem_limit_bytes=VMEM_LIMIT),
    )(*targs, *pargs, *cargs, *aargs)
    return list(res[:len(gtoks)]), list(res[len(gtoks):])


def _bcast_row(x, r):
    return jnp.broadcast_to(x[r:r + 1, :], x.shape)


SCAN_TB = 512


def scan_real(a, b, *, seq, reverse, name):
    T, C = a.shape
    tb = min(SCAN_TB, seq)
    nb, nt, nblk = T // seq, seq // tb, tb // SUBLANES

    def body(a_ref, b_ref, h_ref, carry_h, carry_a):
        @pl.when(pl.program_id(1) == 0)
        def _():
            carry_h[...] = jnp.zeros_like(carry_h)
            carry_a[...] = jnp.zeros_like(carry_a)

        row = lax.broadcasted_iota(jnp.int32, (SUBLANES, C), 0)

        def blk(n, c):
            ch, ca = c
            k = nblk - 1 - n if reverse else n
            o = pl.multiple_of(k * SUBLANES, SUBLANES)
            A = a_ref[pl.ds(o, SUBLANES), :]
            B = b_ref[pl.ds(o, SUBLANES), :]
            if reverse:
                a_first = _bcast_row(A, 0)
                A = jnp.where(row == SUBLANES - 1, ca, pltpu.roll(A, SUBLANES - 1, 0))
                for s in (1, 2, 4):
                    keep = row < SUBLANES - s
                    Bs = jnp.where(keep, pltpu.roll(B, SUBLANES - s, 0), 0.0)
                    As = jnp.where(keep, pltpu.roll(A, SUBLANES - s, 0), 1.0)
                    B = B + A * Bs
                    A = A * As
                h = B + A * ch
                h_ref[pl.ds(o, SUBLANES), :] = h
                return _bcast_row(h, 0), a_first
            for s in (1, 2, 4):
                keep = row >= s
                Bs = jnp.where(keep, pltpu.roll(B, s, 0), 0.0)
                As = jnp.where(keep, pltpu.roll(A, s, 0), 1.0)
                B = B + A * Bs
                A = A * As
            h = B + A * ch
            h_ref[pl.ds(o, SUBLANES), :] = h
            return _bcast_row(h, SUBLANES - 1), ca

        ch, ca = lax.fori_loop(0, nblk, blk, (carry_h[...], carry_a[...]))
        carry_h[...] = ch
        carry_a[...] = ca

    if reverse:
        spec = pl.BlockSpec((tb, C), lambda bi, i: (bi * nt + nt - 1 - i, 0))
    else:
        spec = pl.BlockSpec((tb, C), lambda bi, i: (bi * nt + i, 0))
    return pl.pallas_call(
        body, name=name, grid=(nb, nt), in_specs=[spec, spec], out_specs=spec,
        out_shape=jax.ShapeDtypeStruct((T, C), F32),
        scratch_shapes=[pltpu.VMEM((SUBLANES, C), F32), pltpu.VMEM((SUBLANES, C), F32)],
        compiler_params=pltpu.CompilerParams(dimension_semantics=("arbitrary", "arbitrary"),
                                             vmem_limit_bytes=VMEM_LIMIT),
    )(a, b)


def scan_cplx(ar, ai, bu, *, seq, reverse, name):
    T = bu.shape[0]
    cw = S5_CW
    ncol = C_CH // cw
    tb = min(SCAN_TB, seq)
    nb, nt, nblk = T // seq, seq // tb, tb // SUBLANES

    def body(ar_ref, ai_ref, b_ref, h_ref, carry_r, carry_i):
        @pl.when(pl.program_id(2) == 0)
        def _():
            carry_r[...] = jnp.zeros_like(carry_r)
            carry_i[...] = jnp.zeros_like(carry_i)

        row = lax.broadcasted_iota(jnp.int32, (SUBLANES, cw), 0)
        Ar = jnp.broadcast_to(ar_ref[...], (SUBLANES, cw))
        Ai = jnp.broadcast_to(ai_ref[...], (SUBLANES, cw))
        levels = []
        for s in (1, 2, 4):
            keep = (row < SUBLANES - s) if reverse else (row >= s)
            sh = SUBLANES - s if reverse else s
            levels.append((Ar, Ai, keep, sh))
            Asr = jnp.where(keep, pltpu.roll(Ar, sh, 0), 1.0)
            Asi = jnp.where(keep, pltpu.roll(Ai, sh, 0), 0.0)
            Ar, Ai = Ar * Asr - Ai * Asi, Ar * Asi + Ai * Asr

        def blk(n, c):
            cr, ci = c
            k = nblk - 1 - n if reverse else n
            o = pl.multiple_of(k * SUBLANES, SUBLANES)
            Br = b_ref[pl.ds(o, SUBLANES), :cw]
            Bi = b_ref[pl.ds(o, SUBLANES), cw:]
            for lr, li, keep, sh in levels:
                Bsr = jnp.where(keep, pltpu.roll(Br, sh, 0), 0.0)
                Bsi = jnp.where(keep, pltpu.roll(Bi, sh, 0), 0.0)
                Br, Bi = Br + lr * Bsr - li * Bsi, Bi + lr * Bsi + li * Bsr
            hr = Br + Ar * cr - Ai * ci
            hi = Bi + Ar * ci + Ai * cr
            h_ref[pl.ds(o, SUBLANES), :cw] = hr
            h_ref[pl.ds(o, SUBLANES), cw:] = hi
            last = 0 if reverse else SUBLANES - 1
            return _bcast_row(hr, last), _bcast_row(hi, last)

        cr, ci = lax.fori_loop(0, nblk, blk, (carry_r[...], carry_i[...]))
        carry_r[...] = cr
        carry_i[...] = ci

    if reverse:
        spec = pl.BlockSpec((tb, 2 * cw), lambda bi, j, i: (bi * nt + nt - 1 - i, j))
    else:
        spec = pl.BlockSpec((tb, 2 * cw), lambda bi, j, i: (bi * nt + i, j))
    aspec = pl.BlockSpec((1, cw), lambda bi, j, i: (0, j))
    return pl.pallas_call(
        body, name=name, grid=(nb, ncol, nt), in_specs=[aspec, aspec, spec], out_specs=spec,
        out_shape=jax.ShapeDtypeStruct((T, 2 * C_CH), F32),
        scratch_shapes=[pltpu.VMEM((SUBLANES, cw), F32), pltpu.VMEM((SUBLANES, cw), F32)],
        compiler_params=pltpu.CompilerParams(dimension_semantics=("arbitrary", "arbitrary", "arbitrary"),
                                             vmem_limit_bytes=VMEM_LIMIT),
    )(ar, ai, bu)


def delta_chunk(q, k, v, bB, gB, S):
    c = B_CHUNK
    qc = q * (B_DK ** -0.5)
    gc = cumsum_rows(gB)
    ii = lax.broadcasted_iota(jnp.int32, (c, c), 0)
    jj = lax.broadcasted_iota(jnp.int32, (c, c), 1)
    incl, strict, diag = ii >= jj, ii > jj, ii == jj
    gcol = gc[:, :c]
    grow = xmm(jnp.ones((c, c), F32), jnp.where(diag, gcol, 0.0))
    decay = jnp.exp(jnp.where(incl, gcol - grow, -1e30))
    kb = k * bB
    a_mat = jnp.where(strict, bmm(kb, k, False, True) * decay, 0.0)
    eye = jnp.where(diag, 1.0, 0.0)
    tinv = eye - a_mat
    p = a_mat
    for _ in range(5):
        p = xmm(p, p)
        tinv = tinv + xmm(tinv, p)
    eg = jnp.exp(gc)
    u = xmm(tinv, v * bB)
    w = xmm(tinv, kb * eg)
    qk = jnp.where(incl, bmm(qc, k, False, True) * decay, 0.0)
    glast = gc[c - 1:c, :]
    k_dec = k * jnp.exp(glast - gc)
    v_new = u - bmm(w, S)
    o = bmm(qc * eg, S) + bmm(qk, v_new)
    s_new = S * jnp.exp(glast) + bmm(k_dec, v_new, True, False)
    return o, s_new


def delta_fwd(q, k, v, bB, gB, *, name):
    nb, seq, _ = q.shape
    n = seq // B_CHUNK
    hd = B_DK

    def body(q_ref, k_ref, v_ref, b_ref, g_ref, o_ref, st_ref, state):
        @pl.when(pl.program_id(1) == 0)
        def _():
            state[...] = jnp.zeros_like(state)

        for h in range(B_HEADS):
            sl = slice(h * hd, (h + 1) * hd)
            st_ref[h] = state[h]
            o, s_new = delta_chunk(q_ref[:, sl], k_ref[:, sl], v_ref[:, sl], b_ref[:, sl], g_ref[:, sl], state[h])
            o_ref[:, sl] = o
            state[h] = s_new

    spec = pl.BlockSpec((None, B_CHUNK, BW), lambda b, i: (b, i, 0))
    return pl.pallas_call(
        body, name=name, grid=(nb, n), in_specs=[spec] * 5,
        out_specs=[spec, pl.BlockSpec((None, None, B_HEADS, hd, hd), lambda b, i: (b, i, 0, 0, 0))],
        out_shape=[jax.ShapeDtypeStruct((nb, seq, BW), F32), jax.ShapeDtypeStruct((nb, n, B_HEADS, hd, hd), F32)],
        scratch_shapes=[pltpu.VMEM((B_HEADS, hd, hd), F32)],
        compiler_params=pltpu.CompilerParams(dimension_semantics=("arbitrary", "arbitrary"),
                                             vmem_limit_bytes=VMEM_LIMIT),
    )(q, k, v, bB, gB)


def delta_bwd(q, k, v, bB, gB, states, do, *, name):
    nb, seq, _ = q.shape
    n = seq // B_CHUNK
    hd = B_DK

    def body(q_ref, k_ref, v_ref, b_ref, g_ref, st_ref, do_ref, dq_ref, dk_ref, dv_ref, db_ref, dg_ref, dstate):
        @pl.when(pl.program_id(1) == 0)
        def _():
            dstate[...] = jnp.zeros_like(dstate)

        for h in range(B_HEADS):
            sl = slice(h * hd, (h + 1) * hd)
            _, vjp = jax.vjp(delta_chunk, q_ref[:, sl], k_ref[:, sl], v_ref[:, sl], b_ref[:, sl], g_ref[:, sl], st_ref[h])
            dq, dk, dv, db, dg, ds = vjp((do_ref[:, sl], dstate[h]))
            dq_ref[:, sl] = dq
            dk_ref[:, sl] = dk
            dv_ref[:, sl] = dv
            db_ref[:, sl] = db
            dg_ref[:, sl] = dg
            dstate[h] = ds

    spec = pl.BlockSpec((None, B_CHUNK, BW), lambda b, i: (b, n - 1 - i, 0))
    sspec = pl.BlockSpec((None, None, B_HEADS, hd, hd), lambda b, i: (b, n - 1 - i, 0, 0, 0))
    return pl.pallas_call(
        body, name=name, grid=(nb, n), in_specs=[spec] * 5 + [sspec, spec],
        out_specs=[spec] * 5, out_shape=[jax.ShapeDtypeStruct((nb, seq, BW), F32)] * 5,
        scratch_shapes=[pltpu.VMEM((B_HEADS, hd, hd), F32)],
        compiler_params=pltpu.CompilerParams(dimension_semantics=("arbitrary", "arbitrary"),
                                             vmem_limit_bytes=VMEM_LIMIT),
    )(q, k, v, bB, gB, states, do)


def loss_head(x, tgt, g, *, tb, name):
    T, D = x.shape
    tb = min(tb, T)
    nt = T // tb

    def body(x_ref, t_ref, g_ref, l_ref, dx_ref, dg_ref):
        tg = t_ref[...]

        def f(xv, gv):
            err = rms(xv, gv) - tg
            return 0.5 * jnp.mean(err * err, axis=-1, keepdims=True)

        rows, vjp = jax.vjp(f, x_ref[...], g_ref[...])
        dx, dg = vjp(jnp.ones_like(rows))
        dx_ref[...] = dx
        tot = jnp.broadcast_to(jnp.sum(rows, axis=0, keepdims=True), (1, 128))

        @pl.when(pl.program_id(0) == 0)
        def _():
            l_ref[...] = tot
            dg_ref[...] = dg

        @pl.when(pl.program_id(0) > 0)
        def _():
            l_ref[...] += tot
            dg_ref[...] += dg

    tok = pl.BlockSpec((tb, D), lambda i: (i, 0))
    return pl.pallas_call(
        body, name=name, grid=(nt,),
        in_specs=[tok, tok, pl.BlockSpec((1, D), lambda i: (0, 0))],
        out_specs=[pl.BlockSpec((1, 128), lambda i: (0, 0)), tok, pl.BlockSpec((1, D), lambda i: (0, 0))],
        out_shape=[jax.ShapeDtypeStruct((1, 128), F32), jax.ShapeDtypeStruct((T, D), F32), jax.ShapeDtypeStruct((1, D), F32)],
        compiler_params=pltpu.CompilerParams(dimension_semantics=("arbitrary",), vmem_limit_bytes=VMEM_LIMIT),
    )(x, tgt, g)


def _row_block(rows, cols):
    budget = 256 * 1024
    tr = max(SUBLANES, min(rows, budget // max(cols, 1)) // SUBLANES * SUBLANES)
    while rows % tr:
        tr -= SUBLANES
        if tr <= 0:
            return rows
    return tr


def adamw(w, g, m, v, *, name):
    R, C = w.shape
    tr = _row_block(R, C)
    c1 = 1.0 / (1.0 - ADAM_B1 ** ADAM_STEP)
    c2 = 1.0 / (1.0 - ADAM_B2 ** ADAM_STEP)

    def body(w_ref, g_ref, m_ref, v_ref, d_ref, mo_ref, vo_ref):
        gv = g_ref[...]
        mn = ADAM_B1 * m_ref[...] + (1.0 - ADAM_B1) * gv
        vn = ADAM_B2 * v_ref[...] + (1.0 - ADAM_B2) * (gv * gv)
        d_ref[...] = -ADAM_LR * ((mn * c1) / (jnp.sqrt(vn * c2) + ADAM_EPS) + ADAM_WD * w_ref[...])
        mo_ref[...] = mn
        vo_ref[...] = vn

    spec = pl.BlockSpec((tr, C), lambda i: (i, 0))
    return pl.pallas_call(
        body, name=name, grid=(R // tr,), in_specs=[spec] * 4, out_specs=[spec] * 3,
        out_shape=[jax.ShapeDtypeStruct((R, C), F32)] * 3,
        compiler_params=pltpu.CompilerParams(dimension_semantics=("parallel",), vmem_limit_bytes=VMEM_LIMIT),
    )(w, g, m, v)


def sum_slabs(x, *, name):
    n, R, C = x.shape
    tr = _row_block(R, C * 2)

    def body(x_ref, o_ref):
        acc = x_ref[0].astype(F32)
        for d in range(1, n):
            acc = acc + x_ref[d].astype(F32)
        o_ref[...] = acc

    return pl.pallas_call(
        body, name=name, grid=(R // tr,),
        in_specs=[pl.BlockSpec((n, tr, C), lambda i: (0, i, 0))],
        out_specs=pl.BlockSpec((tr, C), lambda i: (i, 0)),
        out_shape=jax.ShapeDtypeStruct((R, C), F32),
        compiler_params=pltpu.CompilerParams(dimension_semantics=("parallel",), vmem_limit_bytes=VMEM_LIMIT),
    )(x)


def _exchange(src, *, scatter, name):
    R, C = src.shape[-2:]

    def body(src_ref, out_ref, send_sems, recv_sems, local_sem):
        x, y, c = lax.axis_index("x"), lax.axis_index("y"), lax.axis_index("c")
        me = 4 * x + 2 * y + c

        def peer(k):
            return (x ^ ((k >> 2) & 1), y ^ ((k >> 1) & 1), c ^ (k & 1))

        def peer_index(k):
            px, py, pc = peer(k)
            return 4 * px + 2 * py + pc

        mine = pltpu.make_async_copy(src_ref.at[me] if scatter else src_ref, out_ref.at[me], local_sem)
        mine.start()
        copies = []
        for k in range(1, N_DEV):
            cp = pltpu.make_async_remote_copy(
                src_ref=src_ref.at[peer_index(k)] if scatter else src_ref,
                dst_ref=out_ref.at[me],
                send_sem=send_sems.at[k - 1], recv_sem=recv_sems.at[k - 1],
                device_id=peer(k), device_id_type=pl.DeviceIdType.MESH)
            cp.start()
            copies.append(cp)
        for k in range(1, N_DEV):
            pltpu.make_async_remote_copy(
                src_ref=src_ref.at[peer_index(k)] if scatter else src_ref,
                dst_ref=out_ref.at[peer_index(k)],
                send_sem=send_sems.at[k - 1], recv_sem=recv_sems.at[k - 1],
                device_id=peer(k), device_id_type=pl.DeviceIdType.MESH).wait_recv()
        for cp in copies:
            cp.wait_send()
        mine.wait()

    return pl.pallas_call(
        body, name=name,
        in_specs=[pl.BlockSpec(memory_space=pl.ANY)],
        out_specs=pl.BlockSpec(memory_space=pl.ANY),
        out_shape=jax.ShapeDtypeStruct((N_DEV, R, C), src.dtype),
        scratch_shapes=[pltpu.SemaphoreType.DMA((N_DEV - 1,)), pltpu.SemaphoreType.DMA((N_DEV - 1,)),
                        pltpu.SemaphoreType.DMA],
    )(src)


TB = 256


def st_norm(t0, x, g):
    return (rms(x.astype(F32), g),)


def _conv(xt, w, bias=None):
    kk = w.shape[0]
    acc = bias
    for i in range(kk):
        term = w[i:i + 1, :] * shift_rows(xt, kk - 1 - i)
        acc = term if acc is None else acc + term
    return acc


def st_a(t0, xa, cw, cb, wr, br, wi, bi, lam):
    xc = _conv(xa, cw, cb)
    r = sigmoid(bmm(xc, wr) + br)
    ig = sigmoid(bmm(xc, wi) + bi)
    log_a = -RG_C * r * softplus(-lam)
    row = lax.broadcasted_iota(jnp.int32, xc.shape, 0) + t0
    mult = jnp.where(row == 0, 1.0, jnp.sqrt(-expm1(2.0 * log_a)))
    return jnp.exp(log_a), mult * ig * xc


def st_a_cot(t0, lam, ha):
    return lam * shift_rows(ha, 1), lam


def _heads(x, n, w):
    return [x[:, h * w:(h + 1) * w] for h in range(n)]


def st_b(t0, q, k, v, pba, cw, alog, dtb):
    qc = silu(_conv(q, cw[:, 0:BW]))
    kc = silu(_conv(k, cw[:, BW:2 * BW]))
    vc = silu(_conv(v, cw[:, 2 * BW:3 * BW]))

    def l2n(x):
        return jnp.concatenate([s * lax.rsqrt(jnp.sum(s * s, axis=-1, keepdims=True) + EPS)
                                for s in _heads(x, B_HEADS, B_DK)], axis=-1)

    sg = sigmoid(pba)
    gg = -jnp.exp(alog) * softplus(pba + dtb)
    lane = lax.broadcasted_iota(jnp.int32, pba.shape, 1)

    def spread(x, first):
        return jnp.concatenate(
            [jnp.broadcast_to(jnp.sum(jnp.where(lane == first + h, x, 0.0), axis=-1, keepdims=True), (x.shape[0], B_DK))
             for h in range(B_HEADS)], axis=-1)

    return l2n(qc), l2n(kc), vc, spread(sg, 0), spread(gg, B_HEADS)


def st_m2(t0, ha, ga, o, z, y0, uc, bn4, cd, wglu, bglu):
    ya = ha * gelu(ga)
    yb = jnp.concatenate(
        [oh * lax.rsqrt(jnp.mean(oh * oh, axis=-1, keepdims=True) + EPS) * bh * silu(zh)
         for oh, bh, zh in zip(_heads(o, B_HEADS, B_DK), _heads(bn4, B_HEADS, B_DK), _heads(z, B_HEADS, B_DK))], axis=-1)
    yc0 = gelu(y0 + cd * uc)
    yc = yc0 * sigmoid(bmm(yc0, wglu) + bglu)
    return (jnp.concatenate([ya, yb, yc], axis=-1),)


def st_m2_cot(t0, d0, d1, d2):
    return (jnp.concatenate([d0, d1, d2], axis=-1),)


def st_m3(t0, g0, g1, g2, p0, p1, p2, bg):
    d = D_MODEL
    return (sigmoid(g0 + bg[:, 0:d]) * p0 + sigmoid(g1 + bg[:, d:2 * d]) * p1 + sigmoid(g2 + bg[:, 2 * d:3 * d]) * p2,)


def st_att(t0, q, kv):
    outs = []
    for h in range(X_HEADS):
        qh = q[:, h * X_HD:(h + 1) * X_HD]
        kh = kv[:, h * X_HD:(h + 1) * X_HD]
        vh = kv[:, D_MODEL + h * X_HD:D_MODEL + (h + 1) * X_HD]
        sc = bmm(qh, kh, False, True) * (X_HD ** -0.5)
        e = jnp.exp(sc - lax.stop_gradient(jnp.max(sc, axis=-1, keepdims=True)))
        outs.append(bmm(e / jnp.sum(e, axis=-1, keepdims=True), vh))
    return (jnp.concatenate(outs, axis=-1),)


def st_f2(t0, ug, uv, cwg, cwv, cbg, cbv):
    return (gelu(_conv(ug, cwg, cbg)) * _conv(uv, cwv, cbv),)


def st_s5step(t0, h, ar, ai):
    hp = shift_rows(h, 1)
    hr, hi = hp[:, :S5_CW], hp[:, S5_CW:]
    return (jnp.concatenate([ar * hr - ai * hi, ar * hi + ai * hr], axis=-1),)


W_MAIN = 6 * BW
W_BA = W_MAIN + 2 * B_HEADS
W_UC = W_BA + BW
COL_GATES, COL_UC = W_MAIN, W_MAIN + 3 * D_MODEL


def split_w_in(w):
    big = jnp.concatenate([w[..., :W_MAIN], w[..., W_UC:], w[..., W_BA:W_UC]], axis=-1)
    ba = jnp.pad(w[..., W_MAIN:W_BA], [(0, 0)] * (w.ndim - 1) + [(0, 128 - 2 * B_HEADS)])
    return big, ba


def merge_w_in(big, ba):
    return jnp.concatenate([big[..., :W_MAIN], ba[..., :2 * B_HEADS], big[..., COL_UC:], big[..., COL_GATES:COL_UC]], axis=-1)


def derive(r):
    L = r['a_w_r'].shape[0]
    eye_a = jnp.eye(A_HEADS, dtype=F32)
    eye_g = jnp.eye(C_GROUPS, dtype=F32)

    def blockdiag(w):
        return jnp.einsum('lhij,hg->lhigj', w, eye_a).reshape(L, BW, BW)

    def lanes(v, first):
        return jnp.zeros((L, 1, 128), F32).at[:, 0, first:first + B_HEADS].set(v)

    lr, li = r['c_lam_re'], r['c_lam_im']
    dt = jnp.exp(r['c_log_dt'])[..., None]
    mag = jnp.exp(lr * dt)
    ar, ai = mag * jnp.cos(li * dt), mag * jnp.sin(li * dt)
    den = lr * lr + li * li
    fr = ((ar - 1.0) * lr + ai * li) / den
    fi = (ai * lr - (ar - 1.0) * li) / den
    br, bi = r['c_b_re'], r['c_b_im']
    bbr = fr[..., None] * br - fi[..., None] * bi
    bbi = fr[..., None] * bi + fi[..., None] * br
    ncol = C_CH // S5_CW

    def b_dense(bb):
        return jnp.einsum('lgpc,gh->lgchp', bb, eye_g).reshape(L, BW, ncol, S5_CW)

    bbig = jnp.stack([b_dense(bbr), b_dense(bbi)], axis=3).reshape(L, BW, 2 * C_CH)

    def c_dense(cc):
        return jnp.einsum('lgcp,gh->lgphc', cc, eye_g).reshape(L, ncol, S5_CW, BW)

    cbig = jnp.stack([c_dense(r['c_c_re']), -c_dense(r['c_c_im'])], axis=2).reshape(L, 2 * C_CH, BW)
    return dict(wr=blockdiag(r['a_w_r']), wi=blockdiag(r['a_w_i']),
                alog=lanes(r['b_a_log'], B_HEADS), dtb=lanes(r['b_dt_bias'], B_HEADS),
                bn4=jnp.tile(r['b_norm'], (1, B_HEADS))[:, None, :],
                ar=ar.reshape(L, 1, C_CH), ai=ai.reshape(L, 1, C_CH), bbig=bbig, cbig=cbig)


DERIVE_FROM = ['a_w_r', 'a_w_i', 'b_a_log', 'b_dt_bias', 'b_norm', 'c_lam_re', 'c_lam_im', 'c_log_dt',
               'c_b_re', 'c_b_im', 'c_c_re', 'c_c_im']


def _row(v):
    return v.reshape(1, -1)


def layer_params(w, big, ba, dv, l):
    return dict(
        mix_norm=_row(w['mix_norm'][l]), w_big=big[l], w_ba=ba[l], b_gate=_row(w['b_gate'][l]),
        a_conv_w=w['a_conv_w'][l], a_conv_b=_row(w['a_conv_b'][l]), wr=dv['wr'][l], b_r=_row(w['a_b_r'][l]),
        wi=dv['wi'][l], b_i=_row(w['a_b_i'][l]), lam=_row(w['a_lam'][l]),
        b_conv_w=w['b_conv_w'][l], alog=dv['alog'][l], dtb=dv['dtb'][l], bn4=dv['bn4'][l],
        ar=dv['ar'][l], ai=dv['ai'][l], bbig=dv['bbig'][l], cbig=dv['cbig'][l],
        c_d=_row(w['c_d'][l]), wglu=w['c_glu_w'][l].astype(F32), bglu=_row(w['c_glu_b'][l]),
        w_branch=w['w_branch'][l], w_out=w['w_out'][l],
        xa_norm=_row(w['xa_norm'][l]), mem_norm=_row(w['mem_norm'][l]),
        w_q=w['xa_w_q'][l], w_kv=w['xa_w_kv'][l], w_o=w['xa_w_o'][l],
        ffn_norm=_row(w['ffn_norm'][l]), w_up=w['ffn_w_up'][l], ffn_conv_w=w['ffn_conv_w'][l],
        ffn_conv_b=_row(w['ffn_conv_b'][l]), w_down=w['ffn_w_down'][l])


def _a_pars(p):
    return [Par(p['a_conv_w']), Par(p['a_conv_b']), Par(p['wr']), Par(p['b_r']), Par(p['wi']), Par(p['b_i']), Par(p['lam'])]


def _f2_pars(p):
    d = D_MODEL
    return [Par(p['ffn_conv_w'], 'col', 0, d), Par(p['ffn_conv_w'], 'col', 3, d),
            Par(p['ffn_conv_b'], 'col', 0, d), Par(p['ffn_conv_b'], 'col', 3, d)]


def layer_fwd(x, mem, p, seq, mseq, l):
    d = D_MODEL
    nb = x.shape[0] // seq
    kw = dict(seq=seq, tb=TB)
    n = lambda s: f"{s}_l{l}"
    h, = stage_fwd(st_norm, [Tok(x, d)], [Par(p['mix_norm'])], [(d, MXU_DTYPE)], name=n("norm_mix"), **kw)
    P = mm(h, p['w_big'], name=n("mm_in"))
    Pba = mm(h, p['w_ba'], name=n("mm_in_ba"))
    a, bb = stage_fwd(st_a, [Tok(P, BW, 0, True)], _a_pars(p), [(BW, F32)] * 2, name=n("rglru_pre"), **kw)
    ha = scan_real(a, bb, seq=seq, reverse=False, name=n("rglru_scan"))
    qn, kn, vv, bB, gB = stage_fwd(
        st_b, [Tok(P, BW, 2, True), Tok(P, BW, 3, True), Tok(P, BW, 4, True), Tok(Pba, 128)],
        [Par(p['b_conv_w']), Par(p['alog']), Par(p['dtb'])], [(BW, F32)] * 5, name=n("delta_pre"), **kw)
    r3 = lambda t: t.reshape(nb, seq, BW)
    o3, states = delta_fwd(r3(qn), r3(kn), r3(vv), r3(bB), r3(gB), name=n("delta"))
    o = o3.reshape(-1, BW)
    bu = mm(P, p['bbig'], a_cols=(COL_UC, BW), name=n("mm_s5_in"))
    hs = scan_cplx(p['ar'], p['ai'], bu, seq=seq, reverse=False, name=n("s5_scan"))
    y0 = mm(hs, p['cbig'], name=n("mm_s5_out"))
    m2_toks = [Tok(ha, BW), Tok(P, BW, 1), Tok(o, BW), Tok(P, BW, 5), Tok(y0, BW), Tok(P, BW, COL_UC // BW)]
    m2_pars = [Par(p['bn4']), Par(p['c_d']), Par(p['wglu']), Par(p['bglu'])]
    Y3, = stage_fwd(st_m2, m2_toks, m2_pars, [(3 * BW, MXU_DTYPE)], name=n("branches"), **kw)
    proj = [mm(Y3, p['w_branch'][k], a_cols=(k * BW, BW), name=n(f"mm_branch{k}")) for k in range(3)]
    g0 = COL_GATES // d
    m3_toks = [Tok(P, d, g0), Tok(P, d, g0 + 1), Tok(P, d, g0 + 2)] + [Tok(t, d) for t in proj]
    mixed, = stage_fwd(st_m3, m3_toks, [Par(p['b_gate'])], [(d, MXU_DTYPE)], name=n("gate_mix"), **kw)
    x1 = mm(mixed, p['w_out'], residual=x, name=n("mm_out"))
    hx, = stage_fwd(st_norm, [Tok(x1, d)], [Par(p['xa_norm'])], [(d, MXU_DTYPE)], name=n("norm_xa"), **kw)
    mn, = stage_fwd(st_norm, [Tok(mem, d)], [Par(p['mem_norm'])], [(d, MXU_DTYPE)], seq=mseq, tb=TB, name=n("norm_mem"))
    qx = mm(hx, p['w_q'], name=n("mm_q"))
    kv = mm(mn, p['w_kv'], name=n("mm_kv"))
    kv3 = kv.reshape(nb, mseq, 2 * d)
    ox, = stage_fwd(st_att, [Tok(qx, d)], [Par(kv3, 'batch')], [(d, MXU_DTYPE)], name=n("attention"), **kw)
    x2 = mm(ox, p['w_o'], residual=x1, name=n("mm_o"))
    hf, = stage_fwd(st_norm, [Tok(x2, d)], [Par(p['ffn_norm'])], [(d, MXU_DTYPE)], name=n("norm_ffn"), **kw)
    U = mm(hf, p['w_up'], name=n("mm_up"))
    act, = stage_fwd(st_f2, [Tok(U, d, 0, True), Tok(U, d, 3, True)], _f2_pars(p), [(d, MXU_DTYPE)], ncol=3,
                     name=n("ffn_act"), **kw)
    x3 = mm(act, p['w_down'], residual=x2, name=n("mm_down"))
    sv = dict(x=x, h=h, P=P, Pba=Pba, a=a, ha=ha, qn=qn, kn=kn, vv=vv, bB=bB, gB=gB, states=states, o=o, hs=hs, y0=y0,
              Y3=Y3, proj=proj, mixed=mixed, x1=x1, hx=hx, mn=mn, qx=qx, kv3=kv3, ox=ox, x2=x2, hf=hf, U=U, act=act)
    return x3, sv


def layer_bwd(dx3, mem, p, sv, seq, mseq, l):
    d = D_MODEL
    nb = dx3.shape[0] // seq
    kw = dict(seq=seq, tb=TB)
    n = lambda s: f"{s}_l{l}"
    g = {}
    P, Pba = sv['P'], sv['Pba']
    dact = mm(dx3, p['w_down'], tb=True, name=n("bmm_down_x"))
    g['w_down'] = mm(sv['act'], dx3, ta=True, name=n("bmm_down_w"))
    (dUg, dUv), gp = stage_bwd(st_f2, [Tok(sv['U'], d, 0, True, MXU_DTYPE), Tok(sv['U'], d, 3, True, MXU_DTYPE)], _f2_pars(p),
                               [Tok(dact, d)], ncol=3, name=n("b_ffn_act"), **kw)
    g['ffn_conv_w'] = jnp.concatenate([gp[0], gp[1]], axis=1)
    g['ffn_conv_b'] = jnp.concatenate([gp[2], gp[3]], axis=1)
    dU = jnp.concatenate([dUg, dUv], axis=1)
    dhf = mm(dU, p['w_up'], tb=True, name=n("bmm_up_x"))
    g['w_up'] = mm(sv['hf'], dU, ta=True, name=n("bmm_up_w"))
    (dx2,), (g['ffn_norm'],) = stage_bwd(st_norm, [Tok(sv['x2'], d, grad=F32, add=dx3)], [Par(p['ffn_norm'])],
                                         [Tok(dhf, d)], name=n("b_norm_ffn"), **kw)
    dox = mm(dx2, p['w_o'], tb=True, name=n("bmm_o_x"))
    g['w_o'] = mm(sv['ox'], dx2, ta=True, name=n("bmm_o_w"))
    (dqx,), (dkv3,) = stage_bwd(st_att, [Tok(sv['qx'], d, grad=MXU_DTYPE)], [Par(sv['kv3'], 'batch')], [Tok(dox, d)],
                                name=n("b_attention"), **kw)
    dkv = dkv3.reshape(-1, 2 * d)
    dhx = mm(dqx, p['w_q'], tb=True, name=n("bmm_q_x"))
    g['w_q'] = mm(sv['hx'], dqx, ta=True, name=n("bmm_q_w"))
    dmn = mm(dkv, p['w_kv'], tb=True, name=n("bmm_kv_x"))
    g['w_kv'] = mm(sv['mn'], dkv, ta=True, name=n("bmm_kv_w"))
    _, (g['mem_norm'],) = stage_bwd(st_norm, [Tok(mem, d)], [Par(p['mem_norm'])], [Tok(dmn, d)], seq=mseq, tb=TB,
                                    name=n("b_norm_mem"))
    (dx1,), (g['xa_norm'],) = stage_bwd(st_norm, [Tok(sv['x1'], d, grad=F32, add=dx2)], [Par(p['xa_norm'])],
                                        [Tok(dhx, d)], name=n("b_norm_xa"), **kw)
    dmixed = mm(dx1, p['w_out'], tb=True, name=n("bmm_out_x"))
    g['w_out'] = mm(sv['mixed'], dx1, ta=True, name=n("bmm_out_w"))
    g0 = COL_GATES // d
    m3_toks = [Tok(P, d, g0 + k, grad=MXU_DTYPE) for k in range(3)] + [Tok(t, d, grad=MXU_DTYPE) for t in sv['proj']]
    dm3, (g['b_gate'],) = stage_bwd(st_m3, m3_toks, [Par(p['b_gate'])], [Tok(dmixed, d)], name=n("b_gate_mix"), **kw)
    dgates, dproj = dm3[:3], dm3[3:]
    dY = [mm(dproj[k], p['w_branch'][k], tb=True, name=n(f"bmm_branch{k}_x")) for k in range(3)]
    g['w_branch'] = jnp.stack([mm(sv['Y3'], dproj[k], ta=True, a_cols=(k * BW, BW), name=n(f"bmm_branch{k}_w"))
                               for k in range(3)])
    m2_toks = [Tok(sv['ha'], BW, grad=F32), Tok(P, BW, 1, grad=MXU_DTYPE), Tok(sv['o'], BW, grad=F32), Tok(P, BW, 5, grad=MXU_DTYPE),
               Tok(sv['y0'], BW, grad=MXU_DTYPE), Tok(P, BW, COL_UC // BW, grad=F32)]
    m2_pars = [Par(p['bn4']), Par(p['c_d']), Par(p['wglu']), Par(p['bglu'])]
    (dha, dga, do, dz, dy0, duc0), (g['bn4'], g['c_d'], g['wglu'], g['bglu']) = stage_bwd(
        st_m2, m2_toks, m2_pars, [Tok(t, BW) for t in dY], cot_fn=st_m2_cot, name=n("b_branches"), **kw)
    dhs = mm(dy0, p['cbig'], tb=True, name=n("bmm_s5_out_x"))
    g['cbig'] = mm(sv['hs'], dy0, ta=True, name=n("bmm_s5_out_w"))
    lam_s = scan_cplx(p['ar'], -p['ai'], dhs, seq=seq, reverse=True, name=n("b_s5_scan"))
    _, (g['ar'], g['ai']) = stage_bwd(st_s5step, [Tok(sv['hs'], 2 * S5_CW, 0, True)],
                                      [Par(p['ar'], 'col', 0, S5_CW), Par(p['ai'], 'col', 0, S5_CW)],
                                      [Tok(lam_s, 2 * S5_CW)], ncol=C_CH // S5_CW, name=n("b_s5_decay"), **kw)
    duc = mm(lam_s, p['bbig'], tb=True, residual=duc0, out_dtype=MXU_DTYPE, name=n("bmm_s5_in_x"))
    g['bbig'] = mm(P, lam_s, ta=True, a_cols=(COL_UC, BW), name=n("bmm_s5_in_w"))
    lam_a = scan_real(sv['a'], dha, seq=seq, reverse=True, name=n("b_rglru_scan"))
    (dxa,), ga = stage_bwd(st_a, [Tok(P, BW, 0, True, MXU_DTYPE)], _a_pars(p), [Tok(lam_a, BW), Tok(sv['ha'], BW, 0, True)],
                           cot_fn=st_a_cot, name=n("b_rglru_pre"), **kw)
    g['a_conv_w'], g['a_conv_b'], g['wr'], g['b_r'], g['wi'], g['b_i'], g['lam'] = ga
    r3 = lambda t: t.reshape(nb, seq, BW)
    dd = delta_bwd(r3(sv['qn']), r3(sv['kn']), r3(sv['vv']), r3(sv['bB']), r3(sv['gB']), sv['states'], r3(do),
                   name=n("b_delta"))
    (dq, dk, dv, dpba), (g['b_conv_w'], g['alog'], g['dtb']) = stage_bwd(
        st_b, [Tok(P, BW, 2, True, MXU_DTYPE), Tok(P, BW, 3, True, MXU_DTYPE), Tok(P, BW, 4, True, MXU_DTYPE), Tok(Pba, 128, grad=MXU_DTYPE)],
        [Par(p['b_conv_w']), Par(p['alog']), Par(p['dtb'])], [Tok(t.reshape(-1, BW), BW) for t in dd],
        name=n("b_delta_pre"), **kw)
    dP = jnp.concatenate([dxa, dga, dq, dk, dv, dz] + list(dgates) + [duc], axis=1)
    dh0 = mm(dpba, p['w_ba'], tb=True, name=n("bmm_in_ba_x"))
    dh = mm(dP, p['w_big'], tb=True, residual=dh0, name=n("bmm_in_x"))
    g['w_big'] = mm(sv['h'], dP, ta=True, name=n("bmm_in_w"))
    g['w_ba'] = mm(sv['h'], dpba, ta=True, name=n("bmm_in_ba_w"))
    (dx,), (g['mix_norm'],) = stage_bwd(st_norm, [Tok(sv['x'], d, grad=F32, add=dx1)], [Par(p['mix_norm'])],
                                        [Tok(dh, d)], name=n("b_norm_mix"), **kw)
    return dx, g


def local_step(x3d, mem3d, tgt3d, w, final_norm):
    nb, seq, d = x3d.shape
    mseq = mem3d.shape[1]
    x = x3d.reshape(nb * seq, d)
    mem = mem3d.reshape(nb * mseq, d)
    L = w['mix_norm'].shape[0]
    big, ba = split_w_in(w['w_in'])
    dv, dv_vjp = jax.vjp(derive, {k: w[k] for k in DERIVE_FROM})
    ps, svs = [], []
    for l in range(L):
        p = layer_params(w, big, ba, dv, l)
        x, sv = layer_fwd(x, mem, p, seq, mseq, l)
        ps.append(p)
        svs.append(sv)
    loss, dx, g_final = loss_head(x, tgt3d.reshape(nb * seq, d), _row(final_norm), tb=TB, name="loss_head")
    gs = [None] * L
    for l in reversed(range(L)):
        dx, gs[l] = layer_bwd(dx, mem, ps[l], svs[l], seq, mseq, l)
    st = lambda k: jnp.stack([gs[l][k] for l in range(L)])
    vec = lambda k: st(k).reshape(L, -1)
    gd = dv_vjp({k: st(k) for k in ('wr', 'wi', 'alog', 'dtb', 'bn4', 'ar', 'ai', 'bbig', 'cbig')})[0]
    out = dict(gd)
    out.update(
        mix_norm=vec('mix_norm'), w_in=merge_w_in(st('w_big'), st('w_ba')), b_gate=vec('b_gate'),
        a_conv_w=st('a_conv_w'), a_conv_b=vec('a_conv_b'), a_b_r=vec('b_r'), a_b_i=vec('b_i'), a_lam=vec('lam'),
        b_conv_w=st('b_conv_w'), c_d=vec('c_d'), c_glu_w=st('wglu'), c_glu_b=vec('bglu'),
        w_branch=st('w_branch'), w_out=st('w_out'), xa_norm=vec('xa_norm'), mem_norm=vec('mem_norm'),
        xa_w_q=st('w_q'), xa_w_kv=st('w_kv'), xa_w_o=st('w_o'), ffn_norm=vec('ffn_norm'), ffn_w_up=st('w_up'),
        ffn_conv_w=st('ffn_conv_w'), ffn_conv_b=vec('ffn_conv_b'), ffn_w_down=st('w_down'),
        final_norm=g_final.reshape(-1))
    return loss, dx.reshape(nb, seq, d), out


LANES = 1024
ROW_PAD = 256


def _pack(vecs, dtype):
    flat = jnp.concatenate([v.reshape(-1).astype(dtype) for v in vecs])
    rows = -(-flat.shape[0] // (LANES * ROW_PAD)) * ROW_PAD
    return jnp.pad(flat, (0, rows * LANES - flat.shape[0])).reshape(rows, LANES)


def _pack_rows(mats, dtype):
    flat = jnp.concatenate([m.astype(dtype) for m in mats], axis=1)
    rows = -(-flat.shape[1] // (LANES * ROW_PAD)) * ROW_PAD
    return jnp.pad(flat, ((0, 0), (0, rows * LANES - flat.shape[1]))).reshape(N_DEV, rows, LANES)


def _unpack(flat, shapes):
    lead = flat.shape[:-2]
    flat = flat.reshape(lead + (-1,))
    out, off = [], 0
    for shp in shapes:
        cnt = int(np.prod(shp))
        out.append(flat[..., off:off + cnt].reshape(lead + tuple(shp)))
        off += cnt
    return out


def _join_shards(stacked, axis):
    t = jnp.moveaxis(stacked, 0, axis)
    shp = list(t.shape)
    return t.reshape(shp[:axis] + [shp[axis] * shp[axis + 1]] + shp[axis + 2:])


def _cut_shards(full, axis):
    shp = list(full.shape)
    t = full.reshape(shp[:axis] + [N_DEV, shp[axis] // N_DEV] + shp[axis + 1:])
    return jnp.moveaxis(t, axis, 0).reshape(N_DEV, -1)


def _as2d(a):
    return a.reshape(-1, a.shape[-1])


def kernel(x, mem, mix_norm, w_in, b_gate, a_conv_w, a_conv_b, a_w_r, a_b_r, a_w_i, a_b_i, a_lam, b_conv_w, b_a_log, b_dt_bias, b_norm, c_lam_re, c_lam_im, c_log_dt, c_b_re, c_b_im, c_c_re, c_c_im, c_d, c_glu_w, c_glu_b, w_branch, w_out, xa_norm, mem_norm, xa_w_q, xa_w_kv, xa_w_o, ffn_norm, ffn_w_up, ffn_conv_w, ffn_conv_b, ffn_w_down, final_norm, loss_target, m_mix_norm, m_w_in, m_b_gate, m_a_conv_w, m_a_conv_b, m_a_w_r, m_a_b_r, m_a_w_i, m_a_b_i, m_a_lam, m_b_conv_w, m_b_a_log, m_b_dt_bias, m_b_norm, m_c_lam_re, m_c_lam_im, m_c_log_dt, m_c_b_re, m_c_b_im, m_c_c_re, m_c_c_im, m_c_d, m_c_glu_w, m_c_glu_b, m_w_branch, m_w_out, m_xa_norm, m_mem_norm, m_xa_w_q, m_xa_w_kv, m_xa_w_o, m_ffn_norm, m_ffn_w_up, m_ffn_conv_w, m_ffn_conv_b, m_ffn_w_down, m_final_norm, v_mix_norm, v_w_in, v_b_gate, v_a_conv_w, v_a_conv_b, v_a_w_r, v_a_b_r, v_a_w_i, v_a_b_i, v_a_lam, v_b_conv_w, v_b_a_log, v_b_dt_bias, v_b_norm, v_c_lam_re, v_c_lam_im, v_c_log_dt, v_c_b_re, v_c_b_im, v_c_c_re, v_c_c_im, v_c_d, v_c_glu_w, v_c_glu_b, v_w_branch, v_w_out, v_xa_norm, v_mem_norm, v_xa_w_q, v_xa_w_kv, v_xa_w_o, v_ffn_norm, v_ffn_w_up, v_ffn_conv_w, v_ffn_conv_b, v_ffn_w_down, v_final_norm):
    args = (x, mem, mix_norm, w_in, b_gate, a_conv_w, a_conv_b, a_w_r, a_b_r, a_w_i, a_b_i, a_lam, b_conv_w, b_a_log, b_dt_bias, b_norm, c_lam_re, c_lam_im, c_log_dt, c_b_re, c_b_im, c_c_re, c_c_im, c_d, c_glu_w, c_glu_b, w_branch, w_out, xa_norm, mem_norm, xa_w_q, xa_w_kv, xa_w_o, ffn_norm, ffn_w_up, ffn_conv_w, ffn_conv_b, ffn_w_down, final_norm, loss_target, m_mix_norm, m_w_in, m_b_gate, m_a_conv_w, m_a_conv_b, m_a_w_r, m_a_b_r, m_a_w_i, m_a_b_i, m_a_lam, m_b_conv_w, m_b_a_log, m_b_dt_bias, m_b_norm, m_c_lam_re, m_c_lam_im, m_c_log_dt, m_c_b_re, m_c_b_im, m_c_c_re, m_c_c_im, m_c_d, m_c_glu_w, m_c_glu_b, m_w_branch, m_w_out, m_xa_norm, m_mem_norm, m_xa_w_q, m_xa_w_kv, m_xa_w_o, m_ffn_norm, m_ffn_w_up, m_ffn_conv_w, m_ffn_conv_b, m_ffn_w_down, m_final_norm, v_mix_norm, v_w_in, v_b_gate, v_a_conv_w, v_a_conv_b, v_a_w_r, v_a_b_r, v_a_w_i, v_a_b_i, v_a_lam, v_b_conv_w, v_b_a_log, v_b_dt_bias, v_b_norm, v_c_lam_re, v_c_lam_im, v_c_log_dt, v_c_b_re, v_c_b_im, v_c_c_re, v_c_c_im, v_c_d, v_c_glu_w, v_c_glu_b, v_w_branch, v_w_out, v_xa_norm, v_mem_norm, v_xa_w_q, v_xa_w_kv, v_xa_w_o, v_ffn_norm, v_ffn_w_up, v_ffn_conv_w, v_ffn_conv_b, v_ffn_w_down, v_final_norm)
    nw = len(WEIGHTS)
    x, mem = args[0], args[1]
    w_loc = dict(zip(WEIGHTS, args[2:2 + nw]))
    tgt = args[2 + nw]
    m_loc = dict(zip(WEIGHTS, args[3 + nw:3 + 2 * nw]))
    v_loc = dict(zip(WEIGHTS, args[3 + 2 * nw:3 + 3 * nw]))
    me = 4 * lax.axis_index("x") + 2 * lax.axis_index("y") + lax.axis_index("c")

    big_shapes = [w_loc[n].shape for n in BIG]
    gathered = _exchange(_pack([w_loc[n] for n in BIG], BF), scatter=False, name="gather_matmul_weights")
    w = dict(w_loc)
    for n, st in zip(BIG, _unpack(gathered, big_shapes)):
        w[n] = _join_shards(st, SHARD_AXIS[n])
    ss_shapes = [w_loc[n].shape for n in SMALL_SHARDED]
    gathered_s = _exchange(_pack([w_loc[n] for n in SMALL_SHARDED], F32), scatter=False, name="gather_conv_weights")
    for n, st in zip(SMALL_SHARDED, _unpack(gathered_s, ss_shapes)):
        w[n] = _join_shards(st, SHARD_AXIS[n])

    final_norm = w.pop('final_norm')
    loss, grad_x, g = local_step(x, mem, tgt, w, final_norm)

    send = _pack_rows([_cut_shards(g[n], SHARD_AXIS[n]) for n in BIG], BF)
    recv = _exchange(send, scatter=True, name="scatter_matmul_grads")
    g_big = dict(zip(BIG, _unpack(sum_slabs(recv, name="sum_matmul_grads"), big_shapes)))
    small_full_shapes = [g[n].shape for n in SMALL]
    packed = _pack([g[n] for n in SMALL] + [loss[0, :1]], F32)
    everyones = _exchange(packed, scatter=False, name="gather_small_grads")
    summed = _unpack(sum_slabs(everyones, name="sum_small_grads"), small_full_shapes + [(1,)])
    loss_total = summed[-1].reshape(())
    g_small = {}
    for n, full in zip(SMALL, summed[:-1]):
        full = full.reshape(small_full_shapes[SMALL.index(n)])
        if n in SMALL_SHARDED:
            ax = SHARD_AXIS[n]
            loc = w_loc[n].shape[ax]
            full = lax.dynamic_slice_in_dim(full, me * loc, loc, axis=ax)
        g_small[n] = full.reshape(w_loc[n].shape)

    grads, delta, new_m, new_v = {}, {}, {}, {}
    for n in BIG:
        shp = w_loc[n].shape
        grads[n] = g_big[n]
        dl, nm, nv = adamw(_as2d(w_loc[n]), _as2d(g_big[n]), _as2d(m_loc[n]), _as2d(v_loc[n]), name=f"adamw_{n}")
        delta[n], new_m[n], new_v[n] = dl.reshape(shp), nm.reshape(shp), nv.reshape(shp)
    small_shapes = [w_loc[n].shape for n in SMALL]
    flat = [_pack([d[n] for n in SMALL], F32) for d in (w_loc, g_small, m_loc, v_loc)]
    res = adamw(*flat, name="adamw_small")
    for d, r in zip((delta, new_m, new_v), res):
        for n, a in zip(SMALL, _unpack(r, small_shapes)):
            d[n] = a
    grads.update(g_small)
    return (loss_total, grad_x, *[grads[n] for n in WEIGHTS], *[delta[n] for n in WEIGHTS],
            *[new_m[n] for n in WEIGHTS], *[new_v[n] for n in WEIGHTS])
```

```python
import functools
from typing import Any, NamedTuple

import jax
import jax.numpy as jnp
import numpy as np
from jax import lax
from jax.experimental import pallas as pl
from jax.experimental.pallas import tpu as pltpu

F32 = jnp.float32
BF = jnp.bfloat16
MXU_DTYPE = BF

EPS = 1e-6
RG_C = 8.0
N_DEV = 8
DEPTH = 4
D_MODEL = 1024
BW = 512
A_HEADS, A_HD = 8, 64
B_HEADS, B_DK = 4, 128
B_CHUNK = 64
C_GROUPS, C_GROUP, C_STATE = 32, 16, 64
C_CH = C_GROUPS * C_STATE
S5_CW = 512
X_HEADS, X_HD = 4, 256
D_FF = 3 * D_MODEL
ADAM_LR, ADAM_B1, ADAM_B2, ADAM_EPS, ADAM_WD, ADAM_STEP = 0.001, 0.9, 0.999, 1e-08, 0.01, 10

SUBLANES = 8
VMEM_LIMIT = 56 * 1024 * 1024

WEIGHTS = ['mix_norm', 'w_in', 'b_gate', 'a_conv_w', 'a_conv_b', 'a_w_r', 'a_b_r', 'a_w_i', 'a_b_i', 'a_lam',
           'b_conv_w', 'b_a_log', 'b_dt_bias', 'b_norm', 'c_lam_re', 'c_lam_im', 'c_log_dt', 'c_b_re', 'c_b_im',
           'c_c_re', 'c_c_im', 'c_d', 'c_glu_w', 'c_glu_b', 'w_branch', 'w_out', 'xa_norm', 'mem_norm', 'xa_w_q',
           'xa_w_kv', 'xa_w_o', 'ffn_norm', 'ffn_w_up', 'ffn_conv_w', 'ffn_conv_b', 'ffn_w_down', 'final_norm']
SHARD_AXIS = {'w_in': 2, 'a_conv_w': 2, 'b_conv_w': 2, 'c_glu_w': 1, 'w_branch': 3, 'w_out': 1, 'xa_w_q': 1,
              'xa_w_kv': 2, 'xa_w_o': 1, 'ffn_w_up': 2, 'ffn_conv_w': 2, 'ffn_w_down': 1}
BIG = ['w_in', 'c_glu_w', 'w_branch', 'w_out', 'xa_w_q', 'xa_w_kv', 'xa_w_o', 'ffn_w_up', 'ffn_w_down']
SMALL_SHARDED = ['a_conv_w', 'b_conv_w', 'ffn_conv_w']
SMALL = [n for n in WEIGHTS if n not in BIG]


def _dot(x, y, tx, ty):
    cx = 0 if tx else 1
    cy = 1 if ty else 0
    return lax.dot_general(x.astype(MXU_DTYPE), y.astype(MXU_DTYPE), (((cx,), (cy,)), ((), ())),
                           preferred_element_type=F32)


@functools.partial(jax.custom_vjp, nondiff_argnums=(2, 3))
def bmm(a, b, ta=False, tb=False):
    return _dot(a, b, ta, tb)


def _bmm_fwd(a, b, ta, tb):
    return _dot(a, b, ta, tb), (a, b)


def _bmm_bwd(ta, tb, res, g):
    a, b = res
    da = _dot(b, g, tb, True) if ta else _dot(g, b, False, not tb)
    db = _dot(g, a, True, ta) if tb else _dot(a, g, not ta, False)
    return da.astype(a.dtype), db.astype(b.dtype)


bmm.defvjp(_bmm_fwd, _bmm_bwd)


def _dotx(x, y, tx, ty):
    cx = 0 if tx else 1
    cy = 1 if ty else 0

    def d(p, q):
        return lax.dot_general(p, q, (((cx,), (cy,)), ((), ())), preferred_element_type=F32)

    xh, yh = x.astype(BF), y.astype(BF)
    xl, yl = (x - xh.astype(F32)).astype(BF), (y - yh.astype(F32)).astype(BF)
    return d(xh, yh) + (d(xh, yl) + d(xl, yh))


@functools.partial(jax.custom_vjp, nondiff_argnums=(2, 3))
def xmm(a, b, ta=False, tb=False):
    return _dotx(a, b, ta, tb)


def _xmm_fwd(a, b, ta, tb):
    return _dotx(a, b, ta, tb), (a, b)


def _xmm_bwd(ta, tb, res, g):
    a, b = res
    da = _dotx(b, g, tb, True) if ta else _dotx(g, b, False, not tb)
    db = _dotx(g, a, True, ta) if tb else _dotx(a, g, not ta, False)
    return da, db


xmm.defvjp(_xmm_fwd, _xmm_bwd)


@functools.partial(jax.custom_vjp, nondiff_argnums=(1,))
def roll_rows(x, s):
    return pltpu.roll(x, s, 0)


def _roll_rows_fwd(x, s):
    return pltpu.roll(x, s, 0), None


def _roll_rows_bwd(s, _, g):
    return (pltpu.roll(g, (g.shape[0] - s) % g.shape[0], 0),)


roll_rows.defvjp(_roll_rows_fwd, _roll_rows_bwd)


def shift_rows(cur_tail, s):
    cur, tail = cur_tail
    if s == 0:
        return cur
    rolled = roll_rows(cur, s)
    row = lax.broadcasted_iota(jnp.int32, tail.shape, 0)
    top = jnp.where(row < s, roll_rows(tail, s), rolled[:SUBLANES])
    if cur.shape[0] == SUBLANES:
        return top
    return jnp.concatenate([top, rolled[SUBLANES:]], axis=0)


def softplus(x):
    return jnp.maximum(x, 0.0) + jnp.log(1.0 + jnp.exp(-jnp.abs(x)))


def expm1(x):
    series = x * (1.0 + x * (0.5 + x * (1.0 / 6.0 + x * (1.0 / 24.0 + x * (1.0 / 120.0)))))
    return jnp.where(jnp.abs(x) < 0.05, series, jnp.exp(x) - 1.0)


def sigmoid(x):
    return 1.0 / (1.0 + jnp.exp(-x))


def silu(x):
    return x * sigmoid(x)


def gelu(x):
    return 0.5 * x * (1.0 + jnp.tanh(0.7978845608028654 * (x + 0.044715 * (x * x * x))))


def rms(x, g):
    var = jnp.mean(x * x, axis=-1, keepdims=True)
    return x * lax.rsqrt(var + EPS) * g


def cumsum_rows(x):
    n = x.shape[0]
    row = lax.broadcasted_iota(jnp.int32, x.shape, 0)
    s = 1
    while s < n:
        x = x + jnp.where(row >= s, roll_rows(x, s), 0.0)
        s *= 2
    return x


def _pick(n, prefs):
    for p in prefs:
        if n % p == 0:
            return p
    return n


def mm(a, b, *, ta=False, tb=False, a_cols=None, b_cols=None, residual=None, out_dtype=F32, name):
    a0, aw = a_cols if a_cols is not None else (0, a.shape[1])
    b0, bw = b_cols if b_cols is not None else (0, b.shape[1])
    if ta:
        K, M = a.shape[0], aw
    else:
        M, K = a.shape[0], aw
    if tb:
        N, Kb = b.shape[0], bw
    else:
        Kb, N = b.shape[0], bw
    assert K == Kb, (name, a.shape, b.shape, ta, tb)
    tm = _pick(M, (1024, 512, 256, 128))
    tk = _pick(K, (2048, 1024, 3328, 512, 256, 128))
    tn = _pick(N, (1024, 512, 256, 128)) if tk <= 1024 else _pick(N, (512, 256, 128))
    nk = K // tk

    def off(c0, t):
        assert c0 % t == 0, (name, c0, t)
        return c0 // t

    if ta:
        a_spec = pl.BlockSpec((tk, tm), lambda i, j, k, o=off(a0, tm): (k, i + o))
    else:
        a_spec = pl.BlockSpec((tm, tk), lambda i, j, k, o=off(a0, tk): (i, k + o))
    if tb:
        b_spec = pl.BlockSpec((tn, tk), lambda i, j, k, o=off(b0, tk): (j, k + o))
    else:
        b_spec = pl.BlockSpec((tk, tn), lambda i, j, k, o=off(b0, tn): (k, j + o))
    o_spec = pl.BlockSpec((tm, tn), lambda i, j, k: (i, j))
    in_specs = [a_spec, b_spec]
    args = [a, b]
    if residual is not None:
        in_specs.append(o_spec)
        args.append(residual)

    def body(*refs):
        a_ref, b_ref = refs[0], refs[1]
        r_ref = refs[2] if residual is not None else None
        o_ref = refs[3] if residual is not None else refs[2]
        part = _dot(a_ref[...], b_ref[...], ta, tb)

        def finish(acc):
            if r_ref is not None:
                acc = acc + r_ref[...].astype(F32)
            o_ref[...] = acc.astype(out_dtype)

        if nk == 1:
            finish(part)
        else:
            acc_ref = refs[-1]
            k = pl.program_id(2)

            @pl.when(k == 0)
            def _():
                acc_ref[...] = part

            @pl.when(k > 0)
            def _():
                acc_ref[...] += part

            @pl.when(k == nk - 1)
            def _():
                finish(acc_ref[...])

    return pl.pallas_call(
        body, name=name, grid=(M // tm, N // tn, nk),
        in_specs=in_specs, out_specs=o_spec,
        out_shape=jax.ShapeDtypeStruct((M, N), out_dtype),
        scratch_shapes=[pltpu.VMEM((tm, tn), F32)] if nk > 1 else [],
        compiler_params=pltpu.CompilerParams(dimension_semantics=("parallel", "parallel", "arbitrary"),
                                             vmem_limit_bytes=VMEM_LIMIT),
    )(*args)


class Tok(NamedTuple):
    arr: Any
    width: int
    col: int = 0
    halo: bool = False
    grad: Any = None
    add: Any = None


class Par(NamedTuple):
    arr: Any
    kind: str = 'const'
    col: int = 0
    width: int = 0
    grad: bool = True


def _tok_specs(toks, tb, rev, ntile):
    specs, args = [], []
    for t in toks:
        if rev:
            cur = lambda j, s, c=t.col: (ntile - 1 - s, c + j)
            tail = lambda j, s, c=t.col: (jnp.maximum((ntile - 1 - s) * (tb // SUBLANES) - 1, 0), c + j)
        else:
            cur = lambda j, s, c=t.col: (s, c + j)
            tail = lambda j, s, c=t.col: (jnp.maximum(s * (tb // SUBLANES) - 1, 0), c + j)
        specs.append(pl.BlockSpec((tb, t.width), cur))
        args.append(t.arr)
        if t.halo:
            specs.append(pl.BlockSpec((SUBLANES, t.width), tail))
            args.append(t.arr)
    return specs, args


def _par_specs(pars, tpb, rev, ntile):
    specs, args = [], []
    for p in pars:
        if p.kind == 'const':
            specs.append(pl.BlockSpec(p.arr.shape, lambda j, s: (0, 0)))
        elif p.kind == 'col':
            specs.append(pl.BlockSpec((p.arr.shape[0], p.width), lambda j, s, c=p.col: (0, c + j)))
        else:
            if rev:
                specs.append(pl.BlockSpec((None,) + p.arr.shape[1:], lambda j, s: ((ntile - 1 - s) // tpb, 0, 0)))
            else:
                specs.append(pl.BlockSpec((None,) + p.arr.shape[1:], lambda j, s: (s // tpb, 0, 0)))
        args.append(p.arr)
    return specs, args


def _read_toks(toks, refs, t0):
    vals, k = [], 0
    for t in toks:
        cur = refs[k][...]
        k += 1
        if t.halo:
            tail = jnp.where(t0 == 0, jnp.zeros_like(refs[k][...]), refs[k][...])
            k += 1
            vals.append((cur, tail))
        else:
            vals.append(cur)
    return vals, k


def stage_fwd(fn, toks, pars, outs, *, seq, tb, ncol=1, name):
    T = toks[0].arr.shape[0]
    tb = min(tb, seq)
    ntile, tpb = T // tb, seq // tb
    tspecs, targs = _tok_specs(toks, tb, False, ntile)
    pspecs, pargs = _par_specs(pars, tpb, False, ntile)
    n_in = len(tspecs) + len(pspecs)

    def body(*refs):
        s = pl.program_id(1)
        t0 = (s % tpb) * tb
        tvals, k = _read_toks(toks, refs, t0)
        pvals = [r[...] for r in refs[k:n_in]]
        res = fn(t0, *tvals, *pvals)
        for r, v in zip(refs[n_in:], res):
            r[...] = v.astype(r.dtype)

    return pl.pallas_call(
        body, name=name, grid=(ncol, ntile),
        in_specs=tspecs + pspecs,
        out_specs=[pl.BlockSpec((tb, w), lambda j, s: (s, j)) for w, _ in outs],
        out_shape=[jax.ShapeDtypeStruct((T, w * ncol), d) for w, d in outs],
        compiler_params=pltpu.CompilerParams(dimension_semantics=("arbitrary", "arbitrary"),
                                             vmem_limit_bytes=VMEM_LIMIT),
    )(*targs, *pargs)


def stage_bwd(fn, toks, pars, cots, *, cot_fn=None, seq, tb, ncol=1, name):
    T = toks[0].arr.shape[0]
    tb = min(tb, seq)
    ntile, tpb = T // tb, seq // tb
    tspecs, targs = _tok_specs(toks, tb, True, ntile)
    pspecs, pargs = _par_specs(pars, tpb, True, ntile)
    cspecs, cargs = _tok_specs(cots, tb, True, ntile)
    adds = [t for t in toks if t.add is not None]
    aspecs = [pl.BlockSpec((tb, t.width), lambda j, s: (ntile - 1 - s, j)) for t in adds]
    aargs = [t.add for t in adds]
    n_t, n_p, n_c = len(tspecs), len(pspecs), len(cspecs)

    gtoks = [t for t in toks if t.grad is not None]
    gpars = [p for p in pars if p.grad]
    out_specs, out_shape = [], []
    for t in gtoks:
        out_specs.append(pl.BlockSpec((tb, t.width), lambda j, s: (ntile - 1 - s, j)))
        out_shape.append(jax.ShapeDtypeStruct((T, t.width * ncol), t.grad))
    for p in gpars:
        if p.kind == 'const':
            out_specs.append(pl.BlockSpec(p.arr.shape, lambda j, s: (0, 0)))
            out_shape.append(jax.ShapeDtypeStruct(p.arr.shape, F32))
        elif p.kind == 'col':
            out_specs.append(pl.BlockSpec((p.arr.shape[0], p.width), lambda j, s: (0, j)))
            out_shape.append(jax.ShapeDtypeStruct((p.arr.shape[0], p.width * ncol), F32))
        else:
            out_specs.append(pl.BlockSpec((None,) + p.arr.shape[1:], lambda j, s: ((ntile - 1 - s) // tpb, 0, 0)))
            out_shape.append(jax.ShapeDtypeStruct(p.arr.shape, F32))
    carries = [t for t in gtoks if t.halo]
    scratch = [pltpu.VMEM((SUBLANES, t.width), F32) for t in carries]

    def body(*refs):
        j, s = pl.program_id(0), pl.program_id(1)
        i = ntile - 1 - s
        t0 = (i % tpb) * tb
        t_refs = refs[:n_t]
        p_refs = refs[n_t:n_t + n_p]
        c_refs = refs[n_t + n_p:n_t + n_p + n_c]
        a_refs = refs[n_t + n_p + n_c:n_t + n_p + n_c + len(adds)]
        o_refs = refs[n_t + n_p + n_c + len(adds):]
        gt_refs = o_refs[:len(gtoks)]
        gp_refs = o_refs[len(gtoks):len(gtoks) + len(gpars)]
        carry_refs = o_refs[len(gtoks) + len(gpars):]

        tvals, _ = _read_toks(toks, t_refs, t0)
        pvals = [r[...] for r in p_refs]
        cvals, _ = _read_toks(cots, c_refs, t0)

        def f(tv, pv):
            return tuple(fn(t0, *tv, *pv))

        res, vjp = jax.vjp(f, tvals, pvals)
        ct = cot_fn(t0, *cvals) if cot_fn is not None else tuple(cvals)
        ct = tuple(c.astype(r.dtype) for c, r in zip(ct, res))
        dt, dp = vjp(ct)

        ci = 0
        ai = 0
        gi = 0
        for t, d in zip(toks, dt):
            if t.grad is None:
                if t.add is not None:
                    ai += 1
                continue
            ref = gt_refs[gi]
            gi += 1
            if t.halo:
                dcur, dtail = d
                carry = carry_refs[ci]
                ci += 1

                @pl.when(s == 0)
                def _(carry=carry):
                    carry[...] = jnp.zeros_like(carry)

                top = dcur[:tb - SUBLANES] if tb > SUBLANES else None
                bot = dcur[tb - SUBLANES:] + carry[...]
                dcur = bot if top is None else jnp.concatenate([top, bot], axis=0)
                carry[...] = jnp.where(t0 == 0, jnp.zeros_like(dtail), dtail)
            else:
                dcur = d
            if t.add is not None:
                dcur = dcur + a_refs[ai][...].astype(F32)
                ai += 1
            ref[...] = dcur.astype(ref.dtype)

        gi = 0
        for p, d in zip(pars, dp):
            if not p.grad:
                continue
            ref = gp_refs[gi]
            gi += 1
            if p.kind == 'const':
                first = jnp.logical_and(j == 0, s == 0)
            elif p.kind == 'col':
                first = s == 0
            else:
                first = s % tpb == 0

            @pl.when(first)
            def _(ref=ref, d=d):
                ref[...] = d.astype(F32)

            @pl.when(jnp.logical_not(first))
            def _(ref=ref, d=d):
                ref[...] += d.astype(F32)

    res = pl.pallas_call(
        body, name=name, grid=(ncol, ntile),
        in_specs=tspecs + pspecs + cspecs + aspecs,
        out_specs=out_specs, out_shape=out_shape, scratch_shapes=scratch,
        compiler_params=pltpu.CompilerParams(dimension_semantics=("arbitrary", "arbitrary"),
                                             vmem_limit_bytes=VMEM_LIMIT),
    )(*targs, *pargs, *cargs, *aargs)
    return list(res[:len(gtoks)]), list(res[len(gtoks):])


def _bcast_row(x, r):
    return jnp.broadcast_to(x[r:r + 1, :], x.shape)


SCAN_TB = 512


def scan_real(a, b, *, seq, reverse, name):
    T, C = a.shape
    tb = min(SCAN_TB, seq)
    nb, nt, nblk = T // seq, seq // tb, tb // SUBLANES

    def body(a_ref, b_ref, h_ref, carry_h, carry_a):
        @pl.when(pl.program_id(1) == 0)
        def _():
            carry_h[...] = jnp.zeros_like(carry_h)
            carry_a[...] = jnp.zeros_like(carry_a)

        row = lax.broadcasted_iota(jnp.int32, (SUBLANES, C), 0)

        def blk(n, c):
            ch, ca = c
            k = nblk - 1 - n if reverse else n
            o = pl.multiple_of(k * SUBLANES, SUBLANES)
            A = a_ref[pl.ds(o, SUBLANES), :]
            B = b_ref[pl.ds(o, SUBLANES), :]
            if reverse:
                a_first = _bcast_row(A, 0)
                A = jnp.where(row == SUBLANES - 1, ca, pltpu.roll(A, SUBLANES - 1, 0))
                for s in (1, 2, 4):
                    keep = row < SUBLANES - s
                    Bs = jnp.where(keep, pltpu.roll(B, SUBLANES - s, 0), 0.0)
                    As = jnp.where(keep, pltpu.roll(A, SUBLANES - s, 0), 1.0)
                    B = B + A * Bs
                    A = A * As
                h = B + A * ch
                h_ref[pl.ds(o, SUBLANES), :] = h
                return _bcast_row(h, 0), a_first
            for s in (1, 2, 4):
                keep = row >= s
                Bs = jnp.where(keep, pltpu.roll(B, s, 0), 0.0)
                As = jnp.where(keep, pltpu.roll(A, s, 0), 1.0)
                B = B + A * Bs
                A = A * As
            h = B + A * ch
            h_ref[pl.ds(o, SUBLANES), :] = h
            return _bcast_row(h, SUBLANES - 1), ca

        ch, ca = lax.fori_loop(0, nblk, blk, (carry_h[...], carry_a[...]))
        carry_h[...] = ch
        carry_a[...] = ca

    if reverse:
        spec = pl.BlockSpec((tb, C), lambda bi, i: (bi * nt + nt - 1 - i, 0))
    else:
        spec = pl.BlockSpec((tb, C), lambda bi, i: (bi * nt + i, 0))
    return pl.pallas_call(
        body, name=name, grid=(nb, nt), in_specs=[spec, spec], out_specs=spec,
        out_shape=jax.ShapeDtypeStruct((T, C), F32),
        scratch_shapes=[pltpu.VMEM((SUBLANES, C), F32), pltpu.VMEM((SUBLANES, C), F32)],
        compiler_params=pltpu.CompilerParams(dimension_semantics=("arbitrary", "arbitrary"),
                                             vmem_limit_bytes=VMEM_LIMIT),
    )(a, b)


def scan_cplx(ar, ai, bu, *, seq, reverse, name):
    T = bu.shape[0]
    cw = S5_CW
    ncol = C_CH // cw
    tb = min(SCAN_TB, seq)
    nb, nt, nblk = T // seq, seq // tb, tb // SUBLANES

    def body(ar_ref, ai_ref, b_ref, h_ref, carry_r, carry_i):
        @pl.when(pl.program_id(2) == 0)
        def _():
            carry_r[...] = jnp.zeros_like(carry_r)
            carry_i[...] = jnp.zeros_like(carry_i)

        row = lax.broadcasted_iota(jnp.int32, (SUBLANES, cw), 0)
        Ar = jnp.broadcast_to(ar_ref[...], (SUBLANES, cw))
        Ai = jnp.broadcast_to(ai_ref[...], (SUBLANES, cw))
        levels = []
        for s in (1, 2, 4):
            keep = (row < SUBLANES - s) if reverse else (row >= s)
            sh = SUBLANES - s if reverse else s
            levels.append((Ar, Ai, keep, sh))
            Asr = jnp.where(keep, pltpu.roll(Ar, sh, 0), 1.0)
            Asi = jnp.where(keep, pltpu.roll(Ai, sh, 0), 0.0)
            Ar, Ai = Ar * Asr - Ai * Asi, Ar * Asi + Ai * Asr

        def blk(n, c):
            cr, ci = c
            k = nblk - 1 - n if reverse else n
            o = pl.multiple_of(k * SUBLANES, SUBLANES)
            Br = b_ref[pl.ds(o, SUBLANES), :cw]
            Bi = b_ref[pl.ds(o, SUBLANES), cw:]
            for lr, li, keep, sh in levels:
                Bsr = jnp.where(keep, pltpu.roll(Br, sh, 0), 0.0)
                Bsi = jnp.where(keep, pltpu.roll(Bi, sh, 0), 0.0)
                Br, Bi = Br + lr * Bsr - li * Bsi, Bi + lr * Bsi + li * Bsr
            hr = Br + Ar * cr - Ai * ci
            hi = Bi + Ar * ci + Ai * cr
            h_ref[pl.ds(o, SUBLANES), :cw] = hr
            h_ref[pl.ds(o, SUBLANES), cw:] = hi
            last = 0 if reverse else SUBLANES - 1
            return _bcast_row(hr, last), _bcast_row(hi, last)

        cr, ci = lax.fori_loop(0, nblk, blk, (carry_r[...], carry_i[...]))
        carry_r[...] = cr
        carry_i[...] = ci

    if reverse:
        spec = pl.BlockSpec((tb, 2 * cw), lambda bi, j, i: (bi * nt + nt - 1 - i, j))
    else:
        spec = pl.BlockSpec((tb, 2 * cw), lambda bi, j, i: (bi * nt + i, j))
    aspec = pl.BlockSpec((1, cw), lambda bi, j, i: (0, j))
    return pl.pallas_call(
        body, name=name, grid=(nb, ncol, nt), in_specs=[aspec, aspec, spec], out_specs=spec,
        out_shape=jax.ShapeDtypeStruct((T, 2 * C_CH), F32),
        scratch_shapes=[pltpu.VMEM((SUBLANES, cw), F32), pltpu.VMEM((SUBLANES, cw), F32)],
        compiler_params=pltpu.CompilerParams(dimension_semantics=("arbitrary", "arbitrary", "arbitrary"),
                                             vmem_limit_bytes=VMEM_LIMIT),
    )(ar, ai, bu)


def _col_sums(x):
    ones = jnp.ones((x.shape[0], x.shape[0]), BF)
    acc, rest = None, x
    for _ in range(3):
        piece = rest.astype(BF)
        rest = rest - piece.astype(F32)
        term = lax.dot_general(ones, piece, (((1,), (0,)), ((), ())), preferred_element_type=F32)
        acc = term if acc is None else acc + term
    return acc


@jax.custom_vjp
def col_sums(x):
    return _col_sums(x)


col_sums.defvjp(lambda x: (_col_sums(x), None), lambda _, g: (_col_sums(g),))


def tri_inv(a):
    ii = lax.broadcasted_iota(jnp.int32, a.shape, 0)
    jj = lax.broadcasted_iota(jnp.int32, a.shape, 1)
    tinv = jnp.where(ii == jj, 1.0, 0.0) - a
    p = a
    for _ in range(5):
        p = _dotx(p, p, False, False)
        tinv = tinv + _dotx(tinv, p, False, False)
    return tinv


@jax.custom_vjp
def tri_inv_saved(a, t):
    return t


tri_inv_saved.defvjp(lambda a, t: (t, t),
                     lambda t, g: (-_dotx(_dotx(t, g, True, False), t, False, True), jnp.zeros_like(t)))


def delta_chunk(q, k, v, bB, gB, S, tinv=None):
    c = B_CHUNK
    qc = q * (B_DK ** -0.5)
    gc = cumsum_rows(gB)
    ii = lax.broadcasted_iota(jnp.int32, (c, c), 0)
    jj = lax.broadcasted_iota(jnp.int32, (c, c), 1)
    incl, strict, diag = ii >= jj, ii > jj, ii == jj
    gcol = gc[:, :c]
    grow = col_sums(jnp.where(diag, gcol, 0.0))
    decay = jnp.exp(jnp.where(incl, gcol - grow, -1e30))
    kb = k * bB
    a_mat = jnp.where(strict, bmm(kb, k, False, True) * decay, 0.0)
    tinv = tri_inv(a_mat) if tinv is None else tri_inv_saved(a_mat, tinv)
    eg = jnp.exp(gc)
    sol = xmm(tinv, jnp.concatenate([v * bB, kb * eg], axis=-1))
    u, w = sol[:, :B_DK], sol[:, B_DK:]
    qk = jnp.where(incl, bmm(qc, k, False, True) * decay, 0.0)
    glast = gc[c - 1:c, :]
    k_dec = k * jnp.exp(glast - gc)
    v_new = u - bmm(w, S)
    o = bmm(qc * eg, S) + bmm(qk, v_new)
    s_new = S * jnp.exp(glast) + bmm(k_dec, v_new, True, False)
    return o, s_new, tinv


def delta_fwd(q, k, v, bB, gB, *, name):
    nb, seq, _ = q.shape
    n = seq // B_CHUNK
    hd = B_DK

    def body(q_ref, k_ref, v_ref, b_ref, g_ref, o_ref, st_ref, ti_ref, state):
        @pl.when(pl.program_id(0) == 0)
        def _():
            state[...] = jnp.zeros_like(state)

        for b in range(nb):
            for h in range(B_HEADS):
                sl = slice(h * hd, (h + 1) * hd)
                s_old = state[b * B_HEADS + h]
                st_ref[b, h] = s_old
                o, s_new, tinv = delta_chunk(q_ref[b, :, sl], k_ref[b, :, sl], v_ref[b, :, sl], b_ref[b, :, sl],
                                             g_ref[b, :, sl], s_old)
                o_ref[b, :, sl] = o
                ti_ref[b, h] = tinv
                state[b * B_HEADS + h] = s_new

    spec = pl.BlockSpec((nb, B_CHUNK, BW), lambda i: (0, i, 0))
    return pl.pallas_call(
        body, name=name, grid=(n,), in_specs=[spec] * 5,
        out_specs=[spec, pl.BlockSpec((nb, None, B_HEADS, hd, hd), lambda i: (0, i, 0, 0, 0)),
                   pl.BlockSpec((nb, None, B_HEADS, B_CHUNK, B_CHUNK), lambda i: (0, i, 0, 0, 0))],
        out_shape=[jax.ShapeDtypeStruct((nb, seq, BW), F32), jax.ShapeDtypeStruct((nb, n, B_HEADS, hd, hd), F32),
                   jax.ShapeDtypeStruct((nb, n, B_HEADS, B_CHUNK, B_CHUNK), F32)],
        scratch_shapes=[pltpu.VMEM((nb * B_HEADS, hd, hd), F32)],
        compiler_params=pltpu.CompilerParams(dimension_semantics=("arbitrary",), vmem_limit_bytes=VMEM_LIMIT),
    )(q, k, v, bB, gB)


def delta_bwd(q, k, v, bB, gB, states, tinvs, do, *, name):
    nb, seq, _ = q.shape
    n = seq // B_CHUNK
    hd = B_DK

    def body(q_ref, k_ref, v_ref, b_ref, g_ref, st_ref, ti_ref, do_ref, dq_ref, dk_ref, dv_ref, db_ref, dg_ref, dstate):
        @pl.when(pl.program_id(0) == 0)
        def _():
            dstate[...] = jnp.zeros_like(dstate)

        for b in range(nb):
            for h in range(B_HEADS):
                sl = slice(h * hd, (h + 1) * hd)

                def f(qv, kv, vv, bv, gv, sv, tv=ti_ref[b, h]):
                    return delta_chunk(qv, kv, vv, bv, gv, sv, tv)[:2]

                _, vjp = jax.vjp(f, q_ref[b, :, sl], k_ref[b, :, sl], v_ref[b, :, sl], b_ref[b, :, sl],
                                 g_ref[b, :, sl], st_ref[b, h])
                dq, dk, dv, db, dg, ds = vjp((do_ref[b, :, sl], dstate[b * B_HEADS + h]))
                dq_ref[b, :, sl] = dq
                dk_ref[b, :, sl] = dk
                dv_ref[b, :, sl] = dv
                db_ref[b, :, sl] = db
                dg_ref[b, :, sl] = dg
                dstate[b * B_HEADS + h] = ds

    spec = pl.BlockSpec((nb, B_CHUNK, BW), lambda i: (0, n - 1 - i, 0))
    sspec = pl.BlockSpec((nb, None, B_HEADS, hd, hd), lambda i: (0, n - 1 - i, 0, 0, 0))
    tspec = pl.BlockSpec((nb, None, B_HEADS, B_CHUNK, B_CHUNK), lambda i: (0, n - 1 - i, 0, 0, 0))
    return pl.pallas_call(
        body, name=name, grid=(n,), in_specs=[spec] * 5 + [sspec, tspec, spec],
        out_specs=[spec] * 5, out_shape=[jax.ShapeDtypeStruct((nb, seq, BW), F32)] * 5,
        scratch_shapes=[pltpu.VMEM((nb * B_HEADS, hd, hd), F32)],
        compiler_params=pltpu.CompilerParams(dimension_semantics=("arbitrary",), vmem_limit_bytes=VMEM_LIMIT),
    )(q, k, v, bB, gB, states, tinvs, do)


def loss_head(x, tgt, g, *, tb, name):
    T, D = x.shape
    tb = min(tb, T)
    nt = T // tb

    def body(x_ref, t_ref, g_ref, l_ref, dx_ref, dg_ref):
        tg = t_ref[...]

        def f(xv, gv):
            err = rms(xv, gv) - tg
            return 0.5 * jnp.mean(err * err, axis=-1, keepdims=True)

        rows, vjp = jax.vjp(f, x_ref[...], g_ref[...])
        dx, dg = vjp(jnp.ones_like(rows))
        dx_ref[...] = dx
        tot = jnp.broadcast_to(jnp.sum(rows, axis=0, keepdims=True), (1, 128))

        @pl.when(pl.program_id(0) == 0)
        def _():
            l_ref[...] = tot
            dg_ref[...] = dg

        @pl.when(pl.program_id(0) > 0)
        def _():
            l_ref[...] += tot
            dg_ref[...] += dg

    tok = pl.BlockSpec((tb, D), lambda i: (i, 0))
    return pl.pallas_call(
        body, name=name, grid=(nt,),
        in_specs=[tok, tok, pl.BlockSpec((1, D), lambda i: (0, 0))],
        out_specs=[pl.BlockSpec((1, 128), lambda i: (0, 0)), tok, pl.BlockSpec((1, D), lambda i: (0, 0))],
        out_shape=[jax.ShapeDtypeStruct((1, 128), F32), jax.ShapeDtypeStruct((T, D), F32), jax.ShapeDtypeStruct((1, D), F32)],
        compiler_params=pltpu.CompilerParams(dimension_semantics=("arbitrary",), vmem_limit_bytes=VMEM_LIMIT),
    )(x, tgt, g)


def _row_block(rows, cols):
    budget = 256 * 1024
    tr = max(SUBLANES, min(rows, budget // max(cols, 1)) // SUBLANES * SUBLANES)
    while rows % tr:
        tr -= SUBLANES
        if tr <= 0:
            return rows
    return tr


def adamw(w, g, m, v, *, name):
    R, C = w.shape
    tr = _row_block(R, C)
    c1 = 1.0 / (1.0 - ADAM_B1 ** ADAM_STEP)
    c2 = 1.0 / (1.0 - ADAM_B2 ** ADAM_STEP)

    def body(w_ref, g_ref, m_ref, v_ref, d_ref, mo_ref, vo_ref):
        gv = g_ref[...]
        mn = ADAM_B1 * m_ref[...] + (1.0 - ADAM_B1) * gv
        vn = ADAM_B2 * v_ref[...] + (1.0 - ADAM_B2) * (gv * gv)
        d_ref[...] = -ADAM_LR * ((mn * c1) / (jnp.sqrt(vn * c2) + ADAM_EPS) + ADAM_WD * w_ref[...])
        mo_ref[...] = mn
        vo_ref[...] = vn

    spec = pl.BlockSpec((tr, C), lambda i: (i, 0))
    return pl.pallas_call(
        body, name=name, grid=(R // tr,), in_specs=[spec] * 4, out_specs=[spec] * 3,
        out_shape=[jax.ShapeDtypeStruct((R, C), F32)] * 3,
        compiler_params=pltpu.CompilerParams(dimension_semantics=("parallel",), vmem_limit_bytes=VMEM_LIMIT),
    )(w, g, m, v)


def sum_slabs(x, *, name):
    n, R, C = x.shape
    tr = _row_block(R, C * 2)

    def body(x_ref, o_ref):
        acc = x_ref[0].astype(F32)
        for d in range(1, n):
            acc = acc + x_ref[d].astype(F32)
        o_ref[...] = acc

    return pl.pallas_call(
        body, name=name, grid=(R // tr,),
        in_specs=[pl.BlockSpec((n, tr, C), lambda i: (0, i, 0))],
        out_specs=pl.BlockSpec((tr, C), lambda i: (i, 0)),
        out_shape=jax.ShapeDtypeStruct((R, C), F32),
        compiler_params=pltpu.CompilerParams(dimension_semantics=("parallel",), vmem_limit_bytes=VMEM_LIMIT),
    )(x)


def _exchange(src, *, scatter, name):
    R, C = src.shape[-2:]

    def body(src_ref, out_ref, send_sems, recv_sems, local_sem):
        x, y, c = lax.axis_index("x"), lax.axis_index("y"), lax.axis_index("c")
        me = 4 * x + 2 * y + c

        def peer(k):
            return (x ^ ((k >> 2) & 1), y ^ ((k >> 1) & 1), c ^ (k & 1))

        def peer_index(k):
            px, py, pc = peer(k)
            return 4 * px + 2 * py + pc

        mine = pltpu.make_async_copy(src_ref.at[me] if scatter else src_ref, out_ref.at[me], local_sem)
        mine.start()
        copies = []
        for k in range(1, N_DEV):
            cp = pltpu.make_async_remote_copy(
                src_ref=src_ref.at[peer_index(k)] if scatter else src_ref,
                dst_ref=out_ref.at[me],
                send_sem=send_sems.at[k - 1], recv_sem=recv_sems.at[k - 1],
                device_id=peer(k), device_id_type=pl.DeviceIdType.MESH)
            cp.start()
            copies.append(cp)
        for k in range(1, N_DEV):
            pltpu.make_async_remote_copy(
                src_ref=src_ref.at[peer_index(k)] if scatter else src_ref,
                dst_ref=out_ref.at[peer_index(k)],
                send_sem=send_sems.at[k - 1], recv_sem=recv_sems.at[k - 1],
                device_id=peer(k), device_id_type=pl.DeviceIdType.MESH).wait_recv()
        for cp in copies:
            cp.wait_send()
        mine.wait()

    return pl.pallas_call(
        body, name=name,
        in_specs=[pl.BlockSpec(memory_space=pl.ANY)],
        out_specs=pl.BlockSpec(memory_space=pl.ANY),
        out_shape=jax.ShapeDtypeStruct((N_DEV, R, C), src.dtype),
        scratch_shapes=[pltpu.SemaphoreType.DMA((N_DEV - 1,)), pltpu.SemaphoreType.DMA((N_DEV - 1,)),
                        pltpu.SemaphoreType.DMA],
    )(src)


TB = 256


def st_norm(t0, x, g):
    return (rms(x.astype(F32), g),)


def _conv(xt, w, bias=None):
    kk = w.shape[0]
    acc = bias
    for i in range(kk):
        term = w[i:i + 1, :] * shift_rows(xt, kk - 1 - i)
        acc = term if acc is None else acc + term
    return acc


def st_a(t0, xa, cw, cb, wr, br, wi, bi, lam):
    xc = _conv(xa, cw, cb)
    r = sigmoid(bmm(xc, wr) + br)
    ig = sigmoid(bmm(xc, wi) + bi)
    log_a = -RG_C * r * softplus(-lam)
    row = lax.broadcasted_iota(jnp.int32, xc.shape, 0) + t0
    mult = jnp.where(row == 0, 1.0, jnp.sqrt(-expm1(2.0 * log_a)))
    return jnp.exp(log_a), mult * ig * xc


def st_a_cot(t0, lam, ha):
    return lam * shift_rows(ha, 1), lam


def _heads(x, n, w):
    return [x[:, h * w:(h + 1) * w] for h in range(n)]


def st_b(t0, q, k, v, pba, cw, alog, dtb):
    qc = silu(_conv(q, cw[:, 0:BW]))
    kc = silu(_conv(k, cw[:, BW:2 * BW]))
    vc = silu(_conv(v, cw[:, 2 * BW:3 * BW]))

    def l2n(x):
        return jnp.concatenate([s * lax.rsqrt(jnp.sum(s * s, axis=-1, keepdims=True) + EPS)
                                for s in _heads(x, B_HEADS, B_DK)], axis=-1)

    sg = sigmoid(pba)
    gg = -jnp.exp(alog) * softplus(pba + dtb)
    lane = lax.broadcasted_iota(jnp.int32, pba.shape, 1)

    def spread(x, first):
        return jnp.concatenate(
            [jnp.broadcast_to(jnp.sum(jnp.where(lane == first + h, x, 0.0), axis=-1, keepdims=True), (x.shape[0], B_DK))
             for h in range(B_HEADS)], axis=-1)

    return l2n(qc), l2n(kc), vc, spread(sg, 0), spread(gg, B_HEADS)


def st_m2(t0, ha, ga, o, z, y0, uc, bn4, cd, wglu, bglu):
    ya = ha * gelu(ga)
    yb = jnp.concatenate(
        [oh * lax.rsqrt(jnp.mean(oh * oh, axis=-1, keepdims=True) + EPS) * bh * silu(zh)
         for oh, bh, zh in zip(_heads(o, B_HEADS, B_DK), _heads(bn4, B_HEADS, B_DK), _heads(z, B_HEADS, B_DK))], axis=-1)
    yc0 = gelu(y0 + cd * uc)
    yc = yc0 * sigmoid(bmm(yc0, wglu) + bglu)
    return (jnp.concatenate([ya, yb, yc], axis=-1),)


def st_m2_cot(t0, d0, d1, d2):
    return (jnp.concatenate([d0, d1, d2], axis=-1),)


def st_m3(t0, g0, g1, g2, p0, p1, p2, bg):
    d = D_MODEL
    return (sigmoid(g0 + bg[:, 0:d]) * p0 + sigmoid(g1 + bg[:, d:2 * d]) * p1 + sigmoid(g2 + bg[:, 2 * d:3 * d]) * p2,)


def st_att(t0, q, kv):
    outs = []
    for h in range(X_HEADS):
        qh = q[:, h * X_HD:(h + 1) * X_HD]
        kh = kv[:, h * X_HD:(h + 1) * X_HD]
        vh = kv[:, D_MODEL + h * X_HD:D_MODEL + (h + 1) * X_HD]
        sc = bmm(qh, kh, False, True) * (X_HD ** -0.5)
        e = jnp.exp(sc - lax.stop_gradient(jnp.max(sc, axis=-1, keepdims=True)))
        outs.append(bmm(e / jnp.sum(e, axis=-1, keepdims=True), vh))
    return (jnp.concatenate(outs, axis=-1),)


def st_f2(t0, ug, uv, cwg, cwv, cbg, cbv):
    return (gelu(_conv(ug, cwg, cbg)) * _conv(uv, cwv, cbv),)


def st_s5step(t0, h, ar, ai):
    hp = shift_rows(h, 1)
    hr, hi = hp[:, :S5_CW], hp[:, S5_CW:]
    return (jnp.concatenate([ar * hr - ai * hi, ar * hi + ai * hr], axis=-1),)


W_MAIN = 6 * BW
W_BA = W_MAIN + 2 * B_HEADS
W_UC = W_BA + BW
COL_GATES, COL_UC = W_MAIN, W_MAIN + 3 * D_MODEL


def split_w_in(wt):
    big = jnp.concatenate([wt[:, :W_MAIN], wt[:, W_UC:], wt[:, W_BA:W_UC]], axis=1)
    ba = jnp.pad(wt[:, W_MAIN:W_BA], ((0, 0), (0, 128 - 2 * B_HEADS), (0, 0)))
    return big, ba


def merge_w_in(big, ba):
    return jnp.concatenate([big[:, :W_MAIN], ba[:, :2 * B_HEADS], big[:, COL_UC:], big[:, COL_GATES:COL_UC]], axis=1)


def derive(r):
    L = r['a_w_r'].shape[0]
    eye_a = jnp.eye(A_HEADS, dtype=F32)
    eye_g = jnp.eye(C_GROUPS, dtype=F32)

    def blockdiag(w):
        return jnp.einsum('lhij,hg->lhigj', w, eye_a).reshape(L, BW, BW)

    def lanes(v, first):
        return jnp.pad(v, ((0, 0), (first, 128 - first - B_HEADS)))[:, None, :]

    lr, li = r['c_lam_re'], r['c_lam_im']
    dt = jnp.exp(r['c_log_dt'])[..., None]
    mag = jnp.exp(lr * dt)
    ar, ai = mag * jnp.cos(li * dt), mag * jnp.sin(li * dt)
    den = lr * lr + li * li
    fr = ((ar - 1.0) * lr + ai * li) / den
    fi = (ai * lr - (ar - 1.0) * li) / den
    br, bi = r['c_b_re'], r['c_b_im']
    bbr = fr[..., None] * br - fi[..., None] * bi
    bbi = fr[..., None] * bi + fi[..., None] * br
    ncol = C_CH // S5_CW

    def b_dense(bb):
        return jnp.einsum('lgpc,gh->lgchp', bb, eye_g).reshape(L, BW, ncol, S5_CW)

    bbig = jnp.stack([b_dense(bbr), b_dense(bbi)], axis=3).reshape(L, BW, 2 * C_CH)

    def c_dense(cc):
        return jnp.einsum('lgcp,gh->lgphc', cc, eye_g).reshape(L, ncol, S5_CW, BW)

    cbig = jnp.stack([c_dense(r['c_c_re']), -c_dense(r['c_c_im'])], axis=2).reshape(L, 2 * C_CH, BW)
    return dict(wr=blockdiag(r['a_w_r']), wi=blockdiag(r['a_w_i']),
                alog=lanes(r['b_a_log'], B_HEADS), dtb=lanes(r['b_dt_bias'], B_HEADS),
                bn4=jnp.tile(r['b_norm'], (1, B_HEADS))[:, None, :],
                ar=ar.reshape(L, 1, C_CH), ai=ai.reshape(L, 1, C_CH), bbig=bbig, cbig=cbig)


DERIVE_FROM = ['a_w_r', 'a_w_i', 'b_a_log', 'b_dt_bias', 'b_norm', 'c_lam_re', 'c_lam_im', 'c_log_dt',
               'c_b_re', 'c_b_im', 'c_c_re', 'c_c_im']


def _row(v):
    return v.reshape(1, -1)


def layer_params(w, big, ba, dv, l):
    return dict(
        mix_norm=_row(w['mix_norm'][l]), w_big=big[l], w_ba=ba[l], b_gate=_row(w['b_gate'][l]),
        a_conv_w=w['a_conv_w'][l], a_conv_b=_row(w['a_conv_b'][l]), wr=dv['wr'][l], b_r=_row(w['a_b_r'][l]),
        wi=dv['wi'][l], b_i=_row(w['a_b_i'][l]), lam=_row(w['a_lam'][l]),
        b_conv_w=w['b_conv_w'][l], alog=dv['alog'][l], dtb=dv['dtb'][l], bn4=dv['bn4'][l],
        ar=dv['ar'][l], ai=dv['ai'][l], bbig=dv['bbig'][l], cbig=dv['cbig'][l],
        c_d=_row(w['c_d'][l]), wglu=w['c_glu_w'][l].astype(F32), bglu=_row(w['c_glu_b'][l]),
        w_brT=w['w_brT'][l], w_out=w['w_out'][l],
        xa_norm=_row(w['xa_norm'][l]), mem_norm=_row(w['mem_norm'][l]),
        w_q=w['xa_w_q'][l], w_kvT=w['w_kvT'][l], w_o=w['xa_w_o'][l],
        ffn_norm=_row(w['ffn_norm'][l]), w_upT=w['w_upT'][l], ffn_conv_w=w['ffn_conv_w'][l],
        ffn_conv_b=_row(w['ffn_conv_b'][l]), w_down=w['ffn_w_down'][l])


def _a_pars(p):
    return [Par(p['a_conv_w']), Par(p['a_conv_b']), Par(p['wr']), Par(p['b_r']), Par(p['wi']), Par(p['b_i']), Par(p['lam'])]


def _f2_pars(p):
    d = D_MODEL
    return [Par(p['ffn_conv_w'], 'col', 0, d), Par(p['ffn_conv_w'], 'col', 3, d),
            Par(p['ffn_conv_b'], 'col', 0, d), Par(p['ffn_conv_b'], 'col', 3, d)]


def layer_fwd(x, mem, p, seq, mseq, l):
    d = D_MODEL
    nb = x.shape[0] // seq
    kw = dict(seq=seq, tb=TB)
    n = lambda s: f"{s}_l{l}"
    h, = stage_fwd(st_norm, [Tok(x, d)], [Par(p['mix_norm'])], [(d, MXU_DTYPE)], name=n("norm_mix"), **kw)
    P = mm(h, p['w_big'], tb=True, name=n("mm_in"))
    Pba = mm(h, p['w_ba'], tb=True, name=n("mm_in_ba"))
    a, bb = stage_fwd(st_a, [Tok(P, BW, 0, True)], _a_pars(p), [(BW, F32)] * 2, name=n("rglru_pre"), **kw)
    ha = scan_real(a, bb, seq=seq, reverse=False, name=n("rglru_scan"))
    qn, kn, vv, bB, gB = stage_fwd(
        st_b, [Tok(P, BW, 2, True), Tok(P, BW, 3, True), Tok(P, BW, 4, True), Tok(Pba, 128)],
        [Par(p['b_conv_w']), Par(p['alog']), Par(p['dtb'])], [(BW, F32)] * 5, name=n("delta_pre"), **kw)
    r3 = lambda t: t.reshape(nb, seq, BW)
    o3, states, tinvs = delta_fwd(r3(qn), r3(kn), r3(vv), r3(bB), r3(gB), name=n("delta"))
    o = o3.reshape(-1, BW)
    bu = mm(P, p['bbig'], a_cols=(COL_UC, BW), name=n("mm_s5_in"))
    hs = scan_cplx(p['ar'], p['ai'], bu, seq=seq, reverse=False, name=n("s5_scan"))
    y0 = mm(hs, p['cbig'], name=n("mm_s5_out"))
    m2_toks = [Tok(ha, BW), Tok(P, BW, 1), Tok(o, BW), Tok(P, BW, 5), Tok(y0, BW), Tok(P, BW, COL_UC // BW)]
    m2_pars = [Par(p['bn4']), Par(p['c_d']), Par(p['wglu']), Par(p['bglu'])]
    Y3, = stage_fwd(st_m2, m2_toks, m2_pars, [(3 * BW, MXU_DTYPE)], name=n("branches"), **kw)
    proj = [mm(Y3, p['w_brT'][k], tb=True, a_cols=(k * BW, BW), name=n(f"mm_branch{k}")) for k in range(3)]
    g0 = COL_GATES // d
    m3_toks = [Tok(P, d, g0), Tok(P, d, g0 + 1), Tok(P, d, g0 + 2)] + [Tok(t, d) for t in proj]
    mixed, = stage_fwd(st_m3, m3_toks, [Par(p['b_gate'])], [(d, MXU_DTYPE)], name=n("gate_mix"), **kw)
    x1 = mm(mixed, p['w_out'], residual=x, name=n("mm_out"))
    hx, = stage_fwd(st_norm, [Tok(x1, d)], [Par(p['xa_norm'])], [(d, MXU_DTYPE)], name=n("norm_xa"), **kw)
    mn, = stage_fwd(st_norm, [Tok(mem, d)], [Par(p['mem_norm'])], [(d, MXU_DTYPE)], seq=mseq, tb=TB, name=n("norm_mem"))
    qx = mm(hx, p['w_q'], name=n("mm_q"))
    kv = mm(mn, p['w_kvT'], tb=True, name=n("mm_kv"))
    kv3 = kv.reshape(nb, mseq, 2 * d)
    ox, = stage_fwd(st_att, [Tok(qx, d)], [Par(kv3, 'batch')], [(d, MXU_DTYPE)], name=n("attention"), **kw)
    x2 = mm(ox, p['w_o'], residual=x1, name=n("mm_o"))
    hf, = stage_fwd(st_norm, [Tok(x2, d)], [Par(p['ffn_norm'])], [(d, MXU_DTYPE)], name=n("norm_ffn"), **kw)
    U = mm(hf, p['w_upT'], tb=True, name=n("mm_up"))
    act, = stage_fwd(st_f2, [Tok(U, d, 0, True), Tok(U, d, 3, True)], _f2_pars(p), [(d, MXU_DTYPE)], ncol=3,
                     name=n("ffn_act"), **kw)
    x3 = mm(act, p['w_down'], residual=x2, name=n("mm_down"))
    sv = dict(x=x, h=h, P=P, Pba=Pba, a=a, ha=ha, qn=qn, kn=kn, vv=vv, bB=bB, gB=gB, states=states, tinvs=tinvs, o=o, hs=hs, y0=y0,
              Y3=Y3, proj=proj, mixed=mixed, x1=x1, hx=hx, mn=mn, qx=qx, kv3=kv3, ox=ox, x2=x2, hf=hf, U=U, act=act)
    return x3, sv


def layer_bwd(dx3, mem, p, sv, seq, mseq, l):
    d = D_MODEL
    nb = dx3.shape[0] // seq
    kw = dict(seq=seq, tb=TB)
    n = lambda s: f"{s}_l{l}"
    g = {}
    P, Pba = sv['P'], sv['Pba']
    dact = mm(dx3, p['w_down'], tb=True, name=n("bmm_down_x"))
    g['w_down'] = mm(sv['act'], dx3, ta=True, name=n("bmm_down_w"))
    (dUg, dUv), gp = stage_bwd(st_f2, [Tok(sv['U'], d, 0, True, MXU_DTYPE), Tok(sv['U'], d, 3, True, MXU_DTYPE)], _f2_pars(p),
                               [Tok(dact, d)], ncol=3, name=n("b_ffn_act"), **kw)
    g['ffn_conv_w'] = jnp.concatenate([gp[0], gp[1]], axis=1)
    g['ffn_conv_b'] = jnp.concatenate([gp[2], gp[3]], axis=1)
    dU = jnp.concatenate([dUg, dUv], axis=1)
    dhf = mm(dU, p['w_upT'], name=n("bmm_up_x"))
    g['w_upT'] = mm(dU, sv['hf'], ta=True, name=n("bmm_up_w"))
    (dx2,), (g['ffn_norm'],) = stage_bwd(st_norm, [Tok(sv['x2'], d, grad=F32, add=dx3)], [Par(p['ffn_norm'])],
                                         [Tok(dhf, d)], name=n("b_norm_ffn"), **kw)
    dox = mm(dx2, p['w_o'], tb=True, name=n("bmm_o_x"))
    g['w_o'] = mm(sv['ox'], dx2, ta=True, name=n("bmm_o_w"))
    (dqx,), (dkv3,) = stage_bwd(st_att, [Tok(sv['qx'], d, grad=MXU_DTYPE)], [Par(sv['kv3'], 'batch')], [Tok(dox, d)],
                                name=n("b_attention"), **kw)
    dkv = dkv3.reshape(-1, 2 * d)
    dhx = mm(dqx, p['w_q'], tb=True, name=n("bmm_q_x"))
    g['w_q'] = mm(sv['hx'], dqx, ta=True, name=n("bmm_q_w"))
    dmn = mm(dkv, p['w_kvT'], name=n("bmm_kv_x"))
    g['w_kvT'] = mm(dkv, sv['mn'], ta=True, name=n("bmm_kv_w"))
    _, (g['mem_norm'],) = stage_bwd(st_norm, [Tok(mem, d)], [Par(p['mem_norm'])], [Tok(dmn, d)], seq=mseq, tb=TB,
                                    name=n("b_norm_mem"))
    (dx1,), (g['xa_norm'],) = stage_bwd(st_norm, [Tok(sv['x1'], d, grad=F32, add=dx2)], [Par(p['xa_norm'])],
                                        [Tok(dhx, d)], name=n("b_norm_xa"), **kw)
    dmixed = mm(dx1, p['w_out'], tb=True, name=n("bmm_out_x"))
    g['w_out'] = mm(sv['mixed'], dx1, ta=True, name=n("bmm_out_w"))
    g0 = COL_GATES // d
    m3_toks = [Tok(P, d, g0 + k, grad=MXU_DTYPE) for k in range(3)] + [Tok(t, d, grad=MXU_DTYPE) for t in sv['proj']]
    dm3, (g['b_gate'],) = stage_bwd(st_m3, m3_toks, [Par(p['b_gate'])], [Tok(dmixed, d)], name=n("b_gate_mix"), **kw)
    dgates, dproj = dm3[:3], dm3[3:]
    dY = [mm(dproj[k], p['w_brT'][k], name=n(f"bmm_branch{k}_x")) for k in range(3)]
    g['w_brT'] = jnp.stack([mm(dproj[k], sv['Y3'], ta=True, b_cols=(k * BW, BW), name=n(f"bmm_branch{k}_w"))
                            for k in range(3)])
    m2_toks = [Tok(sv['ha'], BW, grad=F32), Tok(P, BW, 1, grad=MXU_DTYPE), Tok(sv['o'], BW, grad=F32), Tok(P, BW, 5, grad=MXU_DTYPE),
               Tok(sv['y0'], BW, grad=MXU_DTYPE), Tok(P, BW, COL_UC // BW, grad=F32)]
    m2_pars = [Par(p['bn4']), Par(p['c_d']), Par(p['wglu']), Par(p['bglu'])]
    (dha, dga, do, dz, dy0, duc0), (g['bn4'], g['c_d'], g['wglu'], g['bglu']) = stage_bwd(
        st_m2, m2_toks, m2_pars, [Tok(t, BW) for t in dY], cot_fn=st_m2_cot, name=n("b_branches"), **kw)
    dhs = mm(dy0, p['cbig'], tb=True, name=n("bmm_s5_out_x"))
    g['cbig'] = mm(sv['hs'], dy0, ta=True, name=n("bmm_s5_out_w"))
    lam_s = scan_cplx(p['ar'], -p['ai'], dhs, seq=seq, reverse=True, name=n("b_s5_scan"))
    _, (g['ar'], g['ai']) = stage_bwd(st_s5step, [Tok(sv['hs'], 2 * S5_CW, 0, True)],
                                      [Par(p['ar'], 'col', 0, S5_CW), Par(p['ai'], 'col', 0, S5_CW)],
                                      [Tok(lam_s, 2 * S5_CW)], ncol=C_CH // S5_CW, name=n("b_s5_decay"), **kw)
    duc = mm(lam_s, p['bbig'], tb=True, residual=duc0, out_dtype=MXU_DTYPE, name=n("bmm_s5_in_x"))
    g['bbig'] = mm(P, lam_s, ta=True, a_cols=(COL_UC, BW), name=n("bmm_s5_in_w"))
    lam_a = scan_real(sv['a'], dha, seq=seq, reverse=True, name=n("b_rglru_scan"))
    (dxa,), ga = stage_bwd(st_a, [Tok(P, BW, 0, True, MXU_DTYPE)], _a_pars(p), [Tok(lam_a, BW), Tok(sv['ha'], BW, 0, True)],
                           cot_fn=st_a_cot, name=n("b_rglru_pre"), **kw)
    g['a_conv_w'], g['a_conv_b'], g['wr'], g['b_r'], g['wi'], g['b_i'], g['lam'] = ga
    r3 = lambda t: t.reshape(nb, seq, BW)
    dd = delta_bwd(r3(sv['qn']), r3(sv['kn']), r3(sv['vv']), r3(sv['bB']), r3(sv['gB']), sv['states'], sv['tinvs'],
                   r3(do), name=n("b_delta"))
    (dq, dk, dv, dpba), (g['b_conv_w'], g['alog'], g['dtb']) = stage_bwd(
        st_b, [Tok(P, BW, 2, True, MXU_DTYPE), Tok(P, BW, 3, True, MXU_DTYPE), Tok(P, BW, 4, True, MXU_DTYPE), Tok(Pba, 128, grad=MXU_DTYPE)],
        [Par(p['b_conv_w']), Par(p['alog']), Par(p['dtb'])], [Tok(t.reshape(-1, BW), BW) for t in dd],
        name=n("b_delta_pre"), **kw)
    dP = jnp.concatenate([dxa, dga, dq, dk, dv, dz] + list(dgates) + [duc], axis=1)
    dh0 = mm(dpba, p['w_ba'], name=n("bmm_in_ba_x"))
    dh = mm(dP, p['w_big'], residual=dh0, name=n("bmm_in_x"))
    g['w_big'] = mm(dP, sv['h'], ta=True, name=n("bmm_in_w"))
    g['w_ba'] = mm(dpba, sv['h'], ta=True, name=n("bmm_in_ba_w"))
    (dx,), (g['mix_norm'],) = stage_bwd(st_norm, [Tok(sv['x'], d, grad=F32, add=dx1)], [Par(p['mix_norm'])],
                                        [Tok(dh, d)], name=n("b_norm_mix"), **kw)
    return dx, g


def local_step(x3d, mem3d, tgt3d, w, final_norm):
    nb, seq, d = x3d.shape
    mseq = mem3d.shape[1]
    x = x3d.reshape(nb * seq, d)
    mem = mem3d.reshape(nb * mseq, d)
    L = w['mix_norm'].shape[0]
    big, ba = split_w_in(w['w_inT'])
    dv, dv_vjp = jax.vjp(derive, {k: w[k] for k in DERIVE_FROM})
    ps, svs = [], []
    for l in range(L):
        p = layer_params(w, big, ba, dv, l)
        x, sv = layer_fwd(x, mem, p, seq, mseq, l)
        ps.append(p)
        svs.append(sv)
    loss, dx, g_final = loss_head(x, tgt3d.reshape(nb * seq, d), _row(final_norm), tb=TB, name="loss_head")
    gs = [None] * L
    for l in reversed(range(L)):
        dx, gs[l] = layer_bwd(dx, mem, ps[l], svs[l], seq, mseq, l)
    st = lambda k: jnp.stack([gs[l][k] for l in range(L)])
    vec = lambda k: st(k).reshape(L, -1)
    gd = dv_vjp({k: st(k) for k in ('wr', 'wi', 'alog', 'dtb', 'bn4', 'ar', 'ai', 'bbig', 'cbig')})[0]
    out = dict(gd)
    out.update(
        mix_norm=vec('mix_norm'), w_inT=merge_w_in(st('w_big'), st('w_ba')), b_gate=vec('b_gate'),
        a_conv_w=st('a_conv_w'), a_conv_b=vec('a_conv_b'), a_b_r=vec('b_r'), a_b_i=vec('b_i'), a_lam=vec('lam'),
        b_conv_w=st('b_conv_w'), c_d=vec('c_d'), c_glu_w=st('wglu'), c_glu_b=vec('bglu'),
        w_brT=st('w_brT'), w_out=st('w_out'), xa_norm=vec('xa_norm'), mem_norm=vec('mem_norm'),
        xa_w_q=st('w_q'), w_kvT=st('w_kvT'), xa_w_o=st('w_o'), ffn_norm=vec('ffn_norm'), w_upT=st('w_upT'),
        ffn_conv_w=st('ffn_conv_w'), ffn_conv_b=vec('ffn_conv_b'), ffn_w_down=st('w_down'),
        final_norm=g_final.reshape(-1))
    return loss, dx.reshape(nb, seq, d), out


LANES = 1024
ROW_PAD = 256


def _pack(vecs, dtype):
    flat = jnp.concatenate([v.reshape(-1).astype(dtype) for v in vecs])
    rows = -(-flat.shape[0] // (LANES * ROW_PAD)) * ROW_PAD
    return jnp.pad(flat, (0, rows * LANES - flat.shape[0])).reshape(rows, LANES)


BIG_LAYOUT = {'w_in': 'w_inT', 'xa_w_kv': 'w_kvT', 'ffn_w_up': 'w_upT', 'w_branch': 'w_brT'}
SEG_ALIGN = 16


def _to_layout(n, a):
    return jnp.swapaxes(a, -1, -2) if n in BIG_LAYOUT else a


def _seg_rows(shape):
    rows = int(np.prod(shape)) // LANES
    return rows, -(-rows // SEG_ALIGN) * SEG_ALIGN


def _pack_segments(mats, dtype):
    lead = mats[0].ndim - 2
    segs = []
    for m in mats:
        pad = -m.shape[-2] % SEG_ALIGN
        segs.append(jnp.pad(m.astype(dtype), [(0, 0)] * lead + [(0, pad), (0, 0)]))
    total = sum(s.shape[-2] for s in segs)
    tail = -total % ROW_PAD
    if tail:
        segs.append(jnp.zeros(segs[0].shape[:-2] + (tail, LANES), dtype))
    return jnp.concatenate(segs, axis=-2)


def _unpack_segments(buf, shapes):
    lead = buf.shape[:-2]
    out, off = [], 0
    for shp in shapes:
        rows, padded = _seg_rows(shp)
        out.append(buf[..., off:off + rows, :].reshape(lead + tuple(shp)))
        off += padded
    return out


def _unpack(flat, shapes):
    lead = flat.shape[:-2]
    flat = flat.reshape(lead + (-1,))
    out, off = [], 0
    for shp in shapes:
        cnt = int(np.prod(shp))
        out.append(flat[..., off:off + cnt].reshape(lead + tuple(shp)))
        off += cnt
    return out


def _join_shards(stacked, axis):
    t = jnp.moveaxis(stacked, 0, axis)
    shp = list(t.shape)
    return t.reshape(shp[:axis] + [shp[axis] * shp[axis + 1]] + shp[axis + 2:])


def _cut_shards(full, axis):
    shp = list(full.shape)
    t = full.reshape(shp[:axis] + [N_DEV, shp[axis] // N_DEV] + shp[axis + 1:])
    return jnp.moveaxis(t, axis, 0).reshape(N_DEV, -1)


def _as2d(a):
    return a.reshape(-1, a.shape[-1])


def kernel(x, mem, mix_norm, w_in, b_gate, a_conv_w, a_conv_b, a_w_r, a_b_r, a_w_i, a_b_i, a_lam, b_conv_w, b_a_log, b_dt_bias, b_norm, c_lam_re, c_lam_im, c_log_dt, c_b_re, c_b_im, c_c_re, c_c_im, c_d, c_glu_w, c_glu_b, w_branch, w_out, xa_norm, mem_norm, xa_w_q, xa_w_kv, xa_w_o, ffn_norm, ffn_w_up, ffn_conv_w, ffn_conv_b, ffn_w_down, final_norm, loss_target, m_mix_norm, m_w_in, m_b_gate, m_a_conv_w, m_a_conv_b, m_a_w_r, m_a_b_r, m_a_w_i, m_a_b_i, m_a_lam, m_b_conv_w, m_b_a_log, m_b_dt_bias, m_b_norm, m_c_lam_re, m_c_lam_im, m_c_log_dt, m_c_b_re, m_c_b_im, m_c_c_re, m_c_c_im, m_c_d, m_c_glu_w, m_c_glu_b, m_w_branch, m_w_out, m_xa_norm, m_mem_norm, m_xa_w_q, m_xa_w_kv, m_xa_w_o, m_ffn_norm, m_ffn_w_up, m_ffn_conv_w, m_ffn_conv_b, m_ffn_w_down, m_final_norm, v_mix_norm, v_w_in, v_b_gate, v_a_conv_w, v_a_conv_b, v_a_w_r, v_a_b_r, v_a_w_i, v_a_b_i, v_a_lam, v_b_conv_w, v_b_a_log, v_b_dt_bias, v_b_norm, v_c_lam_re, v_c_lam_im, v_c_log_dt, v_c_b_re, v_c_b_im, v_c_c_re, v_c_c_im, v_c_d, v_c_glu_w, v_c_glu_b, v_w_branch, v_w_out, v_xa_norm, v_mem_norm, v_xa_w_q, v_xa_w_kv, v_xa_w_o, v_ffn_norm, v_ffn_w_up, v_ffn_conv_w, v_ffn_conv_b, v_ffn_w_down, v_final_norm):
    args = (x, mem, mix_norm, w_in, b_gate, a_conv_w, a_conv_b, a_w_r, a_b_r, a_w_i, a_b_i, a_lam, b_conv_w, b_a_log, b_dt_bias, b_norm, c_lam_re, c_lam_im, c_log_dt, c_b_re, c_b_im, c_c_re, c_c_im, c_d, c_glu_w, c_glu_b, w_branch, w_out, xa_norm, mem_norm, xa_w_q, xa_w_kv, xa_w_o, ffn_norm, ffn_w_up, ffn_conv_w, ffn_conv_b, ffn_w_down, final_norm, loss_target, m_mix_norm, m_w_in, m_b_gate, m_a_conv_w, m_a_conv_b, m_a_w_r, m_a_b_r, m_a_w_i, m_a_b_i, m_a_lam, m_b_conv_w, m_b_a_log, m_b_dt_bias, m_b_norm, m_c_lam_re, m_c_lam_im, m_c_log_dt, m_c_b_re, m_c_b_im, m_c_c_re, m_c_c_im, m_c_d, m_c_glu_w, m_c_glu_b, m_w_branch, m_w_out, m_xa_norm, m_mem_norm, m_xa_w_q, m_xa_w_kv, m_xa_w_o, m_ffn_norm, m_ffn_w_up, m_ffn_conv_w, m_ffn_conv_b, m_ffn_w_down, m_final_norm, v_mix_norm, v_w_in, v_b_gate, v_a_conv_w, v_a_conv_b, v_a_w_r, v_a_b_r, v_a_w_i, v_a_b_i, v_a_lam, v_b_conv_w, v_b_a_log, v_b_dt_bias, v_b_norm, v_c_lam_re, v_c_lam_im, v_c_log_dt, v_c_b_re, v_c_b_im, v_c_c_re, v_c_c_im, v_c_d, v_c_glu_w, v_c_glu_b, v_w_branch, v_w_out, v_xa_norm, v_mem_norm, v_xa_w_q, v_xa_w_kv, v_xa_w_o, v_ffn_norm, v_ffn_w_up, v_ffn_conv_w, v_ffn_conv_b, v_ffn_w_down, v_final_norm)
    nw = len(WEIGHTS)
    x, mem = args[0], args[1]
    w_loc = dict(zip(WEIGHTS, args[2:2 + nw]))
    tgt = args[2 + nw]
    m_loc = dict(zip(WEIGHTS, args[3 + nw:3 + 2 * nw]))
    v_loc = dict(zip(WEIGHTS, args[3 + 2 * nw:3 + 3 * nw]))
    me = 4 * lax.axis_index("x") + 2 * lax.axis_index("y") + lax.axis_index("c")

    lay = {n: _to_layout(n, w_loc[n]) for n in BIG}
    lay_shapes = [lay[n].shape for n in BIG]
    gathered = _exchange(_pack_segments([lay[n].reshape(-1, LANES) for n in BIG], BF), scatter=False,
                         name="gather_matmul_weights")
    w = {n: a for n, a in w_loc.items() if n not in BIG}
    for n, st in zip(BIG, _unpack_segments(gathered, lay_shapes)):
        t = jnp.moveaxis(st, 0, -3)
        w[BIG_LAYOUT.get(n, n)] = t.reshape(t.shape[:-3] + (N_DEV * t.shape[-2], t.shape[-1]))
    ss_shapes = [w_loc[n].shape for n in SMALL_SHARDED]
    gathered_s = _exchange(_pack([w_loc[n] for n in SMALL_SHARDED], F32), scatter=False, name="gather_conv_weights")
    for n, st in zip(SMALL_SHARDED, _unpack(gathered_s, ss_shapes)):
        w[n] = _join_shards(st, SHARD_AXIS[n])

    final_norm = w.pop('final_norm')
    loss, grad_x, g = local_step(x, mem, tgt, w, final_norm)

    def cut(full):
        t = full.reshape(full.shape[:-2] + (N_DEV, full.shape[-2] // N_DEV, full.shape[-1]))
        return jnp.moveaxis(t, -3, 0).reshape(N_DEV, -1, LANES)

    send = _pack_segments([cut(g[BIG_LAYOUT.get(n, n)]) for n in BIG], BF)
    recv = _exchange(send, scatter=True, name="scatter_matmul_grads")
    g_lay = _unpack_segments(sum_slabs(recv, name="sum_matmul_grads"), lay_shapes)
    g_big = {n: _to_layout(n, a) for n, a in zip(BIG, g_lay)}
    small_full_shapes = [g[n].shape for n in SMALL]
    packed = _pack([g[n] for n in SMALL] + [loss[0, :1]], F32)
    everyones = _exchange(packed, scatter=False, name="gather_small_grads")
    summed = _unpack(sum_slabs(everyones, name="sum_small_grads"), small_full_shapes + [(1,)])
    loss_total = summed[-1].reshape(())
    g_small = {}
    for n, full in zip(SMALL, summed[:-1]):
        full = full.reshape(small_full_shapes[SMALL.index(n)])
        if n in SMALL_SHARDED:
            ax = SHARD_AXIS[n]
            loc = w_loc[n].shape[ax]
            full = lax.dynamic_slice_in_dim(full, me * loc, loc, axis=ax)
        g_small[n] = full.reshape(w_loc[n].shape)

    grads, delta, new_m, new_v = {}, {}, {}, {}
    for n in BIG:
        shp = w_loc[n].shape
        grads[n] = g_big[n]
        dl, nm, nv = adamw(_as2d(w_loc[n]), _as2d(g_big[n]), _as2d(m_loc[n]), _as2d(v_loc[n]), name=f"adamw_{n}")
        delta[n], new_m[n], new_v[n] = dl.reshape(shp), nm.reshape(shp), nv.reshape(shp)
    small_shapes = [w_loc[n].shape for n in SMALL]
    flat = [_pack([d[n] for n in SMALL], F32) for d in (w_loc, g_small, m_loc, v_loc)]
    res = adamw(*flat, name="adamw_small")
    for d, r in zip((delta, new_m, new_v), res):
        for n, a in zip(SMALL, _unpack(r, small_shapes)):
            d[n] = a
    grads.update(g_small)
    return (loss_total, grad_x, *[grads[n] for n in WEIGHTS], *[delta[n] for n in WEIGHTS],
            *[new_m[n] for n in WEIGHTS], *[new_v[n] for n in WEIGHTS])
```

```python
import functools
from typing import Any, NamedTuple

import jax
import jax.numpy as jnp
import numpy as np
from jax import lax
from jax.experimental import pallas as pl
from jax.experimental.pallas import tpu as pltpu

F32 = jnp.float32
BF = jnp.bfloat16
MXU_DTYPE = BF

EPS = 1e-6
RG_C = 8.0
N_DEV = 8
DEPTH = 4
D_MODEL = 1024
BW = 512
A_HEADS, A_HD = 8, 64
B_HEADS, B_DK = 4, 128
B_CHUNK = 64
C_GROUPS, C_GROUP, C_STATE = 32, 16, 64
C_CH = C_GROUPS * C_STATE
S5_CW = 512
X_HEADS, X_HD = 4, 256
D_FF = 3 * D_MODEL
ADAM_LR, ADAM_B1, ADAM_B2, ADAM_EPS, ADAM_WD, ADAM_STEP = 0.001, 0.9, 0.999, 1e-08, 0.01, 10

SUBLANES = 8
VMEM_LIMIT = 56 * 1024 * 1024

WEIGHTS = ['mix_norm', 'w_in', 'b_gate', 'a_conv_w', 'a_conv_b', 'a_w_r', 'a_b_r', 'a_w_i', 'a_b_i', 'a_lam',
           'b_conv_w', 'b_a_log', 'b_dt_bias', 'b_norm', 'c_lam_re', 'c_lam_im', 'c_log_dt', 'c_b_re', 'c_b_im',
           'c_c_re', 'c_c_im', 'c_d', 'c_glu_w', 'c_glu_b', 'w_branch', 'w_out', 'xa_norm', 'mem_norm', 'xa_w_q',
           'xa_w_kv', 'xa_w_o', 'ffn_norm', 'ffn_w_up', 'ffn_conv_w', 'ffn_conv_b', 'ffn_w_down', 'final_norm']
SHARD_AXIS = {'w_in': 2, 'a_conv_w': 2, 'b_conv_w': 2, 'c_glu_w': 1, 'w_branch': 3, 'w_out': 1, 'xa_w_q': 1,
              'xa_w_kv': 2, 'xa_w_o': 1, 'ffn_w_up': 2, 'ffn_conv_w': 2, 'ffn_w_down': 1}
BIG = ['w_in', 'c_glu_w', 'w_branch', 'w_out', 'xa_w_q', 'xa_w_kv', 'xa_w_o', 'ffn_w_up', 'ffn_w_down']
SMALL_SHARDED = ['a_conv_w', 'b_conv_w', 'ffn_conv_w']
SMALL = [n for n in WEIGHTS if n not in BIG]


def _dot(x, y, tx, ty):
    cx = 0 if tx else 1
    cy = 1 if ty else 0
    return lax.dot_general(x.astype(MXU_DTYPE), y.astype(MXU_DTYPE), (((cx,), (cy,)), ((), ())),
                           preferred_element_type=F32)


@functools.partial(jax.custom_vjp, nondiff_argnums=(2, 3))
def bmm(a, b, ta=False, tb=False):
    return _dot(a, b, ta, tb)


def _bmm_fwd(a, b, ta, tb):
    return _dot(a, b, ta, tb), (a, b)


def _bmm_bwd(ta, tb, res, g):
    a, b = res
    da = _dot(b, g, tb, True) if ta else _dot(g, b, False, not tb)
    db = _dot(g, a, True, ta) if tb else _dot(a, g, not ta, False)
    return da.astype(a.dtype), db.astype(b.dtype)


bmm.defvjp(_bmm_fwd, _bmm_bwd)


def _dotx(x, y, tx, ty):
    cx = 0 if tx else 1
    cy = 1 if ty else 0

    def d(p, q):
        return lax.dot_general(p, q, (((cx,), (cy,)), ((), ())), preferred_element_type=F32)

    xh, yh = x.astype(BF), y.astype(BF)
    xl, yl = (x - xh.astype(F32)).astype(BF), (y - yh.astype(F32)).astype(BF)
    return d(xh, yh) + (d(xh, yl) + d(xl, yh))


@functools.partial(jax.custom_vjp, nondiff_argnums=(2, 3))
def xmm(a, b, ta=False, tb=False):
    return _dotx(a, b, ta, tb)


def _xmm_fwd(a, b, ta, tb):
    return _dotx(a, b, ta, tb), (a, b)


def _xmm_bwd(ta, tb, res, g):
    a, b = res
    da = _dotx(b, g, tb, True) if ta else _dotx(g, b, False, not tb)
    db = _dotx(g, a, True, ta) if tb else _dotx(a, g, not ta, False)
    return da, db


xmm.defvjp(_xmm_fwd, _xmm_bwd)


@functools.partial(jax.custom_vjp, nondiff_argnums=(1,))
def roll_rows(x, s):
    return pltpu.roll(x, s, 0)


def _roll_rows_fwd(x, s):
    return pltpu.roll(x, s, 0), None


def _roll_rows_bwd(s, _, g):
    return (pltpu.roll(g, (g.shape[0] - s) % g.shape[0], 0),)


roll_rows.defvjp(_roll_rows_fwd, _roll_rows_bwd)


def shift_rows(cur_tail, s):
    cur, tail = cur_tail
    if s == 0:
        return cur
    rolled = roll_rows(cur, s)
    row = lax.broadcasted_iota(jnp.int32, tail.shape, 0)
    top = jnp.where(row < s, roll_rows(tail, s), rolled[:SUBLANES])
    if cur.shape[0] == SUBLANES:
        return top
    return jnp.concatenate([top, rolled[SUBLANES:]], axis=0)


def softplus(x):
    return jnp.maximum(x, 0.0) + jnp.log(1.0 + jnp.exp(-jnp.abs(x)))


def expm1(x):
    series = x * (1.0 + x * (0.5 + x * (1.0 / 6.0 + x * (1.0 / 24.0 + x * (1.0 / 120.0)))))
    return jnp.where(jnp.abs(x) < 0.05, series, jnp.exp(x) - 1.0)


def sigmoid(x):
    return 1.0 / (1.0 + jnp.exp(-x))


def silu(x):
    return x * sigmoid(x)


def gelu(x):
    return 0.5 * x * (1.0 + jnp.tanh(0.7978845608028654 * (x + 0.044715 * (x * x * x))))


def rms(x, g):
    var = jnp.mean(x * x, axis=-1, keepdims=True)
    return x * lax.rsqrt(var + EPS) * g


def cumsum_rows(x):
    n = x.shape[0]
    row = lax.broadcasted_iota(jnp.int32, x.shape, 0)
    s = 1
    while s < n:
        x = x + jnp.where(row >= s, roll_rows(x, s), 0.0)
        s *= 2
    return x


def _pick(n, prefs):
    for p in prefs:
        if n % p == 0:
            return p
    return n


def mm(a, b, *, ta=False, tb=False, a_cols=None, b_cols=None, residual=None, out_dtype=F32, name):
    a0, aw = a_cols if a_cols is not None else (0, a.shape[1])
    b0, bw = b_cols if b_cols is not None else (0, b.shape[1])
    if ta:
        K, M = a.shape[0], aw
    else:
        M, K = a.shape[0], aw
    if tb:
        N, Kb = b.shape[0], bw
    else:
        Kb, N = b.shape[0], bw
    assert K == Kb, (name, a.shape, b.shape, ta, tb)
    tm = _pick(M, (1024, 512, 256, 128))
    tk = _pick(K, (2048, 1024, 3328, 512, 256, 128))
    tn = _pick(N, (1024, 512, 256, 128)) if tk <= 1024 else _pick(N, (512, 256, 128))
    nk = K // tk

    def off(c0, t):
        assert c0 % t == 0, (name, c0, t)
        return c0 // t

    if ta:
        a_spec = pl.BlockSpec((tk, tm), lambda i, j, k, o=off(a0, tm): (k, i + o))
    else:
        a_spec = pl.BlockSpec((tm, tk), lambda i, j, k, o=off(a0, tk): (i, k + o))
    if tb:
        b_spec = pl.BlockSpec((tn, tk), lambda i, j, k, o=off(b0, tk): (j, k + o))
    else:
        b_spec = pl.BlockSpec((tk, tn), lambda i, j, k, o=off(b0, tn): (k, j + o))
    o_spec = pl.BlockSpec((tm, tn), lambda i, j, k: (i, j))
    in_specs = [a_spec, b_spec]
    args = [a, b]
    if residual is not None:
        in_specs.append(o_spec)
        args.append(residual)

    def body(*refs):
        a_ref, b_ref = refs[0], refs[1]
        r_ref = refs[2] if residual is not None else None
        o_ref = refs[3] if residual is not None else refs[2]
        part = _dot(a_ref[...], b_ref[...], ta, tb)

        def finish(acc):
            if r_ref is not None:
                acc = acc + r_ref[...].astype(F32)
            o_ref[...] = acc.astype(out_dtype)

        if nk == 1:
            finish(part)
        else:
            acc_ref = refs[-1]
            k = pl.program_id(2)

            @pl.when(k == 0)
            def _():
                acc_ref[...] = part

            @pl.when(k > 0)
            def _():
                acc_ref[...] += part

            @pl.when(k == nk - 1)
            def _():
                finish(acc_ref[...])

    return pl.pallas_call(
        body, name=name, grid=(M // tm, N // tn, nk),
        in_specs=in_specs, out_specs=o_spec,
        out_shape=jax.ShapeDtypeStruct((M, N), out_dtype),
        scratch_shapes=[pltpu.VMEM((tm, tn), F32)] if nk > 1 else [],
        compiler_params=pltpu.CompilerParams(dimension_semantics=("parallel", "parallel", "arbitrary"),
                                             vmem_limit_bytes=VMEM_LIMIT),
    )(*args)


class Tok(NamedTuple):
    arr: Any
    width: int
    col: int = 0
    halo: bool = False
    grad: Any = None
    add: Any = None


class Par(NamedTuple):
    arr: Any
    kind: str = 'const'
    col: int = 0
    width: int = 0
    grad: bool = True


def _tok_specs(toks, tb, rev, ntile):
    specs, args = [], []
    for t in toks:
        if rev:
            cur = lambda j, s, c=t.col: (ntile - 1 - s, c + j)
            tail = lambda j, s, c=t.col: (jnp.maximum((ntile - 1 - s) * (tb // SUBLANES) - 1, 0), c + j)
        else:
            cur = lambda j, s, c=t.col: (s, c + j)
            tail = lambda j, s, c=t.col: (jnp.maximum(s * (tb // SUBLANES) - 1, 0), c + j)
        specs.append(pl.BlockSpec((tb, t.width), cur))
        args.append(t.arr)
        if t.halo:
            specs.append(pl.BlockSpec((SUBLANES, t.width), tail))
            args.append(t.arr)
    return specs, args


def _par_specs(pars, tpb, rev, ntile):
    specs, args = [], []
    for p in pars:
        if p.kind == 'const':
            specs.append(pl.BlockSpec(p.arr.shape, lambda j, s: (0, 0)))
        elif p.kind == 'col':
            specs.append(pl.BlockSpec((p.arr.shape[0], p.width), lambda j, s, c=p.col: (0, c + j)))
        else:
            if rev:
                specs.append(pl.BlockSpec((None,) + p.arr.shape[1:], lambda j, s: ((ntile - 1 - s) // tpb, 0, 0)))
            else:
                specs.append(pl.BlockSpec((None,) + p.arr.shape[1:], lambda j, s: (s // tpb, 0, 0)))
        args.append(p.arr)
    return specs, args


def _rows(ref, r0, n):
    return ref[...] if isinstance(r0, int) else ref[pl.ds(r0, n), :]


def _read_toks(toks, refs, t0, r0, sub):
    vals, k = [], 0
    for t in toks:
        ref = refs[k]
        k += 1
        cur = _rows(ref, r0, sub)
        if t.halo:
            tail = jnp.where(t0 == 0, jnp.zeros_like(refs[k][...]), refs[k][...])
            k += 1
            if not isinstance(r0, int):
                before = ref[pl.ds(pl.multiple_of(jnp.maximum(r0 - SUBLANES, 0), SUBLANES), SUBLANES), :]
                tail = jnp.where(r0 == 0, tail, before)
            vals.append((cur, tail))
        else:
            vals.append(cur)
    return vals, k


def _row_blocks(tb, sub, reverse, block):
    if sub is None or sub >= tb:
        block(0)
        return
    nsub = tb // sub

    def step(n, carry):
        r = nsub - 1 - n if reverse else n
        block(pl.multiple_of(r * sub, sub))
        return carry

    lax.fori_loop(0, nsub, step, 0)


def stage_fwd(fn, toks, pars, outs, *, seq, tb, ncol=1, sub=None, name):
    T = toks[0].arr.shape[0]
    tb = min(tb, seq)
    sub = SUB_ROWS.get(fn) if sub is None else sub
    ntile, tpb = T // tb, seq // tb
    tspecs, targs = _tok_specs(toks, tb, False, ntile)
    pspecs, pargs = _par_specs(pars, tpb, False, ntile)
    n_in = len(tspecs) + len(pspecs)

    def body(*refs):
        s = pl.program_id(1)
        t0 = (s % tpb) * tb
        pvals = [r[...] for r in refs[len(tspecs):n_in]]

        def block(r0):
            tvals, _ = _read_toks(toks, refs, t0, r0, sub)
            res = fn(t0 + r0, *tvals, *pvals)
            for r, v in zip(refs[n_in:], res):
                if isinstance(r0, int):
                    r[...] = v.astype(r.dtype)
                else:
                    r[pl.ds(r0, sub), :] = v.astype(r.dtype)

        _row_blocks(tb, sub, False, block)

    return pl.pallas_call(
        body, name=name, grid=(ncol, ntile),
        in_specs=tspecs + pspecs,
        out_specs=[pl.BlockSpec((tb, w), lambda j, s: (s, j)) for w, _ in outs],
        out_shape=[jax.ShapeDtypeStruct((T, w * ncol), d) for w, d in outs],
        compiler_params=pltpu.CompilerParams(dimension_semantics=("arbitrary", "arbitrary"),
                                             vmem_limit_bytes=VMEM_LIMIT),
    )(*targs, *pargs)


def stage_bwd(fn, toks, pars, cots, *, cot_fn=None, seq, tb, ncol=1, sub=None, name):
    T = toks[0].arr.shape[0]
    tb = min(tb, seq)
    sub = SUB_ROWS.get(fn) if sub is None else sub
    ntile, tpb = T // tb, seq // tb
    tspecs, targs = _tok_specs(toks, tb, True, ntile)
    pspecs, pargs = _par_specs(pars, tpb, True, ntile)
    cspecs, cargs = _tok_specs(cots, tb, True, ntile)
    adds = [t for t in toks if t.add is not None]
    assert all(t.grad is not None for t in adds)
    aspecs = [pl.BlockSpec((tb, t.width), lambda j, s: (ntile - 1 - s, j)) for t in adds]
    aargs = [t.add for t in adds]
    n_t, n_p, n_c = len(tspecs), len(pspecs), len(cspecs)

    gtoks = [t for t in toks if t.grad is not None]
    gpars = [p for p in pars if p.grad]
    out_specs, out_shape = [], []
    for t in gtoks:
        out_specs.append(pl.BlockSpec((tb, t.width), lambda j, s: (ntile - 1 - s, j)))
        out_shape.append(jax.ShapeDtypeStruct((T, t.width * ncol), t.grad))
    for p in gpars:
        if p.kind == 'const':
            out_specs.append(pl.BlockSpec(p.arr.shape, lambda j, s: (0, 0)))
            out_shape.append(jax.ShapeDtypeStruct(p.arr.shape, F32))
        elif p.kind == 'col':
            out_specs.append(pl.BlockSpec((p.arr.shape[0], p.width), lambda j, s: (0, j)))
            out_shape.append(jax.ShapeDtypeStruct((p.arr.shape[0], p.width * ncol), F32))
        else:
            out_specs.append(pl.BlockSpec((None,) + p.arr.shape[1:], lambda j, s: ((ntile - 1 - s) // tpb, 0, 0)))
            out_shape.append(jax.ShapeDtypeStruct(p.arr.shape, F32))
    carries = [t for t in gtoks if t.halo]
    scratch = [pltpu.VMEM((SUBLANES, t.width), F32) for t in carries]

    def body(*refs):
        j, s = pl.program_id(0), pl.program_id(1)
        i = ntile - 1 - s
        t0 = (i % tpb) * tb
        t_refs = refs[:n_t]
        p_refs = refs[n_t:n_t + n_p]
        c_refs = refs[n_t + n_p:n_t + n_p + n_c]
        a_refs = refs[n_t + n_p + n_c:n_t + n_p + n_c + len(adds)]
        o_refs = refs[n_t + n_p + n_c + len(adds):]
        gt_refs = o_refs[:len(gtoks)]
        gp_refs = o_refs[len(gtoks):len(gtoks) + len(gpars)]
        carry_refs = o_refs[len(gtoks) + len(gpars):]

        pvals = [r[...] for r in p_refs]

        @pl.when(s == 0)
        def _():
            for carry in carry_refs:
                carry[...] = jnp.zeros_like(carry)

        for p, ref in zip(gpars, gp_refs):
            if p.kind == 'const':
                first = jnp.logical_and(j == 0, s == 0)
            elif p.kind == 'col':
                first = s == 0
            else:
                first = s % tpb == 0

            @pl.when(first)
            def _(ref=ref):
                ref[...] = jnp.zeros_like(ref)

        def block(r0):
            rows = tb if isinstance(r0, int) else sub
            tt = t0 + r0
            tvals, _ = _read_toks(toks, t_refs, t0, r0, sub)
            cvals, _ = _read_toks(cots, c_refs, t0, r0, sub)

            def f(tv, pv):
                return tuple(fn(tt, *tv, *pv))

            res, vjp = jax.vjp(f, tvals, pvals)
            ct = cot_fn(tt, *cvals) if cot_fn is not None else tuple(cvals)
            ct = tuple(c.astype(r.dtype) for c, r in zip(ct, res))
            dt, dp = vjp(ct)

            ci = 0
            ai = 0
            gi = 0
            for t, d in zip(toks, dt):
                if t.grad is None:
                    continue
                ref = gt_refs[gi]
                gi += 1
                if t.halo:
                    dcur, dtail = d
                    carry = carry_refs[ci]
                    ci += 1
                    top = dcur[:rows - SUBLANES] if rows > SUBLANES else None
                    bot = dcur[rows - SUBLANES:] + carry[...]
                    dcur = bot if top is None else jnp.concatenate([top, bot], axis=0)
                    carry[...] = jnp.where(tt == 0, jnp.zeros_like(dtail), dtail)
                else:
                    dcur = d
                if t.add is not None:
                    dcur = dcur + _rows(a_refs[ai], r0, rows).astype(F32)
                    ai += 1
                if isinstance(r0, int):
                    ref[...] = dcur.astype(ref.dtype)
                else:
                    ref[pl.ds(r0, rows), :] = dcur.astype(ref.dtype)

            gi = 0
            for p, d in zip(pars, dp):
                if not p.grad:
                    continue
                gp_refs[gi][...] += d.astype(F32)
                gi += 1

        _row_blocks(tb, sub, True, block)

    res = pl.pallas_call(
        body, name=name, grid=(ncol, ntile),
        in_specs=tspecs + pspecs + cspecs + aspecs,
        out_specs=out_specs, out_shape=out_shape, scratch_shapes=scratch,
        compiler_params=pltpu.CompilerParams(dimension_semantics=("arbitrary", "arbitrary"),
                                             vmem_limit_bytes=VMEM_LIMIT),
    )(*targs, *pargs, *cargs, *aargs)
    return list(res[:len(gtoks)]), list(res[len(gtoks):])


def _bcast_row(x, r):
    return jnp.broadcast_to(x[r:r + 1, :], x.shape)


SCAN_TB = 512


def scan_real(a, b, *, seq, reverse, name):
    T, C = a.shape
    tb = min(SCAN_TB, seq)
    nb, nt, nblk = T // seq, seq // tb, tb // SUBLANES

    def body(a_ref, b_ref, h_ref, carry_h, carry_a):
        @pl.when(pl.program_id(1) == 0)
        def _():
            carry_h[...] = jnp.zeros_like(carry_h)
            carry_a[...] = jnp.zeros_like(carry_a)

        row = lax.broadcasted_iota(jnp.int32, (SUBLANES, C), 0)

        def blk(n, c):
            ch, ca = c
            k = nblk - 1 - n if reverse else n
            o = pl.multiple_of(k * SUBLANES, SUBLANES)
            A = a_ref[pl.ds(o, SUBLANES), :]
            B = b_ref[pl.ds(o, SUBLANES), :]
            if reverse:
                a_first = _bcast_row(A, 0)
                A = jnp.where(row == SUBLANES - 1, ca, pltpu.roll(A, SUBLANES - 1, 0))
                for s in (1, 2, 4):
                    keep = row < SUBLANES - s
                    Bs = jnp.where(keep, pltpu.roll(B, SUBLANES - s, 0), 0.0)
                    As = jnp.where(keep, pltpu.roll(A, SUBLANES - s, 0), 1.0)
                    B = B + A * Bs
                    A = A * As
                h = B + A * ch
                h_ref[pl.ds(o, SUBLANES), :] = h
                return _bcast_row(h, 0), a_first
            for s in (1, 2, 4):
                keep = row >= s
                Bs = jnp.where(keep, pltpu.roll(B, s, 0), 0.0)
                As = jnp.where(keep, pltpu.roll(A, s, 0), 1.0)
                B = B + A * Bs
                A = A * As
            h = B + A * ch
            h_ref[pl.ds(o, SUBLANES), :] = h
            return _bcast_row(h, SUBLANES - 1), ca

        ch, ca = lax.fori_loop(0, nblk, blk, (carry_h[...], carry_a[...]))
        carry_h[...] = ch
        carry_a[...] = ca

    if reverse:
        spec = pl.BlockSpec((tb, C), lambda bi, i: (bi * nt + nt - 1 - i, 0))
    else:
        spec = pl.BlockSpec((tb, C), lambda bi, i: (bi * nt + i, 0))
    return pl.pallas_call(
        body, name=name, grid=(nb, nt), in_specs=[spec, spec], out_specs=spec,
        out_shape=jax.ShapeDtypeStruct((T, C), F32),
        scratch_shapes=[pltpu.VMEM((SUBLANES, C), F32), pltpu.VMEM((SUBLANES, C), F32)],
        compiler_params=pltpu.CompilerParams(dimension_semantics=("arbitrary", "arbitrary"),
                                             vmem_limit_bytes=VMEM_LIMIT),
    )(a, b)


def scan_cplx(ar, ai, bu, *, seq, reverse, name):
    T = bu.shape[0]
    cw = S5_CW
    ncol = C_CH // cw
    tb = min(SCAN_TB, seq)
    nb, nt, nblk = T // seq, seq // tb, tb // SUBLANES

    def body(ar_ref, ai_ref, b_ref, h_ref, carry_r, carry_i):
        @pl.when(pl.program_id(2) == 0)
        def _():
            carry_r[...] = jnp.zeros_like(carry_r)
            carry_i[...] = jnp.zeros_like(carry_i)

        row = lax.broadcasted_iota(jnp.int32, (SUBLANES, cw), 0)
        Ar = jnp.broadcast_to(ar_ref[...], (SUBLANES, cw))
        Ai = jnp.broadcast_to(ai_ref[...], (SUBLANES, cw))
        levels = []
        for s in (1, 2, 4):
            keep = (row < SUBLANES - s) if reverse else (row >= s)
            sh = SUBLANES - s if reverse else s
            levels.append((Ar, Ai, keep, sh))
            Asr = jnp.where(keep, pltpu.roll(Ar, sh, 0), 1.0)
            Asi = jnp.where(keep, pltpu.roll(Ai, sh, 0), 0.0)
            Ar, Ai = Ar * Asr - Ai * Asi, Ar * Asi + Ai * Asr

        def blk(n, c):
            cr, ci = c
            k = nblk - 1 - n if reverse else n
            o = pl.multiple_of(k * SUBLANES, SUBLANES)
            Br = b_ref[pl.ds(o, SUBLANES), :cw]
            Bi = b_ref[pl.ds(o, SUBLANES), cw:]
            for lr, li, keep, sh in levels:
                Bsr = jnp.where(keep, pltpu.roll(Br, sh, 0), 0.0)
                Bsi = jnp.where(keep, pltpu.roll(Bi, sh, 0), 0.0)
                Br, Bi = Br + lr * Bsr - li * Bsi, Bi + lr * Bsi + li * Bsr
            hr = Br + Ar * cr - Ai * ci
            hi = Bi + Ar * ci + Ai * cr
            h_ref[pl.ds(o, SUBLANES), :cw] = hr
            h_ref[pl.ds(o, SUBLANES), cw:] = hi
            last = 0 if reverse else SUBLANES - 1
            return _bcast_row(hr, last), _bcast_row(hi, last)

        cr, ci = lax.fori_loop(0, nblk, blk, (carry_r[...], carry_i[...]))
        carry_r[...] = cr
        carry_i[...] = ci

    if reverse:
        spec = pl.BlockSpec((tb, 2 * cw), lambda bi, j, i: (bi * nt + nt - 1 - i, j))
    else:
        spec = pl.BlockSpec((tb, 2 * cw), lambda bi, j, i: (bi * nt + i, j))
    aspec = pl.BlockSpec((1, cw), lambda bi, j, i: (0, j))
    return pl.pallas_call(
        body, name=name, grid=(nb, ncol, nt), in_specs=[aspec, aspec, spec], out_specs=spec,
        out_shape=jax.ShapeDtypeStruct((T, 2 * C_CH), F32),
        scratch_shapes=[pltpu.VMEM((SUBLANES, cw), F32), pltpu.VMEM((SUBLANES, cw), F32)],
        compiler_params=pltpu.CompilerParams(dimension_semantics=("arbitrary", "arbitrary", "arbitrary"),
                                             vmem_limit_bytes=VMEM_LIMIT),
    )(ar, ai, bu)


def _col_sums(x):
    ones = jnp.ones((x.shape[0], x.shape[0]), BF)
    acc, rest = None, x
    for _ in range(3):
        piece = rest.astype(BF)
        rest = rest - piece.astype(F32)
        term = lax.dot_general(ones, piece, (((1,), (0,)), ((), ())), preferred_element_type=F32)
        acc = term if acc is None else acc + term
    return acc


@jax.custom_vjp
def col_sums(x):
    return _col_sums(x)


col_sums.defvjp(lambda x: (_col_sums(x), None), lambda _, g: (_col_sums(g),))


def tri_inv(a):
    ii = lax.broadcasted_iota(jnp.int32, a.shape, 0)
    jj = lax.broadcasted_iota(jnp.int32, a.shape, 1)
    c = a.shape[0]
    tinv = jnp.where(ii == jj, 1.0, 0.0) - a
    p = _dotx(a, a, False, False)
    for _ in range(4):
        both = _dotx(jnp.concatenate([tinv, p], axis=0), p, False, False)
        tinv = tinv + both[:c]
        p = both[c:]
    return tinv + _dotx(tinv, p, False, False)


@jax.custom_vjp
def tri_inv_saved(a, t):
    return t


tri_inv_saved.defvjp(lambda a, t: (t, t),
                     lambda t, g: (-_dotx(_dotx(t, g, True, False), t, False, True), jnp.zeros_like(t)))


def delta_chunk(q, k, v, bB, gB, S, tinv=None):
    c = B_CHUNK
    qc = q * (B_DK ** -0.5)
    gc = cumsum_rows(gB)
    ii = lax.broadcasted_iota(jnp.int32, (c, c), 0)
    jj = lax.broadcasted_iota(jnp.int32, (c, c), 1)
    incl, strict, diag = ii >= jj, ii > jj, ii == jj
    gcol = gc[:, :c]
    grow = col_sums(jnp.where(diag, gcol, 0.0))
    decay = jnp.exp(jnp.where(incl, gcol - grow, -1e30))
    kb = k * bB
    a_mat = jnp.where(strict, bmm(kb, k, False, True) * decay, 0.0)
    tinv = tri_inv(a_mat) if tinv is None else tri_inv_saved(a_mat, tinv)
    eg = jnp.exp(gc)
    sol = xmm(tinv, jnp.concatenate([v * bB, kb * eg], axis=-1))
    u, w = sol[:, :B_DK], sol[:, B_DK:]
    qk = jnp.where(incl, bmm(qc, k, False, True) * decay, 0.0)
    glast = gc[c - 1:c, :]
    k_dec = k * jnp.exp(glast - gc)
    v_new = u - bmm(w, S)
    o = bmm(qc * eg, S) + bmm(qk, v_new)
    s_new = S * jnp.exp(glast) + bmm(k_dec, v_new, True, False)
    return o, s_new, tinv


def delta_fwd(q, k, v, bB, gB, *, name):
    nb, seq, _ = q.shape
    n = seq // B_CHUNK
    hd = B_DK

    def body(q_ref, k_ref, v_ref, b_ref, g_ref, o_ref, st_ref, ti_ref, state):
        @pl.when(pl.program_id(0) == 0)
        def _():
            state[...] = jnp.zeros_like(state)

        for b in range(nb):
            for h in range(B_HEADS):
                sl = slice(h * hd, (h + 1) * hd)
                s_old = state[b * B_HEADS + h]
                st_ref[b, h] = s_old
                o, s_new, tinv = delta_chunk(q_ref[b, :, sl], k_ref[b, :, sl], v_ref[b, :, sl], b_ref[b, :, sl],
                                             g_ref[b, :, sl], s_old)
                o_ref[b, :, sl] = o
                ti_ref[b, h] = tinv
                state[b * B_HEADS + h] = s_new

    spec = pl.BlockSpec((nb, B_CHUNK, BW), lambda i: (0, i, 0))
    return pl.pallas_call(
        body, name=name, grid=(n,), in_specs=[spec] * 5,
        out_specs=[spec, pl.BlockSpec((nb, None, B_HEADS, hd, hd), lambda i: (0, i, 0, 0, 0)),
                   pl.BlockSpec((nb, None, B_HEADS, B_CHUNK, B_CHUNK), lambda i: (0, i, 0, 0, 0))],
        out_shape=[jax.ShapeDtypeStruct((nb, seq, BW), F32), jax.ShapeDtypeStruct((nb, n, B_HEADS, hd, hd), F32),
                   jax.ShapeDtypeStruct((nb, n, B_HEADS, B_CHUNK, B_CHUNK), F32)],
        scratch_shapes=[pltpu.VMEM((nb * B_HEADS, hd, hd), F32)],
        compiler_params=pltpu.CompilerParams(dimension_semantics=("arbitrary",), vmem_limit_bytes=VMEM_LIMIT),
    )(q, k, v, bB, gB)


def delta_bwd(q, k, v, bB, gB, states, tinvs, do, *, name):
    nb, seq, _ = q.shape
    n = seq // B_CHUNK
    hd = B_DK

    def body(q_ref, k_ref, v_ref, b_ref, g_ref, st_ref, ti_ref, do_ref, dq_ref, dk_ref, dv_ref, db_ref, dg_ref, dstate):
        @pl.when(pl.program_id(0) == 0)
        def _():
            dstate[...] = jnp.zeros_like(dstate)

        for b in range(nb):
            for h in range(B_HEADS):
                sl = slice(h * hd, (h + 1) * hd)

                def f(qv, kv, vv, bv, gv, sv, tv=ti_ref[b, h]):
                    return delta_chunk(qv, kv, vv, bv, gv, sv, tv)[:2]

                _, vjp = jax.vjp(f, q_ref[b, :, sl], k_ref[b, :, sl], v_ref[b, :, sl], b_ref[b, :, sl],
                                 g_ref[b, :, sl], st_ref[b, h])
                dq, dk, dv, db, dg, ds = vjp((do_ref[b, :, sl], dstate[b * B_HEADS + h]))
                dq_ref[b, :, sl] = dq
                dk_ref[b, :, sl] = dk
                dv_ref[b, :, sl] = dv
                db_ref[b, :, sl] = db
                dg_ref[b, :, sl] = dg
                dstate[b * B_HEADS + h] = ds

    spec = pl.BlockSpec((nb, B_CHUNK, BW), lambda i: (0, n - 1 - i, 0))
    sspec = pl.BlockSpec((nb, None, B_HEADS, hd, hd), lambda i: (0, n - 1 - i, 0, 0, 0))
    tspec = pl.BlockSpec((nb, None, B_HEADS, B_CHUNK, B_CHUNK), lambda i: (0, n - 1 - i, 0, 0, 0))
    return pl.pallas_call(
        body, name=name, grid=(n,), in_specs=[spec] * 5 + [sspec, tspec, spec],
        out_specs=[spec] * 5, out_shape=[jax.ShapeDtypeStruct((nb, seq, BW), F32)] * 5,
        scratch_shapes=[pltpu.VMEM((nb * B_HEADS, hd, hd), F32)],
        compiler_params=pltpu.CompilerParams(dimension_semantics=("arbitrary",), vmem_limit_bytes=VMEM_LIMIT),
    )(q, k, v, bB, gB, states, tinvs, do)


def loss_head(x, tgt, g, *, tb, name):
    T, D = x.shape
    tb = min(tb, T)
    nt = T // tb

    def body(x_ref, t_ref, g_ref, l_ref, dx_ref, dg_ref):
        tg = t_ref[...]

        def f(xv, gv):
            err = rms(xv, gv) - tg
            return 0.5 * jnp.mean(err * err, axis=-1, keepdims=True)

        rows, vjp = jax.vjp(f, x_ref[...], g_ref[...])
        dx, dg = vjp(jnp.ones_like(rows))
        dx_ref[...] = dx
        tot = jnp.broadcast_to(jnp.sum(rows, axis=0, keepdims=True), (1, 128))

        @pl.when(pl.program_id(0) == 0)
        def _():
            l_ref[...] = tot
            dg_ref[...] = dg

        @pl.when(pl.program_id(0) > 0)
        def _():
            l_ref[...] += tot
            dg_ref[...] += dg

    tok = pl.BlockSpec((tb, D), lambda i: (i, 0))
    return pl.pallas_call(
        body, name=name, grid=(nt,),
        in_specs=[tok, tok, pl.BlockSpec((1, D), lambda i: (0, 0))],
        out_specs=[pl.BlockSpec((1, 128), lambda i: (0, 0)), tok, pl.BlockSpec((1, D), lambda i: (0, 0))],
        out_shape=[jax.ShapeDtypeStruct((1, 128), F32), jax.ShapeDtypeStruct((T, D), F32), jax.ShapeDtypeStruct((1, D), F32)],
        compiler_params=pltpu.CompilerParams(dimension_semantics=("arbitrary",), vmem_limit_bytes=VMEM_LIMIT),
    )(x, tgt, g)


def _row_block(rows, cols):
    budget = 256 * 1024
    tr = max(SUBLANES, min(rows, budget // max(cols, 1)) // SUBLANES * SUBLANES)
    while rows % tr:
        tr -= SUBLANES
        if tr <= 0:
            return rows
    return tr


def adamw(w, g, m, v, *, name):
    R, C = w.shape
    tr = _row_block(R, C)
    c1 = 1.0 / (1.0 - ADAM_B1 ** ADAM_STEP)
    c2 = 1.0 / (1.0 - ADAM_B2 ** ADAM_STEP)

    def body(w_ref, g_ref, m_ref, v_ref, d_ref, mo_ref, vo_ref):
        gv = g_ref[...]
        mn = ADAM_B1 * m_ref[...] + (1.0 - ADAM_B1) * gv
        vn = ADAM_B2 * v_ref[...] + (1.0 - ADAM_B2) * (gv * gv)
        d_ref[...] = -ADAM_LR * ((mn * c1) / (jnp.sqrt(vn * c2) + ADAM_EPS) + ADAM_WD * w_ref[...])
        mo_ref[...] = mn
        vo_ref[...] = vn

    spec = pl.BlockSpec((tr, C), lambda i: (i, 0))
    return pl.pallas_call(
        body, name=name, grid=(R // tr,), in_specs=[spec] * 4, out_specs=[spec] * 3,
        out_shape=[jax.ShapeDtypeStruct((R, C), F32)] * 3,
        compiler_params=pltpu.CompilerParams(dimension_semantics=("parallel",), vmem_limit_bytes=VMEM_LIMIT),
    )(w, g, m, v)


def sum_slabs(x, *, name):
    n, R, C = x.shape
    tr = _row_block(R, C * 2)

    def body(x_ref, o_ref):
        acc = x_ref[0].astype(F32)
        for d in range(1, n):
            acc = acc + x_ref[d].astype(F32)
        o_ref[...] = acc

    return pl.pallas_call(
        body, name=name, grid=(R // tr,),
        in_specs=[pl.BlockSpec((n, tr, C), lambda i: (0, i, 0))],
        out_specs=pl.BlockSpec((tr, C), lambda i: (i, 0)),
        out_shape=jax.ShapeDtypeStruct((R, C), F32),
        compiler_params=pltpu.CompilerParams(dimension_semantics=("parallel",), vmem_limit_bytes=VMEM_LIMIT),
    )(x)


def _gather(src, *, name):
    R, C = src.shape

    def body(src_ref, out_ref, send_sems, recv_sems, local_sem):
        x, y, c = lax.axis_index("x"), lax.axis_index("y"), lax.axis_index("c")
        me, sibling = (x, y, c), (x, y, 1 - c)
        chips = [(1 - x, y), (x, 1 - y), (1 - x, 1 - y)]

        def slab(px, py, pc):
            return out_ref.at[4 * px + 2 * py + pc]

        def copy(k, block, to, first_hand=False):
            return pltpu.make_async_remote_copy(
                src_ref=src_ref if first_hand else slab(*block), dst_ref=slab(*block),
                send_sem=send_sems.at[k], recv_sem=recv_sems.at[k],
                device_id=to, device_id_type=pl.DeviceIdType.MESH)

        mine = pltpu.make_async_copy(src_ref, slab(*me), local_sem)
        mine.start()
        first = [copy(0, me, sibling, True)] + [copy(1 + j, me, (*chip, c), True) for j, chip in enumerate(chips)]
        for cp in first:
            cp.start()
        passed = [copy(4 + j, (*chip, c), sibling) for j, chip in enumerate(chips)]
        for j, chip in enumerate(chips):
            copy(1 + j, (*chip, c), me).wait_recv()
            passed[j].start()
        copy(0, sibling, me).wait_recv()
        for j, chip in enumerate(chips):
            copy(4 + j, (*chip, 1 - c), me).wait_recv()
        for cp in first + passed:
            cp.wait_send()
        mine.wait()

    return pl.pallas_call(
        body, name=name,
        in_specs=[pl.BlockSpec(memory_space=pl.ANY)],
        out_specs=pl.BlockSpec(memory_space=pl.ANY),
        out_shape=jax.ShapeDtypeStruct((N_DEV, R, C), src.dtype),
        scratch_shapes=[pltpu.SemaphoreType.DMA((N_DEV - 1,)), pltpu.SemaphoreType.DMA((N_DEV - 1,)),
                        pltpu.SemaphoreType.DMA],
    )(src)


def _exchange(src, *, scatter, name):
    R, C = src.shape[-2:]

    def body(src_ref, out_ref, send_sems, recv_sems, local_sem):
        x, y, c = lax.axis_index("x"), lax.axis_index("y"), lax.axis_index("c")
        me = 4 * x + 2 * y + c

        def peer(k):
            return (x ^ ((k >> 2) & 1), y ^ ((k >> 1) & 1), c ^ (k & 1))

        def peer_index(k):
            px, py, pc = peer(k)
            return 4 * px + 2 * py + pc

        mine = pltpu.make_async_copy(src_ref.at[me] if scatter else src_ref, out_ref.at[me], local_sem)
        mine.start()
        copies = []
        for k in range(1, N_DEV):
            cp = pltpu.make_async_remote_copy(
                src_ref=src_ref.at[peer_index(k)] if scatter else src_ref,
                dst_ref=out_ref.at[me],
                send_sem=send_sems.at[k - 1], recv_sem=recv_sems.at[k - 1],
                device_id=peer(k), device_id_type=pl.DeviceIdType.MESH)
            cp.start()
            copies.append(cp)
        for k in range(1, N_DEV):
            pltpu.make_async_remote_copy(
                src_ref=src_ref.at[peer_index(k)] if scatter else src_ref,
                dst_ref=out_ref.at[peer_index(k)],
                send_sem=send_sems.at[k - 1], recv_sem=recv_sems.at[k - 1],
                device_id=peer(k), device_id_type=pl.DeviceIdType.MESH).wait_recv()
        for cp in copies:
            cp.wait_send()
        mine.wait()

    return pl.pallas_call(
        body, name=name,
        in_specs=[pl.BlockSpec(memory_space=pl.ANY)],
        out_specs=pl.BlockSpec(memory_space=pl.ANY),
        out_shape=jax.ShapeDtypeStruct((N_DEV, R, C), src.dtype),
        scratch_shapes=[pltpu.SemaphoreType.DMA((N_DEV - 1,)), pltpu.SemaphoreType.DMA((N_DEV - 1,)),
                        pltpu.SemaphoreType.DMA],
    )(src)


TB = 256


def st_norm(t0, x, g):
    return (rms(x.astype(F32), g),)


def _conv(xt, w, bias=None):
    kk = w.shape[0]
    acc = bias
    for i in range(kk):
        term = w[i:i + 1, :] * shift_rows(xt, kk - 1 - i)
        acc = term if acc is None else acc + term
    return acc


def st_a(t0, xa, cw, cb, wr, br, wi, bi, lam):
    xc = _conv(xa, cw, cb)
    r = sigmoid(bmm(xc, wr) + br)
    ig = sigmoid(bmm(xc, wi) + bi)
    log_a = -RG_C * r * softplus(-lam)
    row = lax.broadcasted_iota(jnp.int32, xc.shape, 0) + t0
    mult = jnp.where(row == 0, 1.0, jnp.sqrt(-expm1(2.0 * log_a)))
    return jnp.exp(log_a), mult * ig * xc


def st_a_cot(t0, lam, ha):
    return lam * shift_rows(ha, 1), lam


def _heads(x, n, w):
    return [x[:, h * w:(h + 1) * w] for h in range(n)]


def st_b(t0, q, k, v, pba, cw, alog, dtb):
    qc = silu(_conv(q, cw[:, 0:BW]))
    kc = silu(_conv(k, cw[:, BW:2 * BW]))
    vc = silu(_conv(v, cw[:, 2 * BW:3 * BW]))

    def l2n(x):
        return jnp.concatenate([s * lax.rsqrt(jnp.sum(s * s, axis=-1, keepdims=True) + EPS)
                                for s in _heads(x, B_HEADS, B_DK)], axis=-1)

    sg = sigmoid(pba)
    gg = -jnp.exp(alog) * softplus(pba + dtb)
    lane = lax.broadcasted_iota(jnp.int32, pba.shape, 1)

    def spread(x, first):
        return jnp.concatenate(
            [jnp.broadcast_to(jnp.sum(jnp.where(lane == first + h, x, 0.0), axis=-1, keepdims=True), (x.shape[0], B_DK))
             for h in range(B_HEADS)], axis=-1)

    return l2n(qc), l2n(kc), vc, spread(sg, 0), spread(gg, B_HEADS)


def st_m2(t0, ha, ga, o, z, y0, uc, bn4, cd, wglu, bglu):
    ya = ha * gelu(ga)
    yb = jnp.concatenate(
        [oh * lax.rsqrt(jnp.mean(oh * oh, axis=-1, keepdims=True) + EPS) * bh * silu(zh)
         for oh, bh, zh in zip(_heads(o, B_HEADS, B_DK), _heads(bn4, B_HEADS, B_DK), _heads(z, B_HEADS, B_DK))], axis=-1)
    yc0 = gelu(y0 + cd * uc)
    yc = yc0 * sigmoid(bmm(yc0, wglu) + bglu)
    return (jnp.concatenate([ya, yb, yc], axis=-1),)


def st_m2_cot(t0, d0, d1, d2):
    return (jnp.concatenate([d0, d1, d2], axis=-1),)


def st_m3(t0, g0, g1, g2, p0, p1, p2, bg):
    d = D_MODEL
    return (sigmoid(g0 + bg[:, 0:d]) * p0 + sigmoid(g1 + bg[:, d:2 * d]) * p1 + sigmoid(g2 + bg[:, 2 * d:3 * d]) * p2,)


def st_att(t0, q, kv):
    outs = []
    for h in range(X_HEADS):
        qh = q[:, h * X_HD:(h + 1) * X_HD]
        kh = kv[:, h * X_HD:(h + 1) * X_HD]
        vh = kv[:, D_MODEL + h * X_HD:D_MODEL + (h + 1) * X_HD]
        sc = bmm(qh, kh, False, True) * (X_HD ** -0.5)
        e = jnp.exp(sc - lax.stop_gradient(jnp.max(sc, axis=-1, keepdims=True)))
        outs.append(bmm(e / jnp.sum(e, axis=-1, keepdims=True), vh))
    return (jnp.concatenate(outs, axis=-1),)


def st_f2(t0, ug, uv, cwg, cwv, cbg, cbv):
    return (gelu(_conv(ug, cwg, cbg)) * _conv(uv, cwv, cbv),)


def st_s5step(t0, h, ar, ai):
    hp = shift_rows(h, 1)
    hr, hi = hp[:, :S5_CW], hp[:, S5_CW:]
    return (jnp.concatenate([ar * hr - ai * hi, ar * hi + ai * hr], axis=-1),)


SUB_ROWS = {st_norm: 32, st_b: 32, st_m3: 32, st_f2: 32, st_s5step: 32}


W_MAIN = 6 * BW
W_BA = W_MAIN + 2 * B_HEADS
W_UC = W_BA + BW
COL_GATES, COL_UC = W_MAIN, W_MAIN + 3 * D_MODEL


def split_w_in(wt):
    big = jnp.concatenate([wt[:, :W_MAIN], wt[:, W_UC:], wt[:, W_BA:W_UC]], axis=1)
    ba = jnp.pad(wt[:, W_MAIN:W_BA], ((0, 0), (0, 128 - 2 * B_HEADS), (0, 0)))
    return big, ba


def merge_w_in(big, ba):
    return jnp.concatenate([big[:, :W_MAIN], ba[:, :2 * B_HEADS], big[:, COL_UC:], big[:, COL_GATES:COL_UC]], axis=1)


def derive(r):
    L = r['a_w_r'].shape[0]
    eye_a = jnp.eye(A_HEADS, dtype=F32)
    eye_g = jnp.eye(C_GROUPS, dtype=F32)

    def blockdiag(w):
        return jnp.einsum('lhij,hg->lhigj', w, eye_a).reshape(L, BW, BW)

    def lanes(v, first):
        return jnp.pad(v, ((0, 0), (first, 128 - first - B_HEADS)))[:, None, :]

    lr, li = r['c_lam_re'], r['c_lam_im']
    dt = jnp.exp(r['c_log_dt'])[..., None]
    mag = jnp.exp(lr * dt)
    ar, ai = mag * jnp.cos(li * dt), mag * jnp.sin(li * dt)
    den = lr * lr + li * li
    fr = ((ar - 1.0) * lr + ai * li) / den
    fi = (ai * lr - (ar - 1.0) * li) / den
    br, bi = r['c_b_re'], r['c_b_im']
    bbr = fr[..., None] * br - fi[..., None] * bi
    bbi = fr[..., None] * bi + fi[..., None] * br
    ncol = C_CH // S5_CW

    def b_dense(bb):
        return jnp.einsum('lgpc,gh->lgchp', bb, eye_g).reshape(L, BW, ncol, S5_CW)

    bbig = jnp.concatenate([b_dense(bbr), b_dense(bbi)], axis=3).reshape(L, BW, 2 * C_CH)

    def c_dense(cc):
        return jnp.einsum('lgcp,gh->lgphc', cc, eye_g).reshape(L, ncol, S5_CW, BW)

    cbig = jnp.stack([c_dense(r['c_c_re']), -c_dense(r['c_c_im'])], axis=2).reshape(L, 2 * C_CH, BW)
    return dict(wr=blockdiag(r['a_w_r']), wi=blockdiag(r['a_w_i']),
                alog=lanes(r['b_a_log'], B_HEADS), dtb=lanes(r['b_dt_bias'], B_HEADS),
                bn4=jnp.tile(r['b_norm'], (1, B_HEADS))[:, None, :],
                ar=ar.reshape(L, 1, C_CH), ai=ai.reshape(L, 1, C_CH), bbig=bbig, cbig=cbig)


DERIVE_FROM = ['a_w_r', 'a_w_i', 'b_a_log', 'b_dt_bias', 'b_norm', 'c_lam_re', 'c_lam_im', 'c_log_dt',
               'c_b_re', 'c_b_im', 'c_c_re', 'c_c_im']


def _row(v):
    return v.reshape(1, -1)


def layer_params(w, big, ba, dv, l):
    return dict(
        mix_norm=_row(w['mix_norm'][l]), w_big=big[l], w_ba=ba[l], b_gate=_row(w['b_gate'][l]),
        a_conv_w=w['a_conv_w'][l], a_conv_b=_row(w['a_conv_b'][l]), wr=dv['wr'][l], b_r=_row(w['a_b_r'][l]),
        wi=dv['wi'][l], b_i=_row(w['a_b_i'][l]), lam=_row(w['a_lam'][l]),
        b_conv_w=w['b_conv_w'][l], alog=dv['alog'][l], dtb=dv['dtb'][l], bn4=dv['bn4'][l],
        ar=dv['ar'][l], ai=dv['ai'][l], bbig=dv['bbig'][l], cbig=dv['cbig'][l],
        c_d=_row(w['c_d'][l]), wglu=w['c_glu_w'][l].astype(F32), bglu=_row(w['c_glu_b'][l]),
        w_brT=w['w_brT'][l], w_out=w['w_out'][l],
        xa_norm=_row(w['xa_norm'][l]), mem_norm=_row(w['mem_norm'][l]),
        w_q=w['xa_w_q'][l], w_kvT=w['w_kvT'][l], w_o=w['xa_w_o'][l],
        ffn_norm=_row(w['ffn_norm'][l]), w_upT=w['w_upT'][l], ffn_conv_w=w['ffn_conv_w'][l],
        ffn_conv_b=_row(w['ffn_conv_b'][l]), w_down=w['ffn_w_down'][l])


def _a_pars(p):
    return [Par(p['a_conv_w']), Par(p['a_conv_b']), Par(p['wr']), Par(p['b_r']), Par(p['wi']), Par(p['b_i']), Par(p['lam'])]


def _f2_pars(p):
    d = D_MODEL
    return [Par(p['ffn_conv_w'], 'col', 0, d), Par(p['ffn_conv_w'], 'col', 3, d),
            Par(p['ffn_conv_b'], 'col', 0, d), Par(p['ffn_conv_b'], 'col', 3, d)]


def layer_fwd(x, mem, p, seq, mseq, l):
    d = D_MODEL
    nb = x.shape[0] // seq
    kw = dict(seq=seq, tb=TB)
    n = lambda s: f"{s}_l{l}"
    h, = stage_fwd(st_norm, [Tok(x, d)], [Par(p['mix_norm'])], [(d, MXU_DTYPE)], name=n("norm_mix"), **kw)
    P = mm(h, p['w_big'], tb=True, name=n("mm_in"))
    Pba = mm(h, p['w_ba'], tb=True, name=n("mm_in_ba"))
    a, bb = stage_fwd(st_a, [Tok(P, BW, 0, True)], _a_pars(p), [(BW, F32)] * 2, name=n("rglru_pre"), **kw)
    ha = scan_real(a, bb, seq=seq, reverse=False, name=n("rglru_scan"))
    qn, kn, vv, bB, gB = stage_fwd(
        st_b, [Tok(P, BW, 2, True), Tok(P, BW, 3, True), Tok(P, BW, 4, True), Tok(Pba, 128)],
        [Par(p['b_conv_w']), Par(p['alog']), Par(p['dtb'])], [(BW, F32)] * 5, name=n("delta_pre"), **kw)
    r3 = lambda t: t.reshape(nb, seq, BW)
    o3, states, tinvs = delta_fwd(r3(qn), r3(kn), r3(vv), r3(bB), r3(gB), name=n("delta"))
    o = o3.reshape(-1, BW)
    bu = mm(P, p['bbig'], a_cols=(COL_UC, BW), name=n("mm_s5_in"))
    hs = scan_cplx(p['ar'], p['ai'], bu, seq=seq, reverse=False, name=n("s5_scan"))
    y0 = mm(hs, p['cbig'], name=n("mm_s5_out"))
    m2_toks = [Tok(ha, BW), Tok(P, BW, 1), Tok(o, BW), Tok(P, BW, 5), Tok(y0, BW), Tok(P, BW, COL_UC // BW)]
    m2_pars = [Par(p['bn4']), Par(p['c_d']), Par(p['wglu']), Par(p['bglu'])]
    Y3, = stage_fwd(st_m2, m2_toks, m2_pars, [(3 * BW, MXU_DTYPE)], name=n("branches"), **kw)
    proj = [mm(Y3, p['w_brT'][k], tb=True, a_cols=(k * BW, BW), name=n(f"mm_branch{k}")) for k in range(3)]
    g0 = COL_GATES // d
    m3_toks = [Tok(P, d, g0), Tok(P, d, g0 + 1), Tok(P, d, g0 + 2)] + [Tok(t, d) for t in proj]
    mixed, = stage_fwd(st_m3, m3_toks, [Par(p['b_gate'])], [(d, MXU_DTYPE)], name=n("gate_mix"), **kw)
    x1 = mm(mixed, p['w_out'], residual=x, name=n("mm_out"))
    hx, = stage_fwd(st_norm, [Tok(x1, d)], [Par(p['xa_norm'])], [(d, MXU_DTYPE)], name=n("norm_xa"), **kw)
    mn, = stage_fwd(st_norm, [Tok(mem, d)], [Par(p['mem_norm'])], [(d, MXU_DTYPE)], seq=mseq, tb=TB, name=n("norm_mem"))
    qx = mm(hx, p['w_q'], name=n("mm_q"))
    kv = mm(mn, p['w_kvT'], tb=True, name=n("mm_kv"))
    kv3 = kv.reshape(nb, mseq, 2 * d)
    ox, = stage_fwd(st_att, [Tok(qx, d)], [Par(kv3, 'batch')], [(d, MXU_DTYPE)], name=n("attention"), **kw)
    x2 = mm(ox, p['w_o'], residual=x1, name=n("mm_o"))
    hf, = stage_fwd(st_norm, [Tok(x2, d)], [Par(p['ffn_norm'])], [(d, MXU_DTYPE)], name=n("norm_ffn"), **kw)
    U = mm(hf, p['w_upT'], tb=True, name=n("mm_up"))
    act, = stage_fwd(st_f2, [Tok(U, d, 0, True), Tok(U, d, 3, True)], _f2_pars(p), [(d, MXU_DTYPE)], ncol=3,
                     name=n("ffn_act"), **kw)
    x3 = mm(act, p['w_down'], residual=x2, name=n("mm_down"))
    sv = dict(x=x, h=h, P=P, Pba=Pba, a=a, ha=ha, qn=qn, kn=kn, vv=vv, bB=bB, gB=gB, states=states, tinvs=tinvs, o=o, hs=hs, y0=y0,
              Y3=Y3, proj=proj, mixed=mixed, x1=x1, hx=hx, mn=mn, qx=qx, kv3=kv3, ox=ox, x2=x2, hf=hf, U=U, act=act)
    return x3, sv


def layer_bwd(dx3, mem, p, sv, seq, mseq, l):
    d = D_MODEL
    nb = dx3.shape[0] // seq
    kw = dict(seq=seq, tb=TB)
    n = lambda s: f"{s}_l{l}"
    g = {}
    P, Pba = sv['P'], sv['Pba']
    dact = mm(dx3, p['w_down'], tb=True, name=n("bmm_down_x"))
    g['w_down'] = mm(sv['act'], dx3, ta=True, name=n("bmm_down_w"))
    (dUg, dUv), gp = stage_bwd(st_f2, [Tok(sv['U'], d, 0, True, MXU_DTYPE), Tok(sv['U'], d, 3, True, MXU_DTYPE)], _f2_pars(p),
                               [Tok(dact, d)], ncol=3, name=n("b_ffn_act"), **kw)
    g['ffn_conv_w'] = jnp.concatenate([gp[0], gp[1]], axis=1)
    g['ffn_conv_b'] = jnp.concatenate([gp[2], gp[3]], axis=1)
    dU = jnp.concatenate([dUg, dUv], axis=1)
    dhf = mm(dU, p['w_upT'], name=n("bmm_up_x"))
    g['w_upT'] = mm(dU, sv['hf'], ta=True, name=n("bmm_up_w"))
    (dx2,), (g['ffn_norm'],) = stage_bwd(st_norm, [Tok(sv['x2'], d, grad=F32, add=dx3)], [Par(p['ffn_norm'])],
                                         [Tok(dhf, d)], name=n("b_norm_ffn"), **kw)
    dox = mm(dx2, p['w_o'], tb=True, name=n("bmm_o_x"))
    g['w_o'] = mm(sv['ox'], dx2, ta=True, name=n("bmm_o_w"))
    (dqx,), (dkv3,) = stage_bwd(st_att, [Tok(sv['qx'], d, grad=MXU_DTYPE)], [Par(sv['kv3'], 'batch')], [Tok(dox, d)],
                                name=n("b_attention"), **kw)
    dkv = dkv3.reshape(-1, 2 * d)
    dhx = mm(dqx, p['w_q'], tb=True, name=n("bmm_q_x"))
    g['w_q'] = mm(sv['hx'], dqx, ta=True, name=n("bmm_q_w"))
    dmn = mm(dkv, p['w_kvT'], name=n("bmm_kv_x"))
    g['w_kvT'] = mm(dkv, sv['mn'], ta=True, name=n("bmm_kv_w"))
    _, (g['mem_norm'],) = stage_bwd(st_norm, [Tok(mem, d)], [Par(p['mem_norm'])], [Tok(dmn, d)], seq=mseq, tb=TB,
                                    name=n("b_norm_mem"))
    (dx1,), (g['xa_norm'],) = stage_bwd(st_norm, [Tok(sv['x1'], d, grad=F32, add=dx2)], [Par(p['xa_norm'])],
                                        [Tok(dhx, d)], name=n("b_norm_xa"), **kw)
    dmixed = mm(dx1, p['w_out'], tb=True, name=n("bmm_out_x"))
    g['w_out'] = mm(sv['mixed'], dx1, ta=True, name=n("bmm_out_w"))
    g0 = COL_GATES // d
    m3_toks = [Tok(P, d, g0 + k, grad=MXU_DTYPE) for k in range(3)] + [Tok(t, d, grad=MXU_DTYPE) for t in sv['proj']]
    dm3, (g['b_gate'],) = stage_bwd(st_m3, m3_toks, [Par(p['b_gate'])], [Tok(dmixed, d)], name=n("b_gate_mix"), **kw)
    dgates, dproj = dm3[:3], dm3[3:]
    dY = [mm(dproj[k], p['w_brT'][k], name=n(f"bmm_branch{k}_x")) for k in range(3)]
    g['w_brT'] = jnp.stack([mm(dproj[k], sv['Y3'], ta=True, b_cols=(k * BW, BW), name=n(f"bmm_branch{k}_w"))
                            for k in range(3)])
    m2_toks = [Tok(sv['ha'], BW, grad=F32), Tok(P, BW, 1, grad=MXU_DTYPE), Tok(sv['o'], BW, grad=F32), Tok(P, BW, 5, grad=MXU_DTYPE),
               Tok(sv['y0'], BW, grad=MXU_DTYPE), Tok(P, BW, COL_UC // BW, grad=F32)]
    m2_pars = [Par(p['bn4']), Par(p['c_d']), Par(p['wglu']), Par(p['bglu'])]
    (dha, dga, do, dz, dy0, duc0), (g['bn4'], g['c_d'], g['wglu'], g['bglu']) = stage_bwd(
        st_m2, m2_toks, m2_pars, [Tok(t, BW) for t in dY], cot_fn=st_m2_cot, name=n("b_branches"), **kw)
    dhs = mm(dy0, p['cbig'], tb=True, name=n("bmm_s5_out_x"))
    g['cbig'] = mm(sv['hs'], dy0, ta=True, name=n("bmm_s5_out_w"))
    lam_s = scan_cplx(p['ar'], -p['ai'], dhs, seq=seq, reverse=True, name=n("b_s5_scan"))
    _, (g['ar'], g['ai']) = stage_bwd(st_s5step, [Tok(sv['hs'], 2 * S5_CW, 0, True)],
                                      [Par(p['ar'], 'col', 0, S5_CW), Par(p['ai'], 'col', 0, S5_CW)],
                                      [Tok(lam_s, 2 * S5_CW)], ncol=C_CH // S5_CW, name=n("b_s5_decay"), **kw)
    duc = mm(lam_s, p['bbig'], tb=True, residual=duc0, out_dtype=MXU_DTYPE, name=n("bmm_s5_in_x"))
    g['bbig'] = mm(P, lam_s, ta=True, a_cols=(COL_UC, BW), name=n("bmm_s5_in_w"))
    lam_a = scan_real(sv['a'], dha, seq=seq, reverse=True, name=n("b_rglru_scan"))
    (dxa,), ga = stage_bwd(st_a, [Tok(P, BW, 0, True, MXU_DTYPE)], _a_pars(p), [Tok(lam_a, BW), Tok(sv['ha'], BW, 0, True)],
                           cot_fn=st_a_cot, name=n("b_rglru_pre"), **kw)
    g['a_conv_w'], g['a_conv_b'], g['wr'], g['b_r'], g['wi'], g['b_i'], g['lam'] = ga
    r3 = lambda t: t.reshape(nb, seq, BW)
    dd = delta_bwd(r3(sv['qn']), r3(sv['kn']), r3(sv['vv']), r3(sv['bB']), r3(sv['gB']), sv['states'], sv['tinvs'],
                   r3(do), name=n("b_delta"))
    (dq, dk, dv, dpba), (g['b_conv_w'], g['alog'], g['dtb']) = stage_bwd(
        st_b, [Tok(P, BW, 2, True, MXU_DTYPE), Tok(P, BW, 3, True, MXU_DTYPE), Tok(P, BW, 4, True, MXU_DTYPE), Tok(Pba, 128, grad=MXU_DTYPE)],
        [Par(p['b_conv_w']), Par(p['alog']), Par(p['dtb'])], [Tok(t.reshape(-1, BW), BW) for t in dd],
        name=n("b_delta_pre"), **kw)
    dP = jnp.concatenate([dxa, dga, dq, dk, dv, dz] + list(dgates) + [duc], axis=1)
    dh0 = mm(dpba, p['w_ba'], name=n("bmm_in_ba_x"))
    dh = mm(dP, p['w_big'], residual=dh0, name=n("bmm_in_x"))
    g['w_big'] = mm(dP, sv['h'], ta=True, name=n("bmm_in_w"))
    g['w_ba'] = mm(dpba, sv['h'], ta=True, name=n("bmm_in_ba_w"))
    (dx,), (g['mix_norm'],) = stage_bwd(st_norm, [Tok(sv['x'], d, grad=F32, add=dx1)], [Par(p['mix_norm'])],
                                        [Tok(dh, d)], name=n("b_norm_mix"), **kw)
    return dx, g


def local_step(x3d, mem3d, tgt3d, w, final_norm):
    nb, seq, d = x3d.shape
    mseq = mem3d.shape[1]
    x = x3d.reshape(nb * seq, d)
    mem = mem3d.reshape(nb * mseq, d)
    L = w['mix_norm'].shape[0]
    big, ba = split_w_in(w['w_inT'])
    dv, dv_vjp = jax.vjp(derive, {k: w[k] for k in DERIVE_FROM})
    ps, svs = [], []
    for l in range(L):
        p = layer_params(w, big, ba, dv, l)
        x, sv = layer_fwd(x, mem, p, seq, mseq, l)
        ps.append(p)
        svs.append(sv)
    loss, dx, g_final = loss_head(x, tgt3d.reshape(nb * seq, d), _row(final_norm), tb=TB, name="loss_head")
    gs = [None] * L
    for l in reversed(range(L)):
        dx, gs[l] = layer_bwd(dx, mem, ps[l], svs[l], seq, mseq, l)
    st = lambda k: jnp.stack([gs[l][k] for l in range(L)])
    vec = lambda k: st(k).reshape(L, -1)
    gd = dv_vjp({k: st(k) for k in ('wr', 'wi', 'alog', 'dtb', 'bn4', 'ar', 'ai', 'bbig', 'cbig')})[0]
    out = dict(gd)
    out.update(
        mix_norm=vec('mix_norm'), w_inT=merge_w_in(st('w_big'), st('w_ba')), b_gate=vec('b_gate'),
        a_conv_w=st('a_conv_w'), a_conv_b=vec('a_conv_b'), a_b_r=vec('b_r'), a_b_i=vec('b_i'), a_lam=vec('lam'),
        b_conv_w=st('b_conv_w'), c_d=vec('c_d'), c_glu_w=st('wglu'), c_glu_b=vec('bglu'),
        w_brT=st('w_brT'), w_out=st('w_out'), xa_norm=vec('xa_norm'), mem_norm=vec('mem_norm'),
        xa_w_q=st('w_q'), w_kvT=st('w_kvT'), xa_w_o=st('w_o'), ffn_norm=vec('ffn_norm'), w_upT=st('w_upT'),
        ffn_conv_w=st('ffn_conv_w'), ffn_conv_b=vec('ffn_conv_b'), ffn_w_down=st('w_down'),
        final_norm=g_final.reshape(-1))
    return loss, dx.reshape(nb, seq, d), out


LANES = 1024
ROW_PAD = 256


def _small_rows(shape):
    return -(-int(np.prod(shape)) // (LANES * SUBLANES)) * SUBLANES


def _pack(vecs, dtype):
    segs = []
    for v in vecs:
        rows = _small_rows(v.shape)
        flat = v.reshape(-1).astype(dtype)
        segs.append(jnp.pad(flat, (0, rows * LANES - flat.shape[0])).reshape(rows, LANES))
    total = sum(s.shape[0] for s in segs)
    tail = -total % ROW_PAD
    if tail:
        segs.append(jnp.zeros((tail, LANES), dtype))
    return jnp.concatenate(segs, axis=0)


BIG_LAYOUT = {'w_in': 'w_inT', 'xa_w_kv': 'w_kvT', 'ffn_w_up': 'w_upT', 'w_branch': 'w_brT'}
SEG_ALIGN = 16


def _to_layout(n, a):
    return jnp.swapaxes(a, -1, -2) if n in BIG_LAYOUT else a


def _seg_rows(shape):
    rows = int(np.prod(shape)) // LANES
    return rows, -(-rows // SEG_ALIGN) * SEG_ALIGN


def _pack_segments(mats, dtype):
    lead = mats[0].ndim - 2
    segs = []
    for m in mats:
        pad = -m.shape[-2] % SEG_ALIGN
        segs.append(jnp.pad(m.astype(dtype), [(0, 0)] * lead + [(0, pad), (0, 0)]))
    total = sum(s.shape[-2] for s in segs)
    tail = -total % ROW_PAD
    if tail:
        segs.append(jnp.zeros(segs[0].shape[:-2] + (tail, LANES), dtype))
    return jnp.concatenate(segs, axis=-2)


def _unpack_segments(buf, shapes):
    lead = buf.shape[:-2]
    out, off = [], 0
    for shp in shapes:
        rows, padded = _seg_rows(shp)
        out.append(buf[..., off:off + rows, :].reshape(lead + tuple(shp)))
        off += padded
    return out


def _unpack(flat, shapes):
    lead = flat.shape[:-2]
    out, off = [], 0
    for shp in shapes:
        cnt, rows = int(np.prod(shp)), _small_rows(shp)
        seg = flat[..., off:off + rows, :].reshape(lead + (-1,))
        out.append(seg[..., :cnt].reshape(lead + tuple(shp)))
        off += rows
    return out


def _join_shards(stacked, axis):
    t = jnp.moveaxis(stacked, 0, axis)
    shp = list(t.shape)
    return t.reshape(shp[:axis] + [shp[axis] * shp[axis + 1]] + shp[axis + 2:])


def _cut_shards(full, axis):
    shp = list(full.shape)
    t = full.reshape(shp[:axis] + [N_DEV, shp[axis] // N_DEV] + shp[axis + 1:])
    return jnp.moveaxis(t, axis, 0).reshape(N_DEV, -1)


def _as2d(a):
    return a.reshape(-1, a.shape[-1])


def kernel(x, mem, mix_norm, w_in, b_gate, a_conv_w, a_conv_b, a_w_r, a_b_r, a_w_i, a_b_i, a_lam, b_conv_w, b_a_log, b_dt_bias, b_norm, c_lam_re, c_lam_im, c_log_dt, c_b_re, c_b_im, c_c_re, c_c_im, c_d, c_glu_w, c_glu_b, w_branch, w_out, xa_norm, mem_norm, xa_w_q, xa_w_kv, xa_w_o, ffn_norm, ffn_w_up, ffn_conv_w, ffn_conv_b, ffn_w_down, final_norm, loss_target, m_mix_norm, m_w_in, m_b_gate, m_a_conv_w, m_a_conv_b, m_a_w_r, m_a_b_r, m_a_w_i, m_a_b_i, m_a_lam, m_b_conv_w, m_b_a_log, m_b_dt_bias, m_b_norm, m_c_lam_re, m_c_lam_im, m_c_log_dt, m_c_b_re, m_c_b_im, m_c_c_re, m_c_c_im, m_c_d, m_c_glu_w, m_c_glu_b, m_w_branch, m_w_out, m_xa_norm, m_mem_norm, m_xa_w_q, m_xa_w_kv, m_xa_w_o, m_ffn_norm, m_ffn_w_up, m_ffn_conv_w, m_ffn_conv_b, m_ffn_w_down, m_final_norm, v_mix_norm, v_w_in, v_b_gate, v_a_conv_w, v_a_conv_b, v_a_w_r, v_a_b_r, v_a_w_i, v_a_b_i, v_a_lam, v_b_conv_w, v_b_a_log, v_b_dt_bias, v_b_norm, v_c_lam_re, v_c_lam_im, v_c_log_dt, v_c_b_re, v_c_b_im, v_c_c_re, v_c_c_im, v_c_d, v_c_glu_w, v_c_glu_b, v_w_branch, v_w_out, v_xa_norm, v_mem_norm, v_xa_w_q, v_xa_w_kv, v_xa_w_o, v_ffn_norm, v_ffn_w_up, v_ffn_conv_w, v_ffn_conv_b, v_ffn_w_down, v_final_norm):
    args = (x, mem, mix_norm, w_in, b_gate, a_conv_w, a_conv_b, a_w_r, a_b_r, a_w_i, a_b_i, a_lam, b_conv_w, b_a_log, b_dt_bias, b_norm, c_lam_re, c_lam_im, c_log_dt, c_b_re, c_b_im, c_c_re, c_c_im, c_d, c_glu_w, c_glu_b, w_branch, w_out, xa_norm, mem_norm, xa_w_q, xa_w_kv, xa_w_o, ffn_norm, ffn_w_up, ffn_conv_w, ffn_conv_b, ffn_w_down, final_norm, loss_target, m_mix_norm, m_w_in, m_b_gate, m_a_conv_w, m_a_conv_b, m_a_w_r, m_a_b_r, m_a_w_i, m_a_b_i, m_a_lam, m_b_conv_w, m_b_a_log, m_b_dt_bias, m_b_norm, m_c_lam_re, m_c_lam_im, m_c_log_dt, m_c_b_re, m_c_b_im, m_c_c_re, m_c_c_im, m_c_d, m_c_glu_w, m_c_glu_b, m_w_branch, m_w_out, m_xa_norm, m_mem_norm, m_xa_w_q, m_xa_w_kv, m_xa_w_o, m_ffn_norm, m_ffn_w_up, m_ffn_conv_w, m_ffn_conv_b, m_ffn_w_down, m_final_norm, v_mix_norm, v_w_in, v_b_gate, v_a_conv_w, v_a_conv_b, v_a_w_r, v_a_b_r, v_a_w_i, v_a_b_i, v_a_lam, v_b_conv_w, v_b_a_log, v_b_dt_bias, v_b_norm, v_c_lam_re, v_c_lam_im, v_c_log_dt, v_c_b_re, v_c_b_im, v_c_c_re, v_c_c_im, v_c_d, v_c_glu_w, v_c_glu_b, v_w_branch, v_w_out, v_xa_norm, v_mem_norm, v_xa_w_q, v_xa_w_kv, v_xa_w_o, v_ffn_norm, v_ffn_w_up, v_ffn_conv_w, v_ffn_conv_b, v_ffn_w_down, v_final_norm)
    nw = len(WEIGHTS)
    x, mem = args[0], args[1]
    w_loc = dict(zip(WEIGHTS, args[2:2 + nw]))
    tgt = args[2 + nw]
    m_loc = dict(zip(WEIGHTS, args[3 + nw:3 + 2 * nw]))
    v_loc = dict(zip(WEIGHTS, args[3 + 2 * nw:3 + 3 * nw]))
    me = 4 * lax.axis_index("x") + 2 * lax.axis_index("y") + lax.axis_index("c")

    lay = {n: _to_layout(n, w_loc[n]) for n in BIG}
    lay_shapes = [lay[n].shape for n in BIG]
    gathered = _gather(_pack_segments([lay[n].reshape(-1, LANES) for n in BIG], BF), name="gather_matmul_weights")
    w = {n: a for n, a in w_loc.items() if n not in BIG}
    for n, st in zip(BIG, _unpack_segments(gathered, lay_shapes)):
        t = jnp.moveaxis(st, 0, -3)
        w[BIG_LAYOUT.get(n, n)] = t.reshape(t.shape[:-3] + (N_DEV * t.shape[-2], t.shape[-1]))
    ss_shapes = [w_loc[n].shape for n in SMALL_SHARDED]
    gathered_s = _gather(_pack([w_loc[n] for n in SMALL_SHARDED], F32), name="gather_conv_weights")
    for n, st in zip(SMALL_SHARDED, _unpack(gathered_s, ss_shapes)):
        w[n] = _join_shards(st, SHARD_AXIS[n])

    final_norm = w.pop('final_norm')
    loss, grad_x, g = local_step(x, mem, tgt, w, final_norm)

    def cut(full):
        t = full.reshape(full.shape[:-2] + (N_DEV, full.shape[-2] // N_DEV, full.shape[-1]))
        return jnp.moveaxis(t, -3, 0).reshape(N_DEV, -1, LANES)

    send = _pack_segments([cut(g[BIG_LAYOUT.get(n, n)]) for n in BIG], BF)
    recv = _exchange(send, scatter=True, name="scatter_matmul_grads")
    g_lay = _unpack_segments(sum_slabs(recv, name="sum_matmul_grads"), lay_shapes)
    g_big = {n: _to_layout(n, a) for n, a in zip(BIG, g_lay)}
    small_full_shapes = [g[n].shape for n in SMALL]
    packed = _pack([g[n] for n in SMALL] + [loss[0, :1]], F32)
    everyones = _gather(packed, name="gather_small_grads")
    summed = _unpack(sum_slabs(everyones, name="sum_small_grads"), small_full_shapes + [(1,)])
    loss_total = summed[-1].reshape(())
    g_small = {}
    for n, full in zip(SMALL, summed[:-1]):
        full = full.reshape(small_full_shapes[SMALL.index(n)])
        if n in SMALL_SHARDED:
            ax = SHARD_AXIS[n]
            loc = w_loc[n].shape[ax]
            full = lax.dynamic_slice_in_dim(full, me * loc, loc, axis=ax)
        g_small[n] = full.reshape(w_loc[n].shape)

    grads, delta, new_m, new_v = {}, {}, {}, {}
    for n in BIG:
        shp = w_loc[n].shape
        grads[n] = g_big[n]
        dl, nm, nv = adamw(_as2d(w_loc[n]), _as2d(g_big[n]), _as2d(m_loc[n]), _as2d(v_loc[n]), name=f"adamw_{n}")
        delta[n], new_m[n], new_v[n] = dl.reshape(shp), nm.reshape(shp), nv.reshape(shp)
    small_shapes = [w_loc[n].shape for n in SMALL]
    flat = [_pack([d[n] for n in SMALL], F32) for d in (w_loc, g_small, m_loc, v_loc)]
    res = adamw(*flat, name="adamw_small")
    for d, r in zip((delta, new_m, new_v), res):
        for n, a in zip(SMALL, _unpack(r, small_shapes)):
            d[n] = a
    grads.update(g_small)
    return (loss_total, grad_x, *[grads[n] for n in WEIGHTS], *[delta[n] for n in WEIGHTS],
            *[new_m[n] for n in WEIGHTS], *[new_v[n] for n in WEIGHTS])
```

```python
import functools
from typing import Any, NamedTuple

import jax
import jax.numpy as jnp
import numpy as np
from jax import lax
from jax.experimental import pallas as pl
from jax.experimental.pallas import tpu as pltpu

F32 = jnp.float32
BF = jnp.bfloat16
MXU_DTYPE = BF

EPS = 1e-6
RG_C = 8.0
N_DEV = 8
DEPTH = 4
D_MODEL = 1024
BW = 512
A_HEADS, A_HD = 8, 64
B_HEADS, B_DK = 4, 128
B_CHUNK = 64
C_GROUPS, C_GROUP, C_STATE = 32, 16, 64
C_CH = C_GROUPS * C_STATE
S5_CW = 512
X_HEADS, X_HD = 4, 256
D_FF = 3 * D_MODEL
ADAM_LR, ADAM_B1, ADAM_B2, ADAM_EPS, ADAM_WD, ADAM_STEP = 0.001, 0.9, 0.999, 1e-08, 0.01, 10

SUBLANES = 8
VMEM_LIMIT = 56 * 1024 * 1024

WEIGHTS = ['mix_norm', 'w_in', 'b_gate', 'a_conv_w', 'a_conv_b', 'a_w_r', 'a_b_r', 'a_w_i', 'a_b_i', 'a_lam',
           'b_conv_w', 'b_a_log', 'b_dt_bias', 'b_norm', 'c_lam_re', 'c_lam_im', 'c_log_dt', 'c_b_re', 'c_b_im',
           'c_c_re', 'c_c_im', 'c_d', 'c_glu_w', 'c_glu_b', 'w_branch', 'w_out', 'xa_norm', 'mem_norm', 'xa_w_q',
           'xa_w_kv', 'xa_w_o', 'ffn_norm', 'ffn_w_up', 'ffn_conv_w', 'ffn_conv_b', 'ffn_w_down', 'final_norm']
SHARD_AXIS = {'w_in': 2, 'a_conv_w': 2, 'b_conv_w': 2, 'c_glu_w': 1, 'w_branch': 3, 'w_out': 1, 'xa_w_q': 1,
              'xa_w_kv': 2, 'xa_w_o': 1, 'ffn_w_up': 2, 'ffn_conv_w': 2, 'ffn_w_down': 1}
BIG = ['w_in', 'c_glu_w', 'w_branch', 'w_out', 'xa_w_q', 'xa_w_kv', 'xa_w_o', 'ffn_w_up', 'ffn_w_down']
SMALL_SHARDED = ['a_conv_w', 'b_conv_w', 'ffn_conv_w']
SMALL = [n for n in WEIGHTS if n not in BIG]


def _dot(x, y, tx, ty):
    cx = 0 if tx else 1
    cy = 1 if ty else 0
    return lax.dot_general(x.astype(MXU_DTYPE), y.astype(MXU_DTYPE), (((cx,), (cy,)), ((), ())),
                           preferred_element_type=F32)


@functools.partial(jax.custom_vjp, nondiff_argnums=(2, 3))
def bmm(a, b, ta=False, tb=False):
    return _dot(a, b, ta, tb)


def _bmm_fwd(a, b, ta, tb):
    return _dot(a, b, ta, tb), (a, b)


def _bmm_bwd(ta, tb, res, g):
    a, b = res
    da = _dot(b, g, tb, True) if ta else _dot(g, b, False, not tb)
    db = _dot(g, a, True, ta) if tb else _dot(a, g, not ta, False)
    return da.astype(a.dtype), db.astype(b.dtype)


bmm.defvjp(_bmm_fwd, _bmm_bwd)


def _dotx(x, y, tx, ty):
    cx = 0 if tx else 1
    cy = 1 if ty else 0

    def d(p, q):
        return lax.dot_general(p, q, (((cx,), (cy,)), ((), ())), preferred_element_type=F32)

    xh, yh = x.astype(BF), y.astype(BF)
    xl, yl = (x - xh.astype(F32)).astype(BF), (y - yh.astype(F32)).astype(BF)
    return d(xh, yh) + (d(xh, yl) + d(xl, yh))


@functools.partial(jax.custom_vjp, nondiff_argnums=(2, 3))
def xmm(a, b, ta=False, tb=False):
    return _dotx(a, b, ta, tb)


def _xmm_fwd(a, b, ta, tb):
    return _dotx(a, b, ta, tb), (a, b)


def _xmm_bwd(ta, tb, res, g):
    a, b = res
    da = _dotx(b, g, tb, True) if ta else _dotx(g, b, False, not tb)
    db = _dotx(g, a, True, ta) if tb else _dotx(a, g, not ta, False)
    return da, db


xmm.defvjp(_xmm_fwd, _xmm_bwd)


@functools.partial(jax.custom_vjp, nondiff_argnums=(1,))
def roll_rows(x, s):
    return pltpu.roll(x, s, 0)


def _roll_rows_fwd(x, s):
    return pltpu.roll(x, s, 0), None


def _roll_rows_bwd(s, _, g):
    return (pltpu.roll(g, (g.shape[0] - s) % g.shape[0], 0),)


roll_rows.defvjp(_roll_rows_fwd, _roll_rows_bwd)


def shift_rows(cur_tail, s):
    cur, tail = cur_tail
    if s == 0:
        return cur
    rolled = roll_rows(cur, s)
    row = lax.broadcasted_iota(jnp.int32, tail.shape, 0)
    top = jnp.where(row < s, roll_rows(tail, s), rolled[:SUBLANES])
    if cur.shape[0] == SUBLANES:
        return top
    return jnp.concatenate([top, rolled[SUBLANES:]], axis=0)


def softplus(x):
    return jnp.maximum(x, 0.0) + jnp.log(1.0 + jnp.exp(-jnp.abs(x)))


def expm1(x):
    series = x * (1.0 + x * (0.5 + x * (1.0 / 6.0 + x * (1.0 / 24.0 + x * (1.0 / 120.0)))))
    return jnp.where(jnp.abs(x) < 0.05, series, jnp.exp(x) - 1.0)


def sigmoid(x):
    return 1.0 / (1.0 + jnp.exp(-x))


def silu(x):
    return x * sigmoid(x)


def gelu(x):
    return 0.5 * x * (1.0 + jnp.tanh(0.7978845608028654 * (x + 0.044715 * (x * x * x))))


def rms(x, g):
    var = jnp.mean(x * x, axis=-1, keepdims=True)
    return x * lax.rsqrt(var + EPS) * g


def cumsum_rows(x):
    n = x.shape[0]
    row = lax.broadcasted_iota(jnp.int32, x.shape, 0)
    s = 1
    while s < n:
        x = x + jnp.where(row >= s, roll_rows(x, s), 0.0)
        s *= 2
    return x


def _pick(n, prefs):
    for p in prefs:
        if n % p == 0:
            return p
    return n


def mm(a, b, *, ta=False, tb=False, a_cols=None, b_cols=None, residual=None, out_dtype=F32, name):
    a0, aw = a_cols if a_cols is not None else (0, a.shape[1])
    b0, bw = b_cols if b_cols is not None else (0, b.shape[1])
    if ta:
        K, M = a.shape[0], aw
    else:
        M, K = a.shape[0], aw
    if tb:
        N, Kb = b.shape[0], bw
    else:
        Kb, N = b.shape[0], bw
    assert K == Kb, (name, a.shape, b.shape, ta, tb)
    tm = _pick(M, (1024, 512, 256, 128))
    tk = _pick(K, (2048, 1024, 3328, 512, 256, 128))
    tn = _pick(N, (1024, 512, 256, 128)) if tk <= 1024 else _pick(N, (512, 256, 128))
    nk = K // tk

    def off(c0, t):
        assert c0 % t == 0, (name, c0, t)
        return c0 // t

    if ta:
        a_spec = pl.BlockSpec((tk, tm), lambda i, j, k, o=off(a0, tm): (k, i + o))
    else:
        a_spec = pl.BlockSpec((tm, tk), lambda i, j, k, o=off(a0, tk): (i, k + o))
    if tb:
        b_spec = pl.BlockSpec((tn, tk), lambda i, j, k, o=off(b0, tk): (j, k + o))
    else:
        b_spec = pl.BlockSpec((tk, tn), lambda i, j, k, o=off(b0, tn): (k, j + o))
    o_spec = pl.BlockSpec((tm, tn), lambda i, j, k: (i, j))
    in_specs = [a_spec, b_spec]
    args = [a, b]
    if residual is not None:
        in_specs.append(o_spec)
        args.append(residual)

    def body(*refs):
        a_ref, b_ref = refs[0], refs[1]
        r_ref = refs[2] if residual is not None else None
        o_ref = refs[3] if residual is not None else refs[2]
        part = _dot(a_ref[...], b_ref[...], ta, tb)

        def finish(acc):
            if r_ref is not None:
                acc = acc + r_ref[...].astype(F32)
            o_ref[...] = acc.astype(out_dtype)

        if nk == 1:
            finish(part)
        else:
            acc_ref = refs[-1]
            k = pl.program_id(2)

            @pl.when(k == 0)
            def _():
                acc_ref[...] = part

            @pl.when(k > 0)
            def _():
                acc_ref[...] += part

            @pl.when(k == nk - 1)
            def _():
                finish(acc_ref[...])

    return pl.pallas_call(
        body, name=name, grid=(M // tm, N // tn, nk),
        in_specs=in_specs, out_specs=o_spec,
        out_shape=jax.ShapeDtypeStruct((M, N), out_dtype),
        scratch_shapes=[pltpu.VMEM((tm, tn), F32)] if nk > 1 else [],
        compiler_params=pltpu.CompilerParams(dimension_semantics=("parallel", "parallel", "arbitrary"),
                                             vmem_limit_bytes=VMEM_LIMIT),
    )(*args)


class Tok(NamedTuple):
    arr: Any
    width: int
    col: int = 0
    halo: bool = False
    grad: Any = None
    add: Any = None


class Par(NamedTuple):
    arr: Any
    kind: str = 'const'
    col: int = 0
    width: int = 0
    grad: bool = True


def _tok_specs(toks, tb, rev, ntile):
    specs, args = [], []
    for t in toks:
        if rev:
            cur = lambda j, s, c=t.col: (ntile - 1 - s, c + j)
            tail = lambda j, s, c=t.col: (jnp.maximum((ntile - 1 - s) * (tb // SUBLANES) - 1, 0), c + j)
        else:
            cur = lambda j, s, c=t.col: (s, c + j)
            tail = lambda j, s, c=t.col: (jnp.maximum(s * (tb // SUBLANES) - 1, 0), c + j)
        specs.append(pl.BlockSpec((tb, t.width), cur))
        args.append(t.arr)
        if t.halo:
            specs.append(pl.BlockSpec((SUBLANES, t.width), tail))
            args.append(t.arr)
    return specs, args


def _par_specs(pars, tpb, rev, ntile):
    specs, args = [], []
    for p in pars:
        if p.kind == 'const':
            specs.append(pl.BlockSpec(p.arr.shape, lambda j, s: (0, 0)))
        elif p.kind == 'col':
            specs.append(pl.BlockSpec((p.arr.shape[0], p.width), lambda j, s, c=p.col: (0, c + j)))
        else:
            if rev:
                specs.append(pl.BlockSpec((None,) + p.arr.shape[1:], lambda j, s: ((ntile - 1 - s) // tpb, 0, 0)))
            else:
                specs.append(pl.BlockSpec((None,) + p.arr.shape[1:], lambda j, s: (s // tpb, 0, 0)))
        args.append(p.arr)
    return specs, args


def _rows(ref, r0, n):
    return ref[...] if isinstance(r0, int) else ref[pl.ds(r0, n), :]


def _read_toks(toks, refs, t0, r0, sub):
    vals, k = [], 0
    for t in toks:
        ref = refs[k]
        k += 1
        cur = _rows(ref, r0, sub)
        if t.halo:
            tail = jnp.where(t0 == 0, jnp.zeros_like(refs[k][...]), refs[k][...])
            k += 1
            if not isinstance(r0, int):
                before = ref[pl.ds(pl.multiple_of(jnp.maximum(r0 - SUBLANES, 0), SUBLANES), SUBLANES), :]
                tail = jnp.where(r0 == 0, tail, before)
            vals.append((cur, tail))
        else:
            vals.append(cur)
    return vals, k


def _row_blocks(tb, sub, reverse, block):
    if sub is None or sub >= tb:
        block(0)
        return
    nsub = tb // sub

    def step(n, carry):
        r = nsub - 1 - n if reverse else n
        block(pl.multiple_of(r * sub, sub))
        return carry

    lax.fori_loop(0, nsub, step, 0)


def stage_fwd(fn, toks, pars, outs, *, seq, tb, ncol=1, sub=None, name):
    T = toks[0].arr.shape[0]
    tb = min(tb, seq)
    ntile, tpb = T // tb, seq // tb
    tspecs, targs = _tok_specs(toks, tb, False, ntile)
    pspecs, pargs = _par_specs(pars, tpb, False, ntile)
    n_in = len(tspecs) + len(pspecs)

    def body(*refs):
        s = pl.program_id(1)
        t0 = (s % tpb) * tb
        pvals = [r[...] for r in refs[len(tspecs):n_in]]

        def block(r0):
            tvals, _ = _read_toks(toks, refs, t0, r0, sub)
            res = fn(t0 + r0, *tvals, *pvals)
            for r, v in zip(refs[n_in:], res):
                if isinstance(r0, int):
                    r[...] = v.astype(r.dtype)
                else:
                    r[pl.ds(r0, sub), :] = v.astype(r.dtype)

        _row_blocks(tb, sub, False, block)

    return pl.pallas_call(
        body, name=name, grid=(ncol, ntile),
        in_specs=tspecs + pspecs,
        out_specs=[pl.BlockSpec((tb, w), lambda j, s: (s, j)) for w, _ in outs],
        out_shape=[jax.ShapeDtypeStruct((T, w * ncol), d) for w, d in outs],
        compiler_params=pltpu.CompilerParams(dimension_semantics=("arbitrary", "arbitrary"),
                                             vmem_limit_bytes=VMEM_LIMIT),
    )(*targs, *pargs)


def stage_bwd(fn, toks, pars, cots, *, cot_fn=None, seq, tb, ncol=1, sub=None, name):
    T = toks[0].arr.shape[0]
    tb = min(tb, seq)
    sub = SUB_ROWS.get(fn) if sub is None else sub
    ntile, tpb = T // tb, seq // tb
    tspecs, targs = _tok_specs(toks, tb, True, ntile)
    pspecs, pargs = _par_specs(pars, tpb, True, ntile)
    cspecs, cargs = _tok_specs(cots, tb, True, ntile)
    adds = [t for t in toks if t.add is not None]
    assert all(t.grad is not None for t in adds)
    aspecs = [pl.BlockSpec((tb, t.width), lambda j, s: (ntile - 1 - s, j)) for t in adds]
    aargs = [t.add for t in adds]
    n_t, n_p, n_c = len(tspecs), len(pspecs), len(cspecs)

    gtoks = [t for t in toks if t.grad is not None]
    gpars = [p for p in pars if p.grad]
    out_specs, out_shape = [], []
    for t in gtoks:
        out_specs.append(pl.BlockSpec((tb, t.width), lambda j, s: (ntile - 1 - s, j)))
        out_shape.append(jax.ShapeDtypeStruct((T, t.width * ncol), t.grad))
    for p in gpars:
        if p.kind == 'const':
            out_specs.append(pl.BlockSpec(p.arr.shape, lambda j, s: (0, 0)))
            out_shape.append(jax.ShapeDtypeStruct(p.arr.shape, F32))
        elif p.kind == 'col':
            out_specs.append(pl.BlockSpec((p.arr.shape[0], p.width), lambda j, s: (0, j)))
            out_shape.append(jax.ShapeDtypeStruct((p.arr.shape[0], p.width * ncol), F32))
        else:
            out_specs.append(pl.BlockSpec((None,) + p.arr.shape[1:], lambda j, s: ((ntile - 1 - s) // tpb, 0, 0)))
            out_shape.append(jax.ShapeDtypeStruct(p.arr.shape, F32))
    carries = [t for t in gtoks if t.halo]
    scratch = [pltpu.VMEM((SUBLANES, t.width), F32) for t in carries]

    def body(*refs):
        j, s = pl.program_id(0), pl.program_id(1)
        i = ntile - 1 - s
        t0 = (i % tpb) * tb
        t_refs = refs[:n_t]
        p_refs = refs[n_t:n_t + n_p]
        c_refs = refs[n_t + n_p:n_t + n_p + n_c]
        a_refs = refs[n_t + n_p + n_c:n_t + n_p + n_c + len(adds)]
        o_refs = refs[n_t + n_p + n_c + len(adds):]
        gt_refs = o_refs[:len(gtoks)]
        gp_refs = o_refs[len(gtoks):len(gtoks) + len(gpars)]
        carry_refs = o_refs[len(gtoks) + len(gpars):]

        pvals = [r[...] for r in p_refs]

        @pl.when(s == 0)
        def _():
            for carry in carry_refs:
                carry[...] = jnp.zeros_like(carry)

        for p, ref in zip(gpars, gp_refs):
            if p.kind == 'const':
                first = jnp.logical_and(j == 0, s == 0)
            elif p.kind == 'col':
                first = s == 0
            else:
                first = s % tpb == 0

            @pl.when(first)
            def _(ref=ref):
                ref[...] = jnp.zeros_like(ref)

        def block(r0):
            rows = tb if isinstance(r0, int) else sub
            tt = t0 + r0
            tvals, _ = _read_toks(toks, t_refs, t0, r0, sub)
            cvals, _ = _read_toks(cots, c_refs, t0, r0, sub)

            def f(tv, pv):
                return tuple(fn(tt, *tv, *pv))

            res, vjp = jax.vjp(f, tvals, pvals)
            ct = cot_fn(tt, *cvals) if cot_fn is not None else tuple(cvals)
            ct = tuple(c.astype(r.dtype) for c, r in zip(ct, res))
            dt, dp = vjp(ct)

            ci = 0
            ai = 0
            gi = 0
            for t, d in zip(toks, dt):
                if t.grad is None:
                    continue
                ref = gt_refs[gi]
                gi += 1
                if t.halo:
                    dcur, dtail = d
                    carry = carry_refs[ci]
                    ci += 1
                    top = dcur[:rows - SUBLANES] if rows > SUBLANES else None
                    bot = dcur[rows - SUBLANES:] + carry[...]
                    dcur = bot if top is None else jnp.concatenate([top, bot], axis=0)
                    carry[...] = jnp.where(tt == 0, jnp.zeros_like(dtail), dtail)
                else:
                    dcur = d
                if t.add is not None:
                    dcur = dcur + _rows(a_refs[ai], r0, rows).astype(F32)
                    ai += 1
                if isinstance(r0, int):
                    ref[...] = dcur.astype(ref.dtype)
                else:
                    ref[pl.ds(r0, rows), :] = dcur.astype(ref.dtype)

            gi = 0
            for p, d in zip(pars, dp):
                if not p.grad:
                    continue
                gp_refs[gi][...] += d.astype(F32)
                gi += 1

        _row_blocks(tb, sub, True, block)

    res = pl.pallas_call(
        body, name=name, grid=(ncol, ntile),
        in_specs=tspecs + pspecs + cspecs + aspecs,
        out_specs=out_specs, out_shape=out_shape, scratch_shapes=scratch,
        compiler_params=pltpu.CompilerParams(dimension_semantics=("arbitrary", "arbitrary"),
                                             vmem_limit_bytes=VMEM_LIMIT),
    )(*targs, *pargs, *cargs, *aargs)
    return list(res[:len(gtoks)]), list(res[len(gtoks):])


def _bcast_row(x, r):
    return jnp.broadcast_to(x[r:r + 1, :], x.shape)


SCAN_TB = 512


def scan_real(a, b, *, seq, reverse, name):
    T, C = a.shape
    tb = min(SCAN_TB, seq)
    nb, nt, nblk = T // seq, seq // tb, tb // SUBLANES

    def body(a_ref, b_ref, h_ref, carry_h, carry_a):
        @pl.when(pl.program_id(1) == 0)
        def _():
            carry_h[...] = jnp.zeros_like(carry_h)
            carry_a[...] = jnp.zeros_like(carry_a)

        row = lax.broadcasted_iota(jnp.int32, (SUBLANES, C), 0)

        def blk(n, c):
            ch, ca = c
            k = nblk - 1 - n if reverse else n
            o = pl.multiple_of(k * SUBLANES, SUBLANES)
            A = a_ref[pl.ds(o, SUBLANES), :]
            B = b_ref[pl.ds(o, SUBLANES), :]
            if reverse:
                a_first = _bcast_row(A, 0)
                A = jnp.where(row == SUBLANES - 1, ca, pltpu.roll(A, SUBLANES - 1, 0))
                for s in (1, 2, 4):
                    keep = row < SUBLANES - s
                    Bs = jnp.where(keep, pltpu.roll(B, SUBLANES - s, 0), 0.0)
                    As = jnp.where(keep, pltpu.roll(A, SUBLANES - s, 0), 1.0)
                    B = B + A * Bs
                    A = A * As
                h = B + A * ch
                h_ref[pl.ds(o, SUBLANES), :] = h
                return _bcast_row(h, 0), a_first
            for s in (1, 2, 4):
                keep = row >= s
                Bs = jnp.where(keep, pltpu.roll(B, s, 0), 0.0)
                As = jnp.where(keep, pltpu.roll(A, s, 0), 1.0)
                B = B + A * Bs
                A = A * As
            h = B + A * ch
            h_ref[pl.ds(o, SUBLANES), :] = h
            return _bcast_row(h, SUBLANES - 1), ca

        ch, ca = lax.fori_loop(0, nblk, blk, (carry_h[...], carry_a[...]))
        carry_h[...] = ch
        carry_a[...] = ca

    if reverse:
        spec = pl.BlockSpec((tb, C), lambda bi, i: (bi * nt + nt - 1 - i, 0))
    else:
        spec = pl.BlockSpec((tb, C), lambda bi, i: (bi * nt + i, 0))
    return pl.pallas_call(
        body, name=name, grid=(nb, nt), in_specs=[spec, spec], out_specs=spec,
        out_shape=jax.ShapeDtypeStruct((T, C), F32),
        scratch_shapes=[pltpu.VMEM((SUBLANES, C), F32), pltpu.VMEM((SUBLANES, C), F32)],
        compiler_params=pltpu.CompilerParams(dimension_semantics=("arbitrary", "arbitrary"),
                                             vmem_limit_bytes=VMEM_LIMIT),
    )(a, b)


def scan_cplx(ar, ai, bu, *, seq, reverse, name):
    T = bu.shape[0]
    cw = S5_CW
    ncol = C_CH // cw
    tb = min(SCAN_TB, seq)
    nb, nt, nblk = T // seq, seq // tb, tb // SUBLANES

    def body(ar_ref, ai_ref, b_ref, h_ref, carry_r, carry_i):
        @pl.when(pl.program_id(2) == 0)
        def _():
            carry_r[...] = jnp.zeros_like(carry_r)
            carry_i[...] = jnp.zeros_like(carry_i)

        row = lax.broadcasted_iota(jnp.int32, (SUBLANES, cw), 0)
        Ar = jnp.broadcast_to(ar_ref[...], (SUBLANES, cw))
        Ai = jnp.broadcast_to(ai_ref[...], (SUBLANES, cw))
        levels = []
        for s in (1, 2, 4):
            keep = (row < SUBLANES - s) if reverse else (row >= s)
            sh = SUBLANES - s if reverse else s
            levels.append((Ar, Ai, keep, sh))
            Asr = jnp.where(keep, pltpu.roll(Ar, sh, 0), 1.0)
            Asi = jnp.where(keep, pltpu.roll(Ai, sh, 0), 0.0)
            Ar, Ai = Ar * Asr - Ai * Asi, Ar * Asi + Ai * Asr

        def blk(n, c):
            cr, ci = c
            k = nblk - 1 - n if reverse else n
            o = pl.multiple_of(k * SUBLANES, SUBLANES)
            Br = b_ref[pl.ds(o, SUBLANES), :cw]
            Bi = b_ref[pl.ds(o, SUBLANES), cw:]
            for lr, li, keep, sh in levels:
                Bsr = jnp.where(keep, pltpu.roll(Br, sh, 0), 0.0)
                Bsi = jnp.where(keep, pltpu.roll(Bi, sh, 0), 0.0)
                Br, Bi = Br + lr * Bsr - li * Bsi, Bi + lr * Bsi + li * Bsr
            hr = Br + Ar * cr - Ai * ci
            hi = Bi + Ar * ci + Ai * cr
            h_ref[pl.ds(o, SUBLANES), :cw] = hr
            h_ref[pl.ds(o, SUBLANES), cw:] = hi
            last = 0 if reverse else SUBLANES - 1
            return _bcast_row(hr, last), _bcast_row(hi, last)

        cr, ci = lax.fori_loop(0, nblk, blk, (carry_r[...], carry_i[...]))
        carry_r[...] = cr
        carry_i[...] = ci

    if reverse:
        spec = pl.BlockSpec((tb, 2 * cw), lambda bi, j, i: (bi * nt + nt - 1 - i, j))
    else:
        spec = pl.BlockSpec((tb, 2 * cw), lambda bi, j, i: (bi * nt + i, j))
    aspec = pl.BlockSpec((1, cw), lambda bi, j, i: (0, j))
    return pl.pallas_call(
        body, name=name, grid=(nb, ncol, nt), in_specs=[aspec, aspec, spec], out_specs=spec,
        out_shape=jax.ShapeDtypeStruct((T, 2 * C_CH), F32),
        scratch_shapes=[pltpu.VMEM((SUBLANES, cw), F32), pltpu.VMEM((SUBLANES, cw), F32)],
        compiler_params=pltpu.CompilerParams(dimension_semantics=("arbitrary", "arbitrary", "arbitrary"),
                                             vmem_limit_bytes=VMEM_LIMIT),
    )(ar, ai, bu)


def _col_sums(x):
    ones = jnp.ones((x.shape[0], x.shape[0]), BF)
    acc, rest = None, x
    for _ in range(3):
        piece = rest.astype(BF)
        rest = rest - piece.astype(F32)
        term = lax.dot_general(ones, piece, (((1,), (0,)), ((), ())), preferred_element_type=F32)
        acc = term if acc is None else acc + term
    return acc


@jax.custom_vjp
def col_sums(x):
    return _col_sums(x)


col_sums.defvjp(lambda x: (_col_sums(x), None), lambda _, g: (_col_sums(g),))


def tri_inv(As):
    c = As[0].shape[0]
    ii = lax.broadcasted_iota(jnp.int32, (c, c), 0)
    jj = lax.broadcasted_iota(jnp.int32, (c, c), 1)
    eye = jnp.where(ii == jj, 1.0, 0.0)
    T = [eye - a for a in As]
    P = [_dotx(a, a, False, False) for a in As]
    for _ in range(4):
        both = [_dotx(jnp.concatenate([t, p], axis=0), p, False, False) for t, p in zip(T, P)]
        T = [t + b[:c] for t, b in zip(T, both)]
        P = [b[c:] for b in both]
    return [t + _dotx(t, p, False, False) for t, p in zip(T, P)]


@jax.custom_vjp
def tri_inv_saved(a, t):
    return t


tri_inv_saved.defvjp(lambda a, t: (t, t),
                     lambda t, g: (-_dotx(_dotx(t, g, True, False), t, False, True), jnp.zeros_like(t)))


def delta_chunks(ins, tinvs=None):
    c = B_CHUNK
    Q, K, V, BB, GB, S = (list(t) for t in zip(*ins))
    n = range(len(ins))
    ii = lax.broadcasted_iota(jnp.int32, (c, c), 0)
    jj = lax.broadcasted_iota(jnp.int32, (c, c), 1)
    incl, strict, diag = ii >= jj, ii > jj, ii == jj
    qc = [q * (B_DK ** -0.5) for q in Q]
    gc = [cumsum_rows(g) for g in GB]
    gcol = [x[:, :c] for x in gc]
    grow = [col_sums(jnp.where(diag, x, 0.0)) for x in gcol]
    decay = [jnp.exp(jnp.where(incl, a - b, -1e30)) for a, b in zip(gcol, grow)]
    kb = [k * b for k, b in zip(K, BB)]
    kk = [bmm(kb[i], K[i], False, True) for i in n]
    a_mat = [jnp.where(strict, kk[i] * decay[i], 0.0) for i in n]
    tinv = tri_inv(a_mat) if tinvs is None else [tri_inv_saved(a, t) for a, t in zip(a_mat, tinvs)]
    eg = [jnp.exp(x) for x in gc]
    sol = [xmm(tinv[i], jnp.concatenate([V[i] * BB[i], kb[i] * eg[i]], axis=-1)) for i in n]
    qkr = [bmm(qc[i], K[i], False, True) for i in n]
    qk = [jnp.where(incl, qkr[i] * decay[i], 0.0) for i in n]
    glast = [x[c - 1:c, :] for x in gc]
    k_dec = [K[i] * jnp.exp(glast[i] - gc[i]) for i in n]
    ws = [bmm(sol[i][:, B_DK:], S[i]) for i in n]
    v_new = [sol[i][:, :B_DK] - ws[i] for i in n]
    o1 = [bmm(qc[i] * eg[i], S[i]) for i in n]
    o2 = [bmm(qk[i], v_new[i]) for i in n]
    kv = [bmm(k_dec[i], v_new[i], True, False) for i in n]
    o = [o1[i] + o2[i] for i in n]
    s_new = [S[i] * jnp.exp(glast[i]) + kv[i] for i in n]
    return o, s_new, tinv


def delta_fwd(q, k, v, bB, gB, *, name):
    nb, seq, _ = q.shape
    n = seq // B_CHUNK
    hd = B_DK

    def body(q_ref, k_ref, v_ref, b_ref, g_ref, o_ref, st_ref, ti_ref, state):
        @pl.when(pl.program_id(0) == 0)
        def _():
            state[...] = jnp.zeros_like(state)

        pairs = [(b, h) for b in range(nb) for h in range(B_HEADS)]
        sls = [slice(h * hd, (h + 1) * hd) for _, h in pairs]
        ins = [(q_ref[b, :, sl], k_ref[b, :, sl], v_ref[b, :, sl], b_ref[b, :, sl], g_ref[b, :, sl],
                state[b * B_HEADS + h]) for (b, h), sl in zip(pairs, sls)]
        os, s_news, tinvs = delta_chunks(ins)
        for (b, h), sl, a, o, s_new, tinv in zip(pairs, sls, ins, os, s_news, tinvs):
            st_ref[b, h] = a[5]
            o_ref[b, :, sl] = o
            ti_ref[b, h] = tinv
            state[b * B_HEADS + h] = s_new

    spec = pl.BlockSpec((nb, B_CHUNK, BW), lambda i: (0, i, 0))
    return pl.pallas_call(
        body, name=name, grid=(n,), in_specs=[spec] * 5,
        out_specs=[spec, pl.BlockSpec((nb, None, B_HEADS, hd, hd), lambda i: (0, i, 0, 0, 0)),
                   pl.BlockSpec((nb, None, B_HEADS, B_CHUNK, B_CHUNK), lambda i: (0, i, 0, 0, 0))],
        out_shape=[jax.ShapeDtypeStruct((nb, seq, BW), F32), jax.ShapeDtypeStruct((nb, n, B_HEADS, hd, hd), F32),
                   jax.ShapeDtypeStruct((nb, n, B_HEADS, B_CHUNK, B_CHUNK), F32)],
        scratch_shapes=[pltpu.VMEM((nb * B_HEADS, hd, hd), F32)],
        compiler_params=pltpu.CompilerParams(dimension_semantics=("arbitrary",), vmem_limit_bytes=VMEM_LIMIT),
    )(q, k, v, bB, gB)


def delta_bwd(q, k, v, bB, gB, states, tinvs, do, *, name):
    nb, seq, _ = q.shape
    n = seq // B_CHUNK
    hd = B_DK

    def body(q_ref, k_ref, v_ref, b_ref, g_ref, st_ref, ti_ref, do_ref, dq_ref, dk_ref, dv_ref, db_ref, dg_ref, dstate):
        @pl.when(pl.program_id(0) == 0)
        def _():
            dstate[...] = jnp.zeros_like(dstate)

        pairs = [(b, h) for b in range(nb) for h in range(B_HEADS)]
        sls = [slice(h * hd, (h + 1) * hd) for _, h in pairs]
        ins = [(q_ref[b, :, sl], k_ref[b, :, sl], v_ref[b, :, sl], b_ref[b, :, sl], g_ref[b, :, sl], st_ref[b, h])
               for (b, h), sl in zip(pairs, sls)]
        saved = [ti_ref[b, h] for b, h in pairs]
        d_o = [do_ref[b, :, sl] for (b, h), sl in zip(pairs, sls)]
        d_s = [dstate[b * B_HEADS + h] for b, h in pairs]

        def f(xs):
            return tuple(delta_chunks(xs, saved)[:2])

        grads, = jax.vjp(f, ins)[1]((d_o, d_s))
        for (b, h), sl, (dq, dk, dv, db, dg, ds) in zip(pairs, sls, grads):
            dq_ref[b, :, sl] = dq
            dk_ref[b, :, sl] = dk
            dv_ref[b, :, sl] = dv
            db_ref[b, :, sl] = db
            dg_ref[b, :, sl] = dg
            dstate[b * B_HEADS + h] = ds

    spec = pl.BlockSpec((nb, B_CHUNK, BW), lambda i: (0, n - 1 - i, 0))
    sspec = pl.BlockSpec((nb, None, B_HEADS, hd, hd), lambda i: (0, n - 1 - i, 0, 0, 0))
    tspec = pl.BlockSpec((nb, None, B_HEADS, B_CHUNK, B_CHUNK), lambda i: (0, n - 1 - i, 0, 0, 0))
    return pl.pallas_call(
        body, name=name, grid=(n,), in_specs=[spec] * 5 + [sspec, tspec, spec],
        out_specs=[spec] * 5, out_shape=[jax.ShapeDtypeStruct((nb, seq, BW), F32)] * 5,
        scratch_shapes=[pltpu.VMEM((nb * B_HEADS, hd, hd), F32)],
        compiler_params=pltpu.CompilerParams(dimension_semantics=("arbitrary",), vmem_limit_bytes=VMEM_LIMIT),
    )(q, k, v, bB, gB, states, tinvs, do)


def loss_head(x, tgt, g, *, tb, name):
    T, D = x.shape
    tb = min(tb, T)
    nt = T // tb

    def body(x_ref, t_ref, g_ref, l_ref, dx_ref, dg_ref):
        tg = t_ref[...]

        def f(xv, gv):
            err = rms(xv, gv) - tg
            return 0.5 * jnp.mean(err * err, axis=-1, keepdims=True)

        rows, vjp = jax.vjp(f, x_ref[...], g_ref[...])
        dx, dg = vjp(jnp.ones_like(rows))
        dx_ref[...] = dx
        tot = jnp.broadcast_to(jnp.sum(rows, axis=0, keepdims=True), (1, 128))

        @pl.when(pl.program_id(0) == 0)
        def _():
            l_ref[...] = tot
            dg_ref[...] = dg

        @pl.when(pl.program_id(0) > 0)
        def _():
            l_ref[...] += tot
            dg_ref[...] += dg

    tok = pl.BlockSpec((tb, D), lambda i: (i, 0))
    return pl.pallas_call(
        body, name=name, grid=(nt,),
        in_specs=[tok, tok, pl.BlockSpec((1, D), lambda i: (0, 0))],
        out_specs=[pl.BlockSpec((1, 128), lambda i: (0, 0)), tok, pl.BlockSpec((1, D), lambda i: (0, 0))],
        out_shape=[jax.ShapeDtypeStruct((1, 128), F32), jax.ShapeDtypeStruct((T, D), F32), jax.ShapeDtypeStruct((1, D), F32)],
        compiler_params=pltpu.CompilerParams(dimension_semantics=("arbitrary",), vmem_limit_bytes=VMEM_LIMIT),
    )(x, tgt, g)


def _row_block(rows, cols):
    budget = 256 * 1024
    tr = max(SUBLANES, min(rows, budget // max(cols, 1)) // SUBLANES * SUBLANES)
    while rows % tr:
        tr -= SUBLANES
        if tr <= 0:
            return rows
    return tr


def adamw(w, g, m, v, *, name):
    R, C = w.shape
    tr = _row_block(R, C)
    c1 = 1.0 / (1.0 - ADAM_B1 ** ADAM_STEP)
    c2 = 1.0 / (1.0 - ADAM_B2 ** ADAM_STEP)

    def body(w_ref, g_ref, m_ref, v_ref, d_ref, mo_ref, vo_ref):
        gv = g_ref[...]
        mn = ADAM_B1 * m_ref[...] + (1.0 - ADAM_B1) * gv
        vn = ADAM_B2 * v_ref[...] + (1.0 - ADAM_B2) * (gv * gv)
        d_ref[...] = -ADAM_LR * ((mn * c1) / (jnp.sqrt(vn * c2) + ADAM_EPS) + ADAM_WD * w_ref[...])
        mo_ref[...] = mn
        vo_ref[...] = vn

    spec = pl.BlockSpec((tr, C), lambda i: (i, 0))
    return pl.pallas_call(
        body, name=name, grid=(R // tr,), in_specs=[spec] * 4, out_specs=[spec] * 3,
        out_shape=[jax.ShapeDtypeStruct((R, C), F32)] * 3,
        compiler_params=pltpu.CompilerParams(dimension_semantics=("parallel",), vmem_limit_bytes=VMEM_LIMIT),
    )(w, g, m, v)


def sum_slabs(x, *, name):
    n, R, C = x.shape
    tr = _row_block(R, C * 2)

    def body(x_ref, o_ref):
        acc = x_ref[0].astype(F32)
        for d in range(1, n):
            acc = acc + x_ref[d].astype(F32)
        o_ref[...] = acc

    return pl.pallas_call(
        body, name=name, grid=(R // tr,),
        in_specs=[pl.BlockSpec((n, tr, C), lambda i: (0, i, 0))],
        out_specs=pl.BlockSpec((tr, C), lambda i: (i, 0)),
        out_shape=jax.ShapeDtypeStruct((R, C), F32),
        compiler_params=pltpu.CompilerParams(dimension_semantics=("parallel",), vmem_limit_bytes=VMEM_LIMIT),
    )(x)


def _gather(src, *, name):
    R, C = src.shape

    def body(src_ref, out_ref, send_sems, recv_sems, local_sem):
        x, y, c = lax.axis_index("x"), lax.axis_index("y"), lax.axis_index("c")
        me, sibling = (x, y, c), (x, y, 1 - c)
        chips = [(1 - x, y), (x, 1 - y), (1 - x, 1 - y)]

        def slab(px, py, pc):
            return out_ref.at[4 * px + 2 * py + pc]

        def copy(k, block, to, first_hand=False):
            return pltpu.make_async_remote_copy(
                src_ref=src_ref if first_hand else slab(*block), dst_ref=slab(*block),
                send_sem=send_sems.at[k], recv_sem=recv_sems.at[k],
                device_id=to, device_id_type=pl.DeviceIdType.MESH)

        mine = pltpu.make_async_copy(src_ref, slab(*me), local_sem)
        mine.start()
        first = [copy(0, me, sibling, True)] + [copy(1 + j, me, (*chip, c), True) for j, chip in enumerate(chips)]
        for cp in first:
            cp.start()
        passed = [copy(4 + j, (*chip, c), sibling) for j, chip in enumerate(chips)]
        for j, chip in enumerate(chips):
            copy(1 + j, (*chip, c), me).wait_recv()
            passed[j].start()
        copy(0, sibling, me).wait_recv()
        for j, chip in enumerate(chips):
            copy(4 + j, (*chip, 1 - c), me).wait_recv()
        for cp in first + passed:
            cp.wait_send()
        mine.wait()

    return pl.pallas_call(
        body, name=name,
        in_specs=[pl.BlockSpec(memory_space=pl.ANY)],
        out_specs=pl.BlockSpec(memory_space=pl.ANY),
        out_shape=jax.ShapeDtypeStruct((N_DEV, R, C), src.dtype),
        scratch_shapes=[pltpu.SemaphoreType.DMA((N_DEV - 1,)), pltpu.SemaphoreType.DMA((N_DEV - 1,)),
                        pltpu.SemaphoreType.DMA],
    )(src)


def _exchange(src, *, scatter, name):
    R, C = src.shape[-2:]

    def body(src_ref, out_ref, send_sems, recv_sems, local_sem):
        x, y, c = lax.axis_index("x"), lax.axis_index("y"), lax.axis_index("c")
        me = 4 * x + 2 * y + c

        def peer(k):
            return (x ^ ((k >> 2) & 1), y ^ ((k >> 1) & 1), c ^ (k & 1))

        def peer_index(k):
            px, py, pc = peer(k)
            return 4 * px + 2 * py + pc

        mine = pltpu.make_async_copy(src_ref.at[me] if scatter else src_ref, out_ref.at[me], local_sem)
        mine.start()
        copies = []
        for k in range(1, N_DEV):
            cp = pltpu.make_async_remote_copy(
                src_ref=src_ref.at[peer_index(k)] if scatter else src_ref,
                dst_ref=out_ref.at[me],
                send_sem=send_sems.at[k - 1], recv_sem=recv_sems.at[k - 1],
                device_id=peer(k), device_id_type=pl.DeviceIdType.MESH)
            cp.start()
            copies.append(cp)
        for k in range(1, N_DEV):
            pltpu.make_async_remote_copy(
                src_ref=src_ref.at[peer_index(k)] if scatter else src_ref,
                dst_ref=out_ref.at[peer_index(k)],
                send_sem=send_sems.at[k - 1], recv_sem=recv_sems.at[k - 1],
                device_id=peer(k), device_id_type=pl.DeviceIdType.MESH).wait_recv()
        for cp in copies:
            cp.wait_send()
        mine.wait()

    return pl.pallas_call(
        body, name=name,
        in_specs=[pl.BlockSpec(memory_space=pl.ANY)],
        out_specs=pl.BlockSpec(memory_space=pl.ANY),
        out_shape=jax.ShapeDtypeStruct((N_DEV, R, C), src.dtype),
        scratch_shapes=[pltpu.SemaphoreType.DMA((N_DEV - 1,)), pltpu.SemaphoreType.DMA((N_DEV - 1,)),
                        pltpu.SemaphoreType.DMA],
    )(src)


TB = 256


def st_norm(t0, x, g):
    return (rms(x.astype(F32), g),)


def _conv(xt, w, bias=None):
    kk = w.shape[0]
    acc = bias
    for i in range(kk):
        term = w[i:i + 1, :] * shift_rows(xt, kk - 1 - i)
        acc = term if acc is None else acc + term
    return acc


def st_a(t0, xa, cw, cb, wr, br, wi, bi, lam):
    xc = _conv(xa, cw, cb)
    r = sigmoid(bmm(xc, wr) + br)
    ig = sigmoid(bmm(xc, wi) + bi)
    log_a = -RG_C * r * softplus(-lam)
    row = lax.broadcasted_iota(jnp.int32, xc.shape, 0) + t0
    mult = jnp.where(row == 0, 1.0, jnp.sqrt(-expm1(2.0 * log_a)))
    return jnp.exp(log_a), mult * ig * xc


def st_a_cot(t0, lam, ha):
    return lam * shift_rows(ha, 1), lam


def _heads(x, n, w):
    return [x[:, h * w:(h + 1) * w] for h in range(n)]


def st_b(t0, q, k, v, pba, cw, alog, dtb):
    qc = silu(_conv(q, cw[:, 0:BW]))
    kc = silu(_conv(k, cw[:, BW:2 * BW]))
    vc = silu(_conv(v, cw[:, 2 * BW:3 * BW]))

    def l2n(x):
        return jnp.concatenate([s * lax.rsqrt(jnp.sum(s * s, axis=-1, keepdims=True) + EPS)
                                for s in _heads(x, B_HEADS, B_DK)], axis=-1)

    sg = sigmoid(pba)
    gg = -jnp.exp(alog) * softplus(pba + dtb)
    lane = lax.broadcasted_iota(jnp.int32, pba.shape, 1)

    def spread(x, first):
        return jnp.concatenate(
            [jnp.broadcast_to(jnp.sum(jnp.where(lane == first + h, x, 0.0), axis=-1, keepdims=True), (x.shape[0], B_DK))
             for h in range(B_HEADS)], axis=-1)

    return l2n(qc), l2n(kc), vc, spread(sg, 0), spread(gg, B_HEADS)


def st_m2(t0, ha, ga, o, z, y0, uc, bn4, cd, wglu, bglu):
    ya = ha * gelu(ga)
    yb = jnp.concatenate(
        [oh * lax.rsqrt(jnp.mean(oh * oh, axis=-1, keepdims=True) + EPS) * bh * silu(zh)
         for oh, bh, zh in zip(_heads(o, B_HEADS, B_DK), _heads(bn4, B_HEADS, B_DK), _heads(z, B_HEADS, B_DK))], axis=-1)
    yc0 = gelu(y0 + cd * uc)
    yc = yc0 * sigmoid(bmm(yc0, wglu) + bglu)
    return (jnp.concatenate([ya, yb, yc], axis=-1),)


def st_m2_cot(t0, d0, d1, d2):
    return (jnp.concatenate([d0, d1, d2], axis=-1),)


def st_m3(t0, g0, g1, g2, p0, p1, p2, bg):
    d = D_MODEL
    return (sigmoid(g0 + bg[:, 0:d]) * p0 + sigmoid(g1 + bg[:, d:2 * d]) * p1 + sigmoid(g2 + bg[:, 2 * d:3 * d]) * p2,)


def st_att(t0, q, kv):
    hs = range(X_HEADS)
    sc = [bmm(q[:, h * X_HD:(h + 1) * X_HD], kv[:, h * X_HD:(h + 1) * X_HD], False, True) * (X_HD ** -0.5) for h in hs]
    e = [jnp.exp(s - lax.stop_gradient(jnp.max(s, axis=-1, keepdims=True))) for s in sc]
    p = [x / jnp.sum(x, axis=-1, keepdims=True) for x in e]
    outs = [bmm(p[h], kv[:, D_MODEL + h * X_HD:D_MODEL + (h + 1) * X_HD]) for h in hs]
    return (jnp.concatenate(outs, axis=-1),)


def st_f2(t0, ug, uv, cwg, cwv, cbg, cbv):
    return (gelu(_conv(ug, cwg, cbg)) * _conv(uv, cwv, cbv),)


def st_s5step(t0, h, ar, ai):
    hp = shift_rows(h, 1)
    hr, hi = hp[:, :S5_CW], hp[:, S5_CW:]
    return (jnp.concatenate([ar * hr - ai * hi, ar * hi + ai * hr], axis=-1),)


SUB_ROWS = {st_f2: 32}


W_MAIN = 6 * BW
W_BA = W_MAIN + 2 * B_HEADS
W_UC = W_BA + BW
COL_GATES, COL_UC = W_MAIN, W_MAIN + 3 * D_MODEL


def split_w_in(wt):
    big = jnp.concatenate([wt[:, :W_MAIN], wt[:, W_UC:], wt[:, W_BA:W_UC]], axis=1)
    ba = jnp.pad(wt[:, W_MAIN:W_BA], ((0, 0), (0, 128 - 2 * B_HEADS), (0, 0)))
    return big, ba


def merge_w_in(big, ba):
    return jnp.concatenate([big[:, :W_MAIN], ba[:, :2 * B_HEADS], big[:, COL_UC:], big[:, COL_GATES:COL_UC]], axis=1)


def derive(r):
    L = r['a_w_r'].shape[0]
    eye_a = jnp.eye(A_HEADS, dtype=F32)
    eye_g = jnp.eye(C_GROUPS, dtype=F32)

    def blockdiag(w):
        return jnp.einsum('lhij,hg->lhigj', w, eye_a).reshape(L, BW, BW)

    def lanes(v, first):
        return jnp.pad(v, ((0, 0), (first, 128 - first - B_HEADS)))[:, None, :]

    lr, li = r['c_lam_re'], r['c_lam_im']
    dt = jnp.exp(r['c_log_dt'])[..., None]
    mag = jnp.exp(lr * dt)
    ar, ai = mag * jnp.cos(li * dt), mag * jnp.sin(li * dt)
    den = lr * lr + li * li
    fr = ((ar - 1.0) * lr + ai * li) / den
    fi = (ai * lr - (ar - 1.0) * li) / den
    br, bi = r['c_b_re'], r['c_b_im']
    bbr = fr[..., None] * br - fi[..., None] * bi
    bbi = fr[..., None] * bi + fi[..., None] * br
    ncol = C_CH // S5_CW

    def b_dense(bb):
        return jnp.einsum('lgpc,gh->lgchp', bb, eye_g).reshape(L, BW, ncol, S5_CW)

    bbig = jnp.concatenate([b_dense(bbr), b_dense(bbi)], axis=3).reshape(L, BW, 2 * C_CH)

    def c_dense(cc):
        return jnp.einsum('lgcp,gh->lgphc', cc, eye_g).reshape(L, ncol, S5_CW, BW)

    cbig = jnp.stack([c_dense(r['c_c_re']), -c_dense(r['c_c_im'])], axis=2).reshape(L, 2 * C_CH, BW)
    return dict(wr=blockdiag(r['a_w_r']), wi=blockdiag(r['a_w_i']),
                alog=lanes(r['b_a_log'], B_HEADS), dtb=lanes(r['b_dt_bias'], B_HEADS),
                bn4=jnp.tile(r['b_norm'], (1, B_HEADS))[:, None, :],
                ar=ar.reshape(L, 1, C_CH), ai=ai.reshape(L, 1, C_CH), bbig=bbig, cbig=cbig)


DERIVE_FROM = ['a_w_r', 'a_w_i', 'b_a_log', 'b_dt_bias', 'b_norm', 'c_lam_re', 'c_lam_im', 'c_log_dt',
               'c_b_re', 'c_b_im', 'c_c_re', 'c_c_im']


def _row(v):
    return v.reshape(1, -1)


def layer_params(w, big, ba, dv, l):
    return dict(
        mix_norm=_row(w['mix_norm'][l]), w_big=big[l], w_ba=ba[l], b_gate=_row(w['b_gate'][l]),
        a_conv_w=w['a_conv_w'][l], a_conv_b=_row(w['a_conv_b'][l]), wr=dv['wr'][l], b_r=_row(w['a_b_r'][l]),
        wi=dv['wi'][l], b_i=_row(w['a_b_i'][l]), lam=_row(w['a_lam'][l]),
        b_conv_w=w['b_conv_w'][l], alog=dv['alog'][l], dtb=dv['dtb'][l], bn4=dv['bn4'][l],
        ar=dv['ar'][l], ai=dv['ai'][l], bbig=dv['bbig'][l], cbig=dv['cbig'][l],
        c_d=_row(w['c_d'][l]), wglu=w['c_glu_w'][l].astype(F32), bglu=_row(w['c_glu_b'][l]),
        w_brT=w['w_brT'][l], w_out=w['w_out'][l],
        xa_norm=_row(w['xa_norm'][l]), mem_norm=_row(w['mem_norm'][l]),
        w_q=w['xa_w_q'][l], w_kvT=w['w_kvT'][l], w_o=w['xa_w_o'][l],
        ffn_norm=_row(w['ffn_norm'][l]), w_upT=w['w_upT'][l], ffn_conv_w=w['ffn_conv_w'][l],
        ffn_conv_b=_row(w['ffn_conv_b'][l]), w_down=w['ffn_w_down'][l])


def _a_pars(p):
    return [Par(p['a_conv_w']), Par(p['a_conv_b']), Par(p['wr']), Par(p['b_r']), Par(p['wi']), Par(p['b_i']), Par(p['lam'])]


def _f2_pars(p):
    d = D_MODEL
    return [Par(p['ffn_conv_w'], 'col', 0, d), Par(p['ffn_conv_w'], 'col', 3, d),
            Par(p['ffn_conv_b'], 'col', 0, d), Par(p['ffn_conv_b'], 'col', 3, d)]


def layer_fwd(x, mem, p, seq, mseq, l):
    d = D_MODEL
    nb = x.shape[0] // seq
    kw = dict(seq=seq, tb=TB)
    n = lambda s: f"{s}_l{l}"
    h, = stage_fwd(st_norm, [Tok(x, d)], [Par(p['mix_norm'])], [(d, MXU_DTYPE)], name=n("norm_mix"), **kw)
    P = mm(h, p['w_big'], tb=True, name=n("mm_in"))
    Pba = mm(h, p['w_ba'], tb=True, name=n("mm_in_ba"))
    a, bb = stage_fwd(st_a, [Tok(P, BW, 0, True)], _a_pars(p), [(BW, F32)] * 2, name=n("rglru_pre"), **kw)
    ha = scan_real(a, bb, seq=seq, reverse=False, name=n("rglru_scan"))
    qn, kn, vv, bB, gB = stage_fwd(
        st_b, [Tok(P, BW, 2, True), Tok(P, BW, 3, True), Tok(P, BW, 4, True), Tok(Pba, 128)],
        [Par(p['b_conv_w']), Par(p['alog']), Par(p['dtb'])], [(BW, F32)] * 5, name=n("delta_pre"), **kw)
    r3 = lambda t: t.reshape(nb, seq, BW)
    o3, states, tinvs = delta_fwd(r3(qn), r3(kn), r3(vv), r3(bB), r3(gB), name=n("delta"))
    o = o3.reshape(-1, BW)
    bu = mm(P, p['bbig'], a_cols=(COL_UC, BW), name=n("mm_s5_in"))
    hs = scan_cplx(p['ar'], p['ai'], bu, seq=seq, reverse=False, name=n("s5_scan"))
    y0 = mm(hs, p['cbig'], name=n("mm_s5_out"))
    m2_toks = [Tok(ha, BW), Tok(P, BW, 1), Tok(o, BW), Tok(P, BW, 5), Tok(y0, BW), Tok(P, BW, COL_UC // BW)]
    m2_pars = [Par(p['bn4']), Par(p['c_d']), Par(p['wglu']), Par(p['bglu'])]
    Y3, = stage_fwd(st_m2, m2_toks, m2_pars, [(3 * BW, MXU_DTYPE)], name=n("branches"), **kw)
    proj = [mm(Y3, p['w_brT'][k], tb=True, a_cols=(k * BW, BW), name=n(f"mm_branch{k}")) for k in range(3)]
    g0 = COL_GATES // d
    m3_toks = [Tok(P, d, g0), Tok(P, d, g0 + 1), Tok(P, d, g0 + 2)] + [Tok(t, d) for t in proj]
    mixed, = stage_fwd(st_m3, m3_toks, [Par(p['b_gate'])], [(d, MXU_DTYPE)], name=n("gate_mix"), **kw)
    x1 = mm(mixed, p['w_out'], residual=x, name=n("mm_out"))
    hx, = stage_fwd(st_norm, [Tok(x1, d)], [Par(p['xa_norm'])], [(d, MXU_DTYPE)], name=n("norm_xa"), **kw)
    mn, = stage_fwd(st_norm, [Tok(mem, d)], [Par(p['mem_norm'])], [(d, MXU_DTYPE)], seq=mseq, tb=TB, name=n("norm_mem"))
    qx = mm(hx, p['w_q'], name=n("mm_q"))
    kv = mm(mn, p['w_kvT'], tb=True, name=n("mm_kv"))
    kv3 = kv.reshape(nb, mseq, 2 * d)
    ox, = stage_fwd(st_att, [Tok(qx, d)], [Par(kv3, 'batch')], [(d, MXU_DTYPE)], name=n("attention"), **kw)
    x2 = mm(ox, p['w_o'], residual=x1, name=n("mm_o"))
    hf, = stage_fwd(st_norm, [Tok(x2, d)], [Par(p['ffn_norm'])], [(d, MXU_DTYPE)], name=n("norm_ffn"), **kw)
    U = mm(hf, p['w_upT'], tb=True, name=n("mm_up"))
    act, = stage_fwd(st_f2, [Tok(U, d, 0, True), Tok(U, d, 3, True)], _f2_pars(p), [(d, MXU_DTYPE)], ncol=3,
                     name=n("ffn_act"), **kw)
    x3 = mm(act, p['w_down'], residual=x2, name=n("mm_down"))
    sv = dict(x=x, h=h, P=P, Pba=Pba, a=a, ha=ha, qn=qn, kn=kn, vv=vv, bB=bB, gB=gB, states=states, tinvs=tinvs, o=o, hs=hs, y0=y0,
              Y3=Y3, proj=proj, mixed=mixed, x1=x1, hx=hx, mn=mn, qx=qx, kv3=kv3, ox=ox, x2=x2, hf=hf, U=U, act=act)
    return x3, sv


def layer_bwd(dx3, mem, p, sv, seq, mseq, l):
    d = D_MODEL
    nb = dx3.shape[0] // seq
    kw = dict(seq=seq, tb=TB)
    n = lambda s: f"{s}_l{l}"
    g = {}
    P, Pba = sv['P'], sv['Pba']
    dact = mm(dx3, p['w_down'], tb=True, name=n("bmm_down_x"))
    g['w_down'] = mm(sv['act'], dx3, ta=True, name=n("bmm_down_w"))
    (dUg, dUv), gp = stage_bwd(st_f2, [Tok(sv['U'], d, 0, True, MXU_DTYPE), Tok(sv['U'], d, 3, True, MXU_DTYPE)], _f2_pars(p),
                               [Tok(dact, d)], ncol=3, name=n("b_ffn_act"), **kw)
    g['ffn_conv_w'] = jnp.concatenate([gp[0], gp[1]], axis=1)
    g['ffn_conv_b'] = jnp.concatenate([gp[2], gp[3]], axis=1)
    dU = jnp.concatenate([dUg, dUv], axis=1)
    dhf = mm(dU, p['w_upT'], name=n("bmm_up_x"))
    g['w_upT'] = mm(dU, sv['hf'], ta=True, name=n("bmm_up_w"))
    (dx2,), (g['ffn_norm'],) = stage_bwd(st_norm, [Tok(sv['x2'], d, grad=F32, add=dx3)], [Par(p['ffn_norm'])],
                                         [Tok(dhf, d)], name=n("b_norm_ffn"), **kw)
    dox = mm(dx2, p['w_o'], tb=True, name=n("bmm_o_x"))
    g['w_o'] = mm(sv['ox'], dx2, ta=True, name=n("bmm_o_w"))
    (dqx,), (dkv3,) = stage_bwd(st_att, [Tok(sv['qx'], d, grad=MXU_DTYPE)], [Par(sv['kv3'], 'batch')], [Tok(dox, d)],
                                name=n("b_attention"), **kw)
    dkv = dkv3.reshape(-1, 2 * d)
    dhx = mm(dqx, p['w_q'], tb=True, name=n("bmm_q_x"))
    g['w_q'] = mm(sv['hx'], dqx, ta=True, name=n("bmm_q_w"))
    dmn = mm(dkv, p['w_kvT'], name=n("bmm_kv_x"))
    g['w_kvT'] = mm(dkv, sv['mn'], ta=True, name=n("bmm_kv_w"))
    _, (g['mem_norm'],) = stage_bwd(st_norm, [Tok(mem, d)], [Par(p['mem_norm'])], [Tok(dmn, d)], seq=mseq, tb=TB,
                                    name=n("b_norm_mem"))
    (dx1,), (g['xa_norm'],) = stage_bwd(st_norm, [Tok(sv['x1'], d, grad=F32, add=dx2)], [Par(p['xa_norm'])],
                                        [Tok(dhx, d)], name=n("b_norm_xa"), **kw)
    dmixed = mm(dx1, p['w_out'], tb=True, name=n("bmm_out_x"))
    g['w_out'] = mm(sv['mixed'], dx1, ta=True, name=n("bmm_out_w"))
    g0 = COL_GATES // d
    m3_toks = [Tok(P, d, g0 + k, grad=MXU_DTYPE) for k in range(3)] + [Tok(t, d, grad=MXU_DTYPE) for t in sv['proj']]
    dm3, (g['b_gate'],) = stage_bwd(st_m3, m3_toks, [Par(p['b_gate'])], [Tok(dmixed, d)], name=n("b_gate_mix"), **kw)
    dgates, dproj = dm3[:3], dm3[3:]
    dY = [mm(dproj[k], p['w_brT'][k], name=n(f"bmm_branch{k}_x")) for k in range(3)]
    g['w_brT'] = jnp.stack([mm(dproj[k], sv['Y3'], ta=True, b_cols=(k * BW, BW), name=n(f"bmm_branch{k}_w"))
                            for k in range(3)])
    m2_toks = [Tok(sv['ha'], BW, grad=F32), Tok(P, BW, 1, grad=MXU_DTYPE), Tok(sv['o'], BW, grad=F32), Tok(P, BW, 5, grad=MXU_DTYPE),
               Tok(sv['y0'], BW, grad=MXU_DTYPE), Tok(P, BW, COL_UC // BW, grad=F32)]
    m2_pars = [Par(p['bn4']), Par(p['c_d']), Par(p['wglu']), Par(p['bglu'])]
    (dha, dga, do, dz, dy0, duc0), (g['bn4'], g['c_d'], g['wglu'], g['bglu']) = stage_bwd(
        st_m2, m2_toks, m2_pars, [Tok(t, BW) for t in dY], cot_fn=st_m2_cot, name=n("b_branches"), **kw)
    dhs = mm(dy0, p['cbig'], tb=True, name=n("bmm_s5_out_x"))
    g['cbig'] = mm(sv['hs'], dy0, ta=True, name=n("bmm_s5_out_w"))
    lam_s = scan_cplx(p['ar'], -p['ai'], dhs, seq=seq, reverse=True, name=n("b_s5_scan"))
    _, (g['ar'], g['ai']) = stage_bwd(st_s5step, [Tok(sv['hs'], 2 * S5_CW, 0, True)],
                                      [Par(p['ar'], 'col', 0, S5_CW), Par(p['ai'], 'col', 0, S5_CW)],
                                      [Tok(lam_s, 2 * S5_CW)], ncol=C_CH // S5_CW, name=n("b_s5_decay"), **kw)
    duc = mm(lam_s, p['bbig'], tb=True, residual=duc0, out_dtype=MXU_DTYPE, name=n("bmm_s5_in_x"))
    g['bbig'] = mm(P, lam_s, ta=True, a_cols=(COL_UC, BW), name=n("bmm_s5_in_w"))
    lam_a = scan_real(sv['a'], dha, seq=seq, reverse=True, name=n("b_rglru_scan"))
    (dxa,), ga = stage_bwd(st_a, [Tok(P, BW, 0, True, MXU_DTYPE)], _a_pars(p), [Tok(lam_a, BW), Tok(sv['ha'], BW, 0, True)],
                           cot_fn=st_a_cot, name=n("b_rglru_pre"), **kw)
    g['a_conv_w'], g['a_conv_b'], g['wr'], g['b_r'], g['wi'], g['b_i'], g['lam'] = ga
    r3 = lambda t: t.reshape(nb, seq, BW)
    dd = delta_bwd(r3(sv['qn']), r3(sv['kn']), r3(sv['vv']), r3(sv['bB']), r3(sv['gB']), sv['states'], sv['tinvs'],
                   r3(do), name=n("b_delta"))
    (dq, dk, dv, dpba), (g['b_conv_w'], g['alog'], g['dtb']) = stage_bwd(
        st_b, [Tok(P, BW, 2, True, MXU_DTYPE), Tok(P, BW, 3, True, MXU_DTYPE), Tok(P, BW, 4, True, MXU_DTYPE), Tok(Pba, 128, grad=MXU_DTYPE)],
        [Par(p['b_conv_w']), Par(p['alog']), Par(p['dtb'])], [Tok(t.reshape(-1, BW), BW) for t in dd],
        name=n("b_delta_pre"), **kw)
    dP = jnp.concatenate([dxa, dga, dq, dk, dv, dz] + list(dgates) + [duc], axis=1)
    dh0 = mm(dpba, p['w_ba'], name=n("bmm_in_ba_x"))
    dh = mm(dP, p['w_big'], residual=dh0, name=n("bmm_in_x"))
    g['w_big'] = mm(dP, sv['h'], ta=True, name=n("bmm_in_w"))
    g['w_ba'] = mm(dpba, sv['h'], ta=True, name=n("bmm_in_ba_w"))
    (dx,), (g['mix_norm'],) = stage_bwd(st_norm, [Tok(sv['x'], d, grad=F32, add=dx1)], [Par(p['mix_norm'])],
                                        [Tok(dh, d)], name=n("b_norm_mix"), **kw)
    return dx, g


def local_step(x3d, mem3d, tgt3d, w, final_norm):
    nb, seq, d = x3d.shape
    mseq = mem3d.shape[1]
    x = x3d.reshape(nb * seq, d)
    mem = mem3d.reshape(nb * mseq, d)
    L = w['mix_norm'].shape[0]
    big, ba = split_w_in(w['w_inT'])
    dv, dv_vjp = jax.vjp(derive, {k: w[k] for k in DERIVE_FROM})
    ps, svs = [], []
    for l in range(L):
        p = layer_params(w, big, ba, dv, l)
        x, sv = layer_fwd(x, mem, p, seq, mseq, l)
        ps.append(p)
        svs.append(sv)
    loss, dx, g_final = loss_head(x, tgt3d.reshape(nb * seq, d), _row(final_norm), tb=TB, name="loss_head")
    gs = [None] * L
    for l in reversed(range(L)):
        dx, gs[l] = layer_bwd(dx, mem, ps[l], svs[l], seq, mseq, l)
    st = lambda k: jnp.stack([gs[l][k] for l in range(L)])
    vec = lambda k: st(k).reshape(L, -1)
    gd = dv_vjp({k: st(k) for k in ('wr', 'wi', 'alog', 'dtb', 'bn4', 'ar', 'ai', 'bbig', 'cbig')})[0]
    out = dict(gd)
    out.update(
        mix_norm=vec('mix_norm'), w_inT=merge_w_in(st('w_big'), st('w_ba')), b_gate=vec('b_gate'),
        a_conv_w=st('a_conv_w'), a_conv_b=vec('a_conv_b'), a_b_r=vec('b_r'), a_b_i=vec('b_i'), a_lam=vec('lam'),
        b_conv_w=st('b_conv_w'), c_d=vec('c_d'), c_glu_w=st('wglu'), c_glu_b=vec('bglu'),
        w_brT=st('w_brT'), w_out=st('w_out'), xa_norm=vec('xa_norm'), mem_norm=vec('mem_norm'),
        xa_w_q=st('w_q'), w_kvT=st('w_kvT'), xa_w_o=st('w_o'), ffn_norm=vec('ffn_norm'), w_upT=st('w_upT'),
        ffn_conv_w=st('ffn_conv_w'), ffn_conv_b=vec('ffn_conv_b'), ffn_w_down=st('w_down'),
        final_norm=g_final.reshape(-1))
    return loss, dx.reshape(nb, seq, d), out


LANES = 1024
ROW_PAD = 256


def _small_rows(shape):
    return -(-int(np.prod(shape)) // (LANES * SUBLANES)) * SUBLANES


def _pack(vecs, dtype):
    segs = []
    for v in vecs:
        rows = _small_rows(v.shape)
        flat = v.reshape(-1).astype(dtype)
        segs.append(jnp.pad(flat, (0, rows * LANES - flat.shape[0])).reshape(rows, LANES))
    total = sum(s.shape[0] for s in segs)
    tail = -total % ROW_PAD
    if tail:
        segs.append(jnp.zeros((tail, LANES), dtype))
    return jnp.concatenate(segs, axis=0)


BIG_LAYOUT = {'w_in': 'w_inT', 'xa_w_kv': 'w_kvT', 'ffn_w_up': 'w_upT', 'w_branch': 'w_brT'}
SEG_ALIGN = 16


def _to_layout(n, a):
    return jnp.swapaxes(a, -1, -2) if n in BIG_LAYOUT else a


def _seg_rows(shape):
    rows = int(np.prod(shape)) // LANES
    return rows, -(-rows // SEG_ALIGN) * SEG_ALIGN


def _pack_segments(mats, dtype):
    lead = mats[0].ndim - 2
    segs = []
    for m in mats:
        pad = -m.shape[-2] % SEG_ALIGN
        segs.append(jnp.pad(m.astype(dtype), [(0, 0)] * lead + [(0, pad), (0, 0)]))
    total = sum(s.shape[-2] for s in segs)
    tail = -total % ROW_PAD
    if tail:
        segs.append(jnp.zeros(segs[0].shape[:-2] + (tail, LANES), dtype))
    return jnp.concatenate(segs, axis=-2)


def _unpack_segments(buf, shapes):
    lead = buf.shape[:-2]
    out, off = [], 0
    for shp in shapes:
        rows, padded = _seg_rows(shp)
        out.append(buf[..., off:off + rows, :].reshape(lead + tuple(shp)))
        off += padded
    return out


def _unpack(flat, shapes):
    lead = flat.shape[:-2]
    out, off = [], 0
    for shp in shapes:
        cnt, rows = int(np.prod(shp)), _small_rows(shp)
        seg = flat[..., off:off + rows, :].reshape(lead + (-1,))
        out.append(seg[..., :cnt].reshape(lead + tuple(shp)))
        off += rows
    return out


def _join_shards(stacked, axis):
    t = jnp.moveaxis(stacked, 0, axis)
    shp = list(t.shape)
    return t.reshape(shp[:axis] + [shp[axis] * shp[axis + 1]] + shp[axis + 2:])


def _cut_shards(full, axis):
    shp = list(full.shape)
    t = full.reshape(shp[:axis] + [N_DEV, shp[axis] // N_DEV] + shp[axis + 1:])
    return jnp.moveaxis(t, axis, 0).reshape(N_DEV, -1)


def _as2d(a):
    return a.reshape(-1, a.shape[-1])


def kernel(x, mem, mix_norm, w_in, b_gate, a_conv_w, a_conv_b, a_w_r, a_b_r, a_w_i, a_b_i, a_lam, b_conv_w, b_a_log, b_dt_bias, b_norm, c_lam_re, c_lam_im, c_log_dt, c_b_re, c_b_im, c_c_re, c_c_im, c_d, c_glu_w, c_glu_b, w_branch, w_out, xa_norm, mem_norm, xa_w_q, xa_w_kv, xa_w_o, ffn_norm, ffn_w_up, ffn_conv_w, ffn_conv_b, ffn_w_down, final_norm, loss_target, m_mix_norm, m_w_in, m_b_gate, m_a_conv_w, m_a_conv_b, m_a_w_r, m_a_b_r, m_a_w_i, m_a_b_i, m_a_lam, m_b_conv_w, m_b_a_log, m_b_dt_bias, m_b_norm, m_c_lam_re, m_c_lam_im, m_c_log_dt, m_c_b_re, m_c_b_im, m_c_c_re, m_c_c_im, m_c_d, m_c_glu_w, m_c_glu_b, m_w_branch, m_w_out, m_xa_norm, m_mem_norm, m_xa_w_q, m_xa_w_kv, m_xa_w_o, m_ffn_norm, m_ffn_w_up, m_ffn_conv_w, m_ffn_conv_b, m_ffn_w_down, m_final_norm, v_mix_norm, v_w_in, v_b_gate, v_a_conv_w, v_a_conv_b, v_a_w_r, v_a_b_r, v_a_w_i, v_a_b_i, v_a_lam, v_b_conv_w, v_b_a_log, v_b_dt_bias, v_b_norm, v_c_lam_re, v_c_lam_im, v_c_log_dt, v_c_b_re, v_c_b_im, v_c_c_re, v_c_c_im, v_c_d, v_c_glu_w, v_c_glu_b, v_w_branch, v_w_out, v_xa_norm, v_mem_norm, v_xa_w_q, v_xa_w_kv, v_xa_w_o, v_ffn_norm, v_ffn_w_up, v_ffn_conv_w, v_ffn_conv_b, v_ffn_w_down, v_final_norm):
    args = (x, mem, mix_norm, w_in, b_gate, a_conv_w, a_conv_b, a_w_r, a_b_r, a_w_i, a_b_i, a_lam, b_conv_w, b_a_log, b_dt_bias, b_norm, c_lam_re, c_lam_im, c_log_dt, c_b_re, c_b_im, c_c_re, c_c_im, c_d, c_glu_w, c_glu_b, w_branch, w_out, xa_norm, mem_norm, xa_w_q, xa_w_kv, xa_w_o, ffn_norm, ffn_w_up, ffn_conv_w, ffn_conv_b, ffn_w_down, final_norm, loss_target, m_mix_norm, m_w_in, m_b_gate, m_a_conv_w, m_a_conv_b, m_a_w_r, m_a_b_r, m_a_w_i, m_a_b_i, m_a_lam, m_b_conv_w, m_b_a_log, m_b_dt_bias, m_b_norm, m_c_lam_re, m_c_lam_im, m_c_log_dt, m_c_b_re, m_c_b_im, m_c_c_re, m_c_c_im, m_c_d, m_c_glu_w, m_c_glu_b, m_w_branch, m_w_out, m_xa_norm, m_mem_norm, m_xa_w_q, m_xa_w_kv, m_xa_w_o, m_ffn_norm, m_ffn_w_up, m_ffn_conv_w, m_ffn_conv_b, m_ffn_w_down, m_final_norm, v_mix_norm, v_w_in, v_b_gate, v_a_conv_w, v_a_conv_b, v_a_w_r, v_a_b_r, v_a_w_i, v_a_b_i, v_a_lam, v_b_conv_w, v_b_a_log, v_b_dt_bias, v_b_norm, v_c_lam_re, v_c_lam_im, v_c_log_dt, v_c_b_re, v_c_b_im, v_c_c_re, v_c_c_im, v_c_d, v_c_glu_w, v_c_glu_b, v_w_branch, v_w_out, v_xa_norm, v_mem_norm, v_xa_w_q, v_xa_w_kv, v_xa_w_o, v_ffn_norm, v_ffn_w_up, v_ffn_conv_w, v_ffn_conv_b, v_ffn_w_down, v_final_norm)
    nw = len(WEIGHTS)
    x, mem = args[0], args[1]
    w_loc = dict(zip(WEIGHTS, args[2:2 + nw]))
    tgt = args[2 + nw]
    m_loc = dict(zip(WEIGHTS, args[3 + nw:3 + 2 * nw]))
    v_loc = dict(zip(WEIGHTS, args[3 + 2 * nw:3 + 3 * nw]))
    me = 4 * lax.axis_index("x") + 2 * lax.axis_index("y") + lax.axis_index("c")

    lay = {n: _to_layout(n, w_loc[n]) for n in BIG}
    lay_shapes = [lay[n].shape for n in BIG]
    gathered = _gather(_pack_segments([lay[n].reshape(-1, LANES) for n in BIG], BF), name="gather_matmul_weights")
    w = {n: a for n, a in w_loc.items() if n not in BIG}
    for n, st in zip(BIG, _unpack_segments(gathered, lay_shapes)):
        t = jnp.moveaxis(st, 0, -3)
        w[BIG_LAYOUT.get(n, n)] = t.reshape(t.shape[:-3] + (N_DEV * t.shape[-2], t.shape[-1]))
    ss_shapes = [w_loc[n].shape for n in SMALL_SHARDED]
    gathered_s = _gather(_pack([w_loc[n] for n in SMALL_SHARDED], F32), name="gather_conv_weights")
    for n, st in zip(SMALL_SHARDED, _unpack(gathered_s, ss_shapes)):
        w[n] = _join_shards(st, SHARD_AXIS[n])

    final_norm = w.pop('final_norm')
    loss, grad_x, g = local_step(x, mem, tgt, w, final_norm)

    def cut(full):
        t = full.reshape(full.shape[:-2] + (N_DEV, full.shape[-2] // N_DEV, full.shape[-1]))
        return jnp.moveaxis(t, -3, 0).reshape(N_DEV, -1, LANES)

    send = _pack_segments([cut(g[BIG_LAYOUT.get(n, n)]) for n in BIG], BF)
    recv = _exchange(send, scatter=True, name="scatter_matmul_grads")
    g_lay = _unpack_segments(sum_slabs(recv, name="sum_matmul_grads"), lay_shapes)
    g_big = {n: _to_layout(n, a) for n, a in zip(BIG, g_lay)}
    small_full_shapes = [g[n].shape for n in SMALL]
    packed = _pack([g[n] for n in SMALL] + [loss[0, :1]], F32)
    everyones = _gather(packed, name="gather_small_grads")
    summed = _unpack(sum_slabs(everyones, name="sum_small_grads"), small_full_shapes + [(1,)])
    loss_total = summed[-1].reshape(())
    g_small = {}
    for n, full in zip(SMALL, summed[:-1]):
        full = full.reshape(small_full_shapes[SMALL.index(n)])
        if n in SMALL_SHARDED:
            ax = SHARD_AXIS[n]
            loc = w_loc[n].shape[ax]
            full = lax.dynamic_slice_in_dim(full, me * loc, loc, axis=ax)
        g_small[n] = full.reshape(w_loc[n].shape)

    grads, delta, new_m, new_v = {}, {}, {}, {}
    for n in BIG:
        shp = w_loc[n].shape
        grads[n] = g_big[n]
        dl, nm, nv = adamw(_as2d(w_loc[n]), _as2d(g_big[n]), _as2d(m_loc[n]), _as2d(v_loc[n]), name=f"adamw_{n}")
        delta[n], new_m[n], new_v[n] = dl.reshape(shp), nm.reshape(shp), nv.reshape(shp)
    small_shapes = [w_loc[n].shape for n in SMALL]
    flat = [_pack([d[n] for n in SMALL], F32) for d in (w_loc, g_small, m_loc, v_loc)]
    res = adamw(*flat, name="adamw_small")
    for d, r in zip((delta, new_m, new_v), res):
        for n, a in zip(SMALL, _unpack(r, small_shapes)):
            d[n] = a
    grads.update(g_small)
    return (loss_total, grad_x, *[grads[n] for n in WEIGHTS], *[delta[n] for n in WEIGHTS],
            *[new_m[n] for n in WEIGHTS], *[new_v[n] for n in WEIGHTS])
```

```python
import functools
from typing import Any, NamedTuple

import jax
import jax.numpy as jnp
import numpy as np
from jax import lax
from jax.experimental import pallas as pl
from jax.experimental.pallas import tpu as pltpu

F32 = jnp.float32
BF = jnp.bfloat16
MXU_DTYPE = BF

EPS = 1e-6
RG_C = 8.0
N_DEV = 8
DEPTH = 4
D_MODEL = 1024
BW = 512
A_HEADS, A_HD = 8, 64
B_HEADS, B_DK = 4, 128
B_CHUNK = 64
C_GROUPS, C_GROUP, C_STATE = 32, 16, 64
C_CH = C_GROUPS * C_STATE
S5_CW = 512
X_HEADS, X_HD = 4, 256
D_FF = 3 * D_MODEL
ADAM_LR, ADAM_B1, ADAM_B2, ADAM_EPS, ADAM_WD, ADAM_STEP = 0.001, 0.9, 0.999, 1e-08, 0.01, 10

SUBLANES = 8
VMEM_LIMIT = 56 * 1024 * 1024

WEIGHTS = ['mix_norm', 'w_in', 'b_gate', 'a_conv_w', 'a_conv_b', 'a_w_r', 'a_b_r', 'a_w_i', 'a_b_i', 'a_lam',
           'b_conv_w', 'b_a_log', 'b_dt_bias', 'b_norm', 'c_lam_re', 'c_lam_im', 'c_log_dt', 'c_b_re', 'c_b_im',
           'c_c_re', 'c_c_im', 'c_d', 'c_glu_w', 'c_glu_b', 'w_branch', 'w_out', 'xa_norm', 'mem_norm', 'xa_w_q',
           'xa_w_kv', 'xa_w_o', 'ffn_norm', 'ffn_w_up', 'ffn_conv_w', 'ffn_conv_b', 'ffn_w_down', 'final_norm']
SHARD_AXIS = {'w_in': 2, 'a_conv_w': 2, 'b_conv_w': 2, 'c_glu_w': 1, 'w_branch': 3, 'w_out': 1, 'xa_w_q': 1,
              'xa_w_kv': 2, 'xa_w_o': 1, 'ffn_w_up': 2, 'ffn_conv_w': 2, 'ffn_w_down': 1}
BIG = ['w_in', 'c_glu_w', 'w_branch', 'w_out', 'xa_w_q', 'xa_w_kv', 'xa_w_o', 'ffn_w_up', 'ffn_w_down']
SMALL_SHARDED = ['a_conv_w', 'b_conv_w', 'ffn_conv_w']
SMALL = [n for n in WEIGHTS if n not in BIG]


def _dot(x, y, tx, ty):
    cx = 0 if tx else 1
    cy = 1 if ty else 0
    return lax.dot_general(x.astype(MXU_DTYPE), y.astype(MXU_DTYPE), (((cx,), (cy,)), ((), ())),
                           preferred_element_type=F32)


@functools.partial(jax.custom_vjp, nondiff_argnums=(2, 3))
def bmm(a, b, ta=False, tb=False):
    return _dot(a, b, ta, tb)


def _bmm_fwd(a, b, ta, tb):
    return _dot(a, b, ta, tb), (a, b)


def _bmm_bwd(ta, tb, res, g):
    a, b = res
    da = _dot(b, g, tb, True) if ta else _dot(g, b, False, not tb)
    db = _dot(g, a, True, ta) if tb else _dot(a, g, not ta, False)
    return da.astype(a.dtype), db.astype(b.dtype)


bmm.defvjp(_bmm_fwd, _bmm_bwd)


def _dotx(x, y, tx, ty):
    cx = 0 if tx else 1
    cy = 1 if ty else 0

    def d(p, q):
        return lax.dot_general(p, q, (((cx,), (cy,)), ((), ())), preferred_element_type=F32)

    xh, yh = x.astype(BF), y.astype(BF)
    xl, yl = (x - xh.astype(F32)).astype(BF), (y - yh.astype(F32)).astype(BF)
    return d(xh, yh) + (d(xh, yl) + d(xl, yh))


@functools.partial(jax.custom_vjp, nondiff_argnums=(2, 3))
def xmm(a, b, ta=False, tb=False):
    return _dotx(a, b, ta, tb)


def _xmm_fwd(a, b, ta, tb):
    return _dotx(a, b, ta, tb), (a, b)


def _xmm_bwd(ta, tb, res, g):
    a, b = res
    da = _dotx(b, g, tb, True) if ta else _dotx(g, b, False, not tb)
    db = _dotx(g, a, True, ta) if tb else _dotx(a, g, not ta, False)
    return da, db


xmm.defvjp(_xmm_fwd, _xmm_bwd)


@functools.partial(jax.custom_vjp, nondiff_argnums=(1,))
def roll_rows(x, s):
    return pltpu.roll(x, s, 0)


def _roll_rows_fwd(x, s):
    return pltpu.roll(x, s, 0), None


def _roll_rows_bwd(s, _, g):
    return (pltpu.roll(g, (g.shape[0] - s) % g.shape[0], 0),)


roll_rows.defvjp(_roll_rows_fwd, _roll_rows_bwd)


def shift_rows(cur_tail, s):
    cur, tail = cur_tail
    if s == 0:
        return cur
    rolled = roll_rows(cur, s)
    row = lax.broadcasted_iota(jnp.int32, tail.shape, 0)
    top = jnp.where(row < s, roll_rows(tail, s), rolled[:SUBLANES])
    if cur.shape[0] == SUBLANES:
        return top
    return jnp.concatenate([top, rolled[SUBLANES:]], axis=0)


def softplus(x):
    return jnp.maximum(x, 0.0) + jnp.log(1.0 + jnp.exp(-jnp.abs(x)))


def expm1(x):
    series = x * (1.0 + x * (0.5 + x * (1.0 / 6.0 + x * (1.0 / 24.0 + x * (1.0 / 120.0)))))
    return jnp.where(jnp.abs(x) < 0.05, series, jnp.exp(x) - 1.0)


def sigmoid(x):
    return 1.0 / (1.0 + jnp.exp(-x))


def silu(x):
    return x * sigmoid(x)


def gelu(x):
    return 0.5 * x * (1.0 + jnp.tanh(0.7978845608028654 * (x + 0.044715 * (x * x * x))))


def rms(x, g):
    var = jnp.mean(x * x, axis=-1, keepdims=True)
    return x * lax.rsqrt(var + EPS) * g


def cumsum_rows(x):
    n = x.shape[0]
    row = lax.broadcasted_iota(jnp.int32, x.shape, 0)
    s = 1
    while s < n:
        x = x + jnp.where(row >= s, roll_rows(x, s), 0.0)
        s *= 2
    return x


def _pick(n, prefs):
    for p in prefs:
        if n % p == 0:
            return p
    return n


def mm(a, b, *, ta=False, tb=False, a_cols=None, b_cols=None, residual=None, out_dtype=F32, name):
    a0, aw = a_cols if a_cols is not None else (0, a.shape[1])
    b0, bw = b_cols if b_cols is not None else (0, b.shape[1])
    if ta:
        K, M = a.shape[0], aw
    else:
        M, K = a.shape[0], aw
    if tb:
        N, Kb = b.shape[0], bw
    else:
        Kb, N = b.shape[0], bw
    assert K == Kb, (name, a.shape, b.shape, ta, tb)
    tm = _pick(M, (1024, 512, 256, 128))
    tk = _pick(K, (2048, 1024, 3328, 512, 256, 128))
    tn = _pick(N, (1024, 512, 256, 128)) if tk <= 1024 else _pick(N, (512, 256, 128))
    nk = K // tk

    def off(c0, t):
        assert c0 % t == 0, (name, c0, t)
        return c0 // t

    if ta:
        a_spec = pl.BlockSpec((tk, tm), lambda i, j, k, o=off(a0, tm): (k, i + o))
    else:
        a_spec = pl.BlockSpec((tm, tk), lambda i, j, k, o=off(a0, tk): (i, k + o))
    if tb:
        b_spec = pl.BlockSpec((tn, tk), lambda i, j, k, o=off(b0, tk): (j, k + o))
    else:
        b_spec = pl.BlockSpec((tk, tn), lambda i, j, k, o=off(b0, tn): (k, j + o))
    o_spec = pl.BlockSpec((tm, tn), lambda i, j, k: (i, j))
    in_specs = [a_spec, b_spec]
    args = [a, b]
    if residual is not None:
        in_specs.append(o_spec)
        args.append(residual)

    def body(*refs):
        a_ref, b_ref = refs[0], refs[1]
        r_ref = refs[2] if residual is not None else None
        o_ref = refs[3] if residual is not None else refs[2]
        part = _dot(a_ref[...], b_ref[...], ta, tb)

        def finish(acc):
            if r_ref is not None:
                acc = acc + r_ref[...].astype(F32)
            o_ref[...] = acc.astype(out_dtype)

        if nk == 1:
            finish(part)
        else:
            acc_ref = refs[-1]
            k = pl.program_id(2)

            @pl.when(k == 0)
            def _():
                acc_ref[...] = part

            @pl.when(k > 0)
            def _():
                acc_ref[...] += part

            @pl.when(k == nk - 1)
            def _():
                finish(acc_ref[...])

    return pl.pallas_call(
        body, name=name, grid=(M // tm, N // tn, nk),
        in_specs=in_specs, out_specs=o_spec,
        out_shape=jax.ShapeDtypeStruct((M, N), out_dtype),
        scratch_shapes=[pltpu.VMEM((tm, tn), F32)] if nk > 1 else [],
        compiler_params=pltpu.CompilerParams(dimension_semantics=("parallel", "parallel", "arbitrary"),
                                             vmem_limit_bytes=VMEM_LIMIT),
    )(*args)


class Tok(NamedTuple):
    arr: Any
    width: int
    col: int = 0
    halo: bool = False
    grad: Any = None
    add: Any = None


class Par(NamedTuple):
    arr: Any
    kind: str = 'const'
    col: int = 0
    width: int = 0
    grad: bool = True


def _tok_specs(toks, tb, rev, ntile):
    specs, args = [], []
    for t in toks:
        if rev:
            cur = lambda j, s, c=t.col: (ntile - 1 - s, c + j)
            tail = lambda j, s, c=t.col: (jnp.maximum((ntile - 1 - s) * (tb // SUBLANES) - 1, 0), c + j)
        else:
            cur = lambda j, s, c=t.col: (s, c + j)
            tail = lambda j, s, c=t.col: (jnp.maximum(s * (tb // SUBLANES) - 1, 0), c + j)
        specs.append(pl.BlockSpec((tb, t.width), cur))
        args.append(t.arr)
        if t.halo:
            specs.append(pl.BlockSpec((SUBLANES, t.width), tail))
            args.append(t.arr)
    return specs, args


def _par_specs(pars, tpb, rev, ntile):
    specs, args = [], []
    for p in pars:
        if p.kind == 'const':
            specs.append(pl.BlockSpec(p.arr.shape, lambda j, s: (0, 0)))
        elif p.kind == 'col':
            specs.append(pl.BlockSpec((p.arr.shape[0], p.width), lambda j, s, c=p.col: (0, c + j)))
        else:
            if rev:
                specs.append(pl.BlockSpec((None,) + p.arr.shape[1:], lambda j, s: ((ntile - 1 - s) // tpb, 0, 0)))
            else:
                specs.append(pl.BlockSpec((None,) + p.arr.shape[1:], lambda j, s: (s // tpb, 0, 0)))
        args.append(p.arr)
    return specs, args


def _rows(ref, r0, n):
    return ref[...] if isinstance(r0, int) else ref[pl.ds(r0, n), :]


def _read_toks(toks, refs, t0, r0, sub):
    vals, k = [], 0
    for t in toks:
        ref = refs[k]
        k += 1
        cur = _rows(ref, r0, sub)
        if t.halo:
            tail = jnp.where(t0 == 0, jnp.zeros_like(refs[k][...]), refs[k][...])
            k += 1
            if not isinstance(r0, int):
                before = ref[pl.ds(pl.multiple_of(jnp.maximum(r0 - SUBLANES, 0), SUBLANES), SUBLANES), :]
                tail = jnp.where(r0 == 0, tail, before)
            vals.append((cur, tail))
        else:
            vals.append(cur)
    return vals, k


def _row_blocks(tb, sub, reverse, block):
    if sub is None or sub >= tb:
        block(0)
        return
    nsub = tb // sub

    def step(n, carry):
        r = nsub - 1 - n if reverse else n
        block(pl.multiple_of(r * sub, sub))
        return carry

    lax.fori_loop(0, nsub, step, 0)


def stage_fwd(fn, toks, pars, outs, *, seq, tb, ncol=1, sub=None, name):
    T = toks[0].arr.shape[0]
    tb = min(tb, seq)
    ntile, tpb = T // tb, seq // tb
    tspecs, targs = _tok_specs(toks, tb, False, ntile)
    pspecs, pargs = _par_specs(pars, tpb, False, ntile)
    n_in = len(tspecs) + len(pspecs)

    def body(*refs):
        s = pl.program_id(1)
        t0 = (s % tpb) * tb
        pvals = [r[...] for r in refs[len(tspecs):n_in]]

        def block(r0):
            tvals, _ = _read_toks(toks, refs, t0, r0, sub)
            res = fn(t0 + r0, *tvals, *pvals)
            for r, v in zip(refs[n_in:], res):
                if isinstance(r0, int):
                    r[...] = v.astype(r.dtype)
                else:
                    r[pl.ds(r0, sub), :] = v.astype(r.dtype)

        _row_blocks(tb, sub, False, block)

    return pl.pallas_call(
        body, name=name, grid=(ncol, ntile),
        in_specs=tspecs + pspecs,
        out_specs=[pl.BlockSpec((tb, w), lambda j, s: (s, j)) for w, _ in outs],
        out_shape=[jax.ShapeDtypeStruct((T, w * ncol), d) for w, d in outs],
        compiler_params=pltpu.CompilerParams(dimension_semantics=("arbitrary", "arbitrary"),
                                             vmem_limit_bytes=VMEM_LIMIT),
    )(*targs, *pargs)


def stage_bwd(fn, toks, pars, cots, *, cot_fn=None, seq, tb, ncol=1, sub=None, name):
    T = toks[0].arr.shape[0]
    tb = min(tb, seq)
    sub = SUB_ROWS.get(fn) if sub is None else sub
    ntile, tpb = T // tb, seq // tb
    tspecs, targs = _tok_specs(toks, tb, True, ntile)
    pspecs, pargs = _par_specs(pars, tpb, True, ntile)
    cspecs, cargs = _tok_specs(cots, tb, True, ntile)
    adds = [t for t in toks if t.add is not None]
    assert all(t.grad is not None for t in adds)
    aspecs = [pl.BlockSpec((tb, t.width), lambda j, s: (ntile - 1 - s, j)) for t in adds]
    aargs = [t.add for t in adds]
    n_t, n_p, n_c = len(tspecs), len(pspecs), len(cspecs)

    gtoks = [t for t in toks if t.grad is not None]
    gpars = [p for p in pars if p.grad]
    out_specs, out_shape = [], []
    for t in gtoks:
        out_specs.append(pl.BlockSpec((tb, t.width), lambda j, s: (ntile - 1 - s, j)))
        out_shape.append(jax.ShapeDtypeStruct((T, t.width * ncol), t.grad))
    for p in gpars:
        if p.kind == 'const':
            out_specs.append(pl.BlockSpec(p.arr.shape, lambda j, s: (0, 0)))
            out_shape.append(jax.ShapeDtypeStruct(p.arr.shape, F32))
        elif p.kind == 'col':
            out_specs.append(pl.BlockSpec((p.arr.shape[0], p.width), lambda j, s: (0, j)))
            out_shape.append(jax.ShapeDtypeStruct((p.arr.shape[0], p.width * ncol), F32))
        else:
            out_specs.append(pl.BlockSpec((None,) + p.arr.shape[1:], lambda j, s: ((ntile - 1 - s) // tpb, 0, 0)))
            out_shape.append(jax.ShapeDtypeStruct(p.arr.shape, F32))
    carries = [t for t in gtoks if t.halo]
    scratch = [pltpu.VMEM((SUBLANES, t.width), F32) for t in carries]

    def body(*refs):
        j, s = pl.program_id(0), pl.program_id(1)
        i = ntile - 1 - s
        t0 = (i % tpb) * tb
        t_refs = refs[:n_t]
        p_refs = refs[n_t:n_t + n_p]
        c_refs = refs[n_t + n_p:n_t + n_p + n_c]
        a_refs = refs[n_t + n_p + n_c:n_t + n_p + n_c + len(adds)]
        o_refs = refs[n_t + n_p + n_c + len(adds):]
        gt_refs = o_refs[:len(gtoks)]
        gp_refs = o_refs[len(gtoks):len(gtoks) + len(gpars)]
        carry_refs = o_refs[len(gtoks) + len(gpars):]

        pvals = [r[...] for r in p_refs]

        @pl.when(s == 0)
        def _():
            for carry in carry_refs:
                carry[...] = jnp.zeros_like(carry)

        for p, ref in zip(gpars, gp_refs):
            if p.kind == 'const':
                first = jnp.logical_and(j == 0, s == 0)
            elif p.kind == 'col':
                first = s == 0
            else:
                first = s % tpb == 0

            @pl.when(first)
            def _(ref=ref):
                ref[...] = jnp.zeros_like(ref)

        def block(r0):
            rows = tb if isinstance(r0, int) else sub
            tt = t0 + r0
            tvals, _ = _read_toks(toks, t_refs, t0, r0, sub)
            cvals, _ = _read_toks(cots, c_refs, t0, r0, sub)

            def f(tv, pv):
                return tuple(fn(tt, *tv, *pv))

            res, vjp = jax.vjp(f, tvals, pvals)
            ct = cot_fn(tt, *cvals) if cot_fn is not None else tuple(cvals)
            ct = tuple(c.astype(r.dtype) for c, r in zip(ct, res))
            dt, dp = vjp(ct)

            ci = 0
            ai = 0
            gi = 0
            for t, d in zip(toks, dt):
                if t.grad is None:
                    continue
                ref = gt_refs[gi]
                gi += 1
                if t.halo:
                    dcur, dtail = d
                    carry = carry_refs[ci]
                    ci += 1
                    top = dcur[:rows - SUBLANES] if rows > SUBLANES else None
                    bot = dcur[rows - SUBLANES:] + carry[...]
                    dcur = bot if top is None else jnp.concatenate([top, bot], axis=0)
                    carry[...] = jnp.where(tt == 0, jnp.zeros_like(dtail), dtail)
                else:
                    dcur = d
                if t.add is not None:
                    dcur = dcur + _rows(a_refs[ai], r0, rows).astype(F32)
                    ai += 1
                if isinstance(r0, int):
                    ref[...] = dcur.astype(ref.dtype)
                else:
                    ref[pl.ds(r0, rows), :] = dcur.astype(ref.dtype)

            gi = 0
            for p, d in zip(pars, dp):
                if not p.grad:
                    continue
                gp_refs[gi][...] += d.astype(F32)
                gi += 1

        _row_blocks(tb, sub, True, block)

    res = pl.pallas_call(
        body, name=name, grid=(ncol, ntile),
        in_specs=tspecs + pspecs + cspecs + aspecs,
        out_specs=out_specs, out_shape=out_shape, scratch_shapes=scratch,
        compiler_params=pltpu.CompilerParams(dimension_semantics=("arbitrary", "arbitrary"),
                                             vmem_limit_bytes=VMEM_LIMIT),
    )(*targs, *pargs, *cargs, *aargs)
    return list(res[:len(gtoks)]), list(res[len(gtoks):])


def _bcast_row(x, r):
    return jnp.broadcast_to(x[r:r + 1, :], x.shape)


SCAN_TB = 512


def scan_real(a, b, *, seq, reverse, name):
    T, C = a.shape
    tb = min(SCAN_TB, seq)
    nb, nt, nblk = T // seq, seq // tb, tb // SUBLANES

    def body(a_ref, b_ref, h_ref, carry_h, carry_a):
        @pl.when(pl.program_id(1) == 0)
        def _():
            carry_h[...] = jnp.zeros_like(carry_h)
            carry_a[...] = jnp.zeros_like(carry_a)

        row = lax.broadcasted_iota(jnp.int32, (SUBLANES, C), 0)

        def blk(n, c):
            ch, ca = c
            k = nblk - 1 - n if reverse else n
            o = pl.multiple_of(k * SUBLANES, SUBLANES)
            A = a_ref[pl.ds(o, SUBLANES), :]
            B = b_ref[pl.ds(o, SUBLANES), :]
            if reverse:
                a_first = _bcast_row(A, 0)
                A = jnp.where(row == SUBLANES - 1, ca, pltpu.roll(A, SUBLANES - 1, 0))
                for s in (1, 2, 4):
                    keep = row < SUBLANES - s
                    Bs = jnp.where(keep, pltpu.roll(B, SUBLANES - s, 0), 0.0)
                    As = jnp.where(keep, pltpu.roll(A, SUBLANES - s, 0), 1.0)
                    B = B + A * Bs
                    A = A * As
                h = B + A * ch
                h_ref[pl.ds(o, SUBLANES), :] = h
                return _bcast_row(h, 0), a_first
            for s in (1, 2, 4):
                keep = row >= s
                Bs = jnp.where(keep, pltpu.roll(B, s, 0), 0.0)
                As = jnp.where(keep, pltpu.roll(A, s, 0), 1.0)
                B = B + A * Bs
                A = A * As
            h = B + A * ch
            h_ref[pl.ds(o, SUBLANES), :] = h
            return _bcast_row(h, SUBLANES - 1), ca

        ch, ca = lax.fori_loop(0, nblk, blk, (carry_h[...], carry_a[...]))
        carry_h[...] = ch
        carry_a[...] = ca

    if reverse:
        spec = pl.BlockSpec((tb, C), lambda bi, i: (bi * nt + nt - 1 - i, 0))
    else:
        spec = pl.BlockSpec((tb, C), lambda bi, i: (bi * nt + i, 0))
    return pl.pallas_call(
        body, name=name, grid=(nb, nt), in_specs=[spec, spec], out_specs=spec,
        out_shape=jax.ShapeDtypeStruct((T, C), F32),
        scratch_shapes=[pltpu.VMEM((SUBLANES, C), F32), pltpu.VMEM((SUBLANES, C), F32)],
        compiler_params=pltpu.CompilerParams(dimension_semantics=("arbitrary", "arbitrary"),
                                             vmem_limit_bytes=VMEM_LIMIT),
    )(a, b)


def scan_cplx(ar, ai, bu, *, seq, reverse, name):
    T = bu.shape[0]
    cw = S5_CW
    ncol = C_CH // cw
    tb = min(SCAN_TB, seq)
    nb, nt, nblk = T // seq, seq // tb, tb // SUBLANES

    def body(ar_ref, ai_ref, b_ref, h_ref, carry_r, carry_i):
        @pl.when(pl.program_id(2) == 0)
        def _():
            carry_r[...] = jnp.zeros_like(carry_r)
            carry_i[...] = jnp.zeros_like(carry_i)

        row = lax.broadcasted_iota(jnp.int32, (SUBLANES, cw), 0)
        Ar = jnp.broadcast_to(ar_ref[...], (SUBLANES, cw))
        Ai = jnp.broadcast_to(ai_ref[...], (SUBLANES, cw))
        levels = []
        for s in (1, 2, 4):
            keep = (row < SUBLANES - s) if reverse else (row >= s)
            sh = SUBLANES - s if reverse else s
            levels.append((Ar, Ai, keep, sh))
            Asr = jnp.where(keep, pltpu.roll(Ar, sh, 0), 1.0)
            Asi = jnp.where(keep, pltpu.roll(Ai, sh, 0), 0.0)
            Ar, Ai = Ar * Asr - Ai * Asi, Ar * Asi + Ai * Asr

        def blk(n, c):
            cr, ci = c
            k = nblk - 1 - n if reverse else n
            o = pl.multiple_of(k * SUBLANES, SUBLANES)
            Br = b_ref[pl.ds(o, SUBLANES), :cw]
            Bi = b_ref[pl.ds(o, SUBLANES), cw:]
            for lr, li, keep, sh in levels:
                Bsr = jnp.where(keep, pltpu.roll(Br, sh, 0), 0.0)
                Bsi = jnp.where(keep, pltpu.roll(Bi, sh, 0), 0.0)
                Br, Bi = Br + lr * Bsr - li * Bsi, Bi + lr * Bsi + li * Bsr
            hr = Br + Ar * cr - Ai * ci
            hi = Bi + Ar * ci + Ai * cr
            h_ref[pl.ds(o, SUBLANES), :cw] = hr
            h_ref[pl.ds(o, SUBLANES), cw:] = hi
            last = 0 if reverse else SUBLANES - 1
            return _bcast_row(hr, last), _bcast_row(hi, last)

        cr, ci = lax.fori_loop(0, nblk, blk, (carry_r[...], carry_i[...]))
        carry_r[...] = cr
        carry_i[...] = ci

    if reverse:
        spec = pl.BlockSpec((tb, 2 * cw), lambda bi, j, i: (bi * nt + nt - 1 - i, j))
    else:
        spec = pl.BlockSpec((tb, 2 * cw), lambda bi, j, i: (bi * nt + i, j))
    aspec = pl.BlockSpec((1, cw), lambda bi, j, i: (0, j))
    return pl.pallas_call(
        body, name=name, grid=(nb, ncol, nt), in_specs=[aspec, aspec, spec], out_specs=spec,
        out_shape=jax.ShapeDtypeStruct((T, 2 * C_CH), F32),
        scratch_shapes=[pltpu.VMEM((SUBLANES, cw), F32), pltpu.VMEM((SUBLANES, cw), F32)],
        compiler_params=pltpu.CompilerParams(dimension_semantics=("arbitrary", "arbitrary", "arbitrary"),
                                             vmem_limit_bytes=VMEM_LIMIT),
    )(ar, ai, bu)


def _col_sums(x):
    ones = jnp.ones((x.shape[0], x.shape[0]), BF)
    acc, rest = None, x
    for _ in range(3):
        piece = rest.astype(BF)
        rest = rest - piece.astype(F32)
        term = lax.dot_general(ones, piece, (((1,), (0,)), ((), ())), preferred_element_type=F32)
        acc = term if acc is None else acc + term
    return acc


@jax.custom_vjp
def col_sums(x):
    return _col_sums(x)


col_sums.defvjp(lambda x: (_col_sums(x), None), lambda _, g: (_col_sums(g),))


def tri_inv(As):
    c = As[0].shape[0]
    ii = lax.broadcasted_iota(jnp.int32, (c, c), 0)
    jj = lax.broadcasted_iota(jnp.int32, (c, c), 1)
    eye = jnp.where(ii == jj, 1.0, 0.0)
    T = [eye - a for a in As]
    P = [_dotx(a, a, False, False) for a in As]
    for _ in range(4):
        both = [_dotx(jnp.concatenate([t, p], axis=0), p, False, False) for t, p in zip(T, P)]
        T = [t + b[:c] for t, b in zip(T, both)]
        P = [b[c:] for b in both]
    return [t + _dotx(t, p, False, False) for t, p in zip(T, P)]


@jax.custom_vjp
def tri_inv_saved(a, t):
    return t


tri_inv_saved.defvjp(lambda a, t: (t, t),
                     lambda t, g: (-_dotx(_dotx(t, g, True, False), t, False, True), jnp.zeros_like(t)))


def delta_chunks(ins, tinvs=None):
    c = B_CHUNK
    Q, K, V, BB, GB, S = (list(t) for t in zip(*ins))
    n = range(len(ins))
    ii = lax.broadcasted_iota(jnp.int32, (c, c), 0)
    jj = lax.broadcasted_iota(jnp.int32, (c, c), 1)
    incl, strict, diag = ii >= jj, ii > jj, ii == jj
    qc = [q * (B_DK ** -0.5) for q in Q]
    gc = [cumsum_rows(g) for g in GB]
    gcol = [x[:, :c] for x in gc]
    grow = [col_sums(jnp.where(diag, x, 0.0)) for x in gcol]
    decay = [jnp.exp(jnp.where(incl, a - b, -1e30)) for a, b in zip(gcol, grow)]
    kb = [k * b for k, b in zip(K, BB)]
    kk = [bmm(kb[i], K[i], False, True) for i in n]
    a_mat = [jnp.where(strict, kk[i] * decay[i], 0.0) for i in n]
    tinv = tri_inv(a_mat) if tinvs is None else [tri_inv_saved(a, t) for a, t in zip(a_mat, tinvs)]
    eg = [jnp.exp(x) for x in gc]
    sol = [xmm(tinv[i], jnp.concatenate([V[i] * BB[i], kb[i] * eg[i]], axis=-1)) for i in n]
    qkr = [bmm(qc[i], K[i], False, True) for i in n]
    qk = [jnp.where(incl, qkr[i] * decay[i], 0.0) for i in n]
    glast = [x[c - 1:c, :] for x in gc]
    k_dec = [K[i] * jnp.exp(glast[i] - gc[i]) for i in n]
    ws = [bmm(sol[i][:, B_DK:], S[i]) for i in n]
    v_new = [sol[i][:, :B_DK] - ws[i] for i in n]
    o1 = [bmm(qc[i] * eg[i], S[i]) for i in n]
    o2 = [bmm(qk[i], v_new[i]) for i in n]
    kv = [bmm(k_dec[i], v_new[i], True, False) for i in n]
    o = [o1[i] + o2[i] for i in n]
    s_new = [S[i] * jnp.exp(glast[i]) + kv[i] for i in n]
    return o, s_new, tinv


def delta_fwd(q, k, v, bB, gB, *, name):
    nb, seq, _ = q.shape
    n = seq // B_CHUNK
    hd = B_DK

    def body(q_ref, k_ref, v_ref, b_ref, g_ref, o_ref, st_ref, ti_ref, state):
        @pl.when(pl.program_id(0) == 0)
        def _():
            state[...] = jnp.zeros_like(state)

        pairs = [(b, h) for b in range(nb) for h in range(B_HEADS)]
        sls = [slice(h * hd, (h + 1) * hd) for _, h in pairs]
        ins = [(q_ref[b, :, sl], k_ref[b, :, sl], v_ref[b, :, sl], b_ref[b, :, sl], g_ref[b, :, sl],
                state[b * B_HEADS + h]) for (b, h), sl in zip(pairs, sls)]
        os, s_news, tinvs = delta_chunks(ins)
        for (b, h), sl, a, o, s_new, tinv in zip(pairs, sls, ins, os, s_news, tinvs):
            st_ref[b, h] = a[5]
            o_ref[b, :, sl] = o
            ti_ref[b, h] = tinv
            state[b * B_HEADS + h] = s_new

    spec = pl.BlockSpec((nb, B_CHUNK, BW), lambda i: (0, i, 0))
    return pl.pallas_call(
        body, name=name, grid=(n,), in_specs=[spec] * 5,
        out_specs=[spec, pl.BlockSpec((nb, None, B_HEADS, hd, hd), lambda i: (0, i, 0, 0, 0)),
                   pl.BlockSpec((nb, None, B_HEADS, B_CHUNK, B_CHUNK), lambda i: (0, i, 0, 0, 0))],
        out_shape=[jax.ShapeDtypeStruct((nb, seq, BW), F32), jax.ShapeDtypeStruct((nb, n, B_HEADS, hd, hd), F32),
                   jax.ShapeDtypeStruct((nb, n, B_HEADS, B_CHUNK, B_CHUNK), F32)],
        scratch_shapes=[pltpu.VMEM((nb * B_HEADS, hd, hd), F32)],
        compiler_params=pltpu.CompilerParams(dimension_semantics=("arbitrary",), vmem_limit_bytes=VMEM_LIMIT),
    )(q, k, v, bB, gB)


def delta_bwd(q, k, v, bB, gB, states, tinvs, do, *, name):
    nb, seq, _ = q.shape
    n = seq // B_CHUNK
    hd = B_DK

    def body(q_ref, k_ref, v_ref, b_ref, g_ref, st_ref, ti_ref, do_ref, dq_ref, dk_ref, dv_ref, db_ref, dg_ref, dstate):
        @pl.when(pl.program_id(0) == 0)
        def _():
            dstate[...] = jnp.zeros_like(dstate)

        pairs = [(b, h) for b in range(nb) for h in range(B_HEADS)]
        sls = [slice(h * hd, (h + 1) * hd) for _, h in pairs]
        ins = [(q_ref[b, :, sl], k_ref[b, :, sl], v_ref[b, :, sl], b_ref[b, :, sl], g_ref[b, :, sl], st_ref[b, h])
               for (b, h), sl in zip(pairs, sls)]
        saved = [ti_ref[b, h] for b, h in pairs]
        d_o = [do_ref[b, :, sl] for (b, h), sl in zip(pairs, sls)]
        d_s = [dstate[b * B_HEADS + h] for b, h in pairs]

        def f(xs):
            return tuple(delta_chunks(xs, saved)[:2])

        grads, = jax.vjp(f, ins)[1]((d_o, d_s))
        for (b, h), sl, (dq, dk, dv, db, dg, ds) in zip(pairs, sls, grads):
            dq_ref[b, :, sl] = dq
            dk_ref[b, :, sl] = dk
            dv_ref[b, :, sl] = dv
            db_ref[b, :, sl] = db
            dg_ref[b, :, sl] = dg
            dstate[b * B_HEADS + h] = ds

    spec = pl.BlockSpec((nb, B_CHUNK, BW), lambda i: (0, n - 1 - i, 0))
    sspec = pl.BlockSpec((nb, None, B_HEADS, hd, hd), lambda i: (0, n - 1 - i, 0, 0, 0))
    tspec = pl.BlockSpec((nb, None, B_HEADS, B_CHUNK, B_CHUNK), lambda i: (0, n - 1 - i, 0, 0, 0))
    return pl.pallas_call(
        body, name=name, grid=(n,), in_specs=[spec] * 5 + [sspec, tspec, spec],
        out_specs=[spec] * 5, out_shape=[jax.ShapeDtypeStruct((nb, seq, BW), F32)] * 5,
        scratch_shapes=[pltpu.VMEM((nb * B_HEADS, hd, hd), F32)],
        compiler_params=pltpu.CompilerParams(dimension_semantics=("arbitrary",), vmem_limit_bytes=VMEM_LIMIT),
    )(q, k, v, bB, gB, states, tinvs, do)


def loss_head(x, tgt, g, *, tb, name):
    T, D = x.shape
    tb = min(tb, T)
    nt = T // tb

    def body(x_ref, t_ref, g_ref, l_ref, dx_ref, dg_ref):
        tg = t_ref[...]

        def f(xv, gv):
            err = rms(xv, gv) - tg
            return 0.5 * jnp.mean(err * err, axis=-1, keepdims=True)

        rows, vjp = jax.vjp(f, x_ref[...], g_ref[...])
        dx, dg = vjp(jnp.ones_like(rows))
        dx_ref[...] = dx
        tot = jnp.broadcast_to(jnp.sum(rows, axis=0, keepdims=True), (1, 128))

        @pl.when(pl.program_id(0) == 0)
        def _():
            l_ref[...] = tot
            dg_ref[...] = dg

        @pl.when(pl.program_id(0) > 0)
        def _():
            l_ref[...] += tot
            dg_ref[...] += dg

    tok = pl.BlockSpec((tb, D), lambda i: (i, 0))
    return pl.pallas_call(
        body, name=name, grid=(nt,),
        in_specs=[tok, tok, pl.BlockSpec((1, D), lambda i: (0, 0))],
        out_specs=[pl.BlockSpec((1, 128), lambda i: (0, 0)), tok, pl.BlockSpec((1, D), lambda i: (0, 0))],
        out_shape=[jax.ShapeDtypeStruct((1, 128), F32), jax.ShapeDtypeStruct((T, D), F32), jax.ShapeDtypeStruct((1, D), F32)],
        compiler_params=pltpu.CompilerParams(dimension_semantics=("arbitrary",), vmem_limit_bytes=VMEM_LIMIT),
    )(x, tgt, g)


def _row_block(rows, cols):
    budget = 256 * 1024
    tr = max(SUBLANES, min(rows, budget // max(cols, 1)) // SUBLANES * SUBLANES)
    while rows % tr:
        tr -= SUBLANES
        if tr <= 0:
            return rows
    return tr


def adamw(w, g, m, v, *, name):
    R, C = w.shape
    tr = _row_block(R, C)
    c1 = 1.0 / (1.0 - ADAM_B1 ** ADAM_STEP)
    c2 = 1.0 / (1.0 - ADAM_B2 ** ADAM_STEP)

    def body(w_ref, g_ref, m_ref, v_ref, d_ref, mo_ref, vo_ref):
        gv = g_ref[...]
        mn = ADAM_B1 * m_ref[...] + (1.0 - ADAM_B1) * gv
        vn = ADAM_B2 * v_ref[...] + (1.0 - ADAM_B2) * (gv * gv)
        d_ref[...] = -ADAM_LR * ((mn * c1) / (jnp.sqrt(vn * c2) + ADAM_EPS) + ADAM_WD * w_ref[...])
        mo_ref[...] = mn
        vo_ref[...] = vn

    spec = pl.BlockSpec((tr, C), lambda i: (i, 0))
    return pl.pallas_call(
        body, name=name, grid=(R // tr,), in_specs=[spec] * 4, out_specs=[spec] * 3,
        out_shape=[jax.ShapeDtypeStruct((R, C), F32)] * 3,
        compiler_params=pltpu.CompilerParams(dimension_semantics=("parallel",), vmem_limit_bytes=VMEM_LIMIT),
    )(w, g, m, v)


def sum_slabs(x, *, name):
    n, R, C = x.shape
    tr = _row_block(R, C * 2)

    def body(x_ref, o_ref):
        acc = x_ref[0].astype(F32)
        for d in range(1, n):
            acc = acc + x_ref[d].astype(F32)
        o_ref[...] = acc

    return pl.pallas_call(
        body, name=name, grid=(R // tr,),
        in_specs=[pl.BlockSpec((n, tr, C), lambda i: (0, i, 0))],
        out_specs=pl.BlockSpec((tr, C), lambda i: (i, 0)),
        out_shape=jax.ShapeDtypeStruct((R, C), F32),
        compiler_params=pltpu.CompilerParams(dimension_semantics=("parallel",), vmem_limit_bytes=VMEM_LIMIT),
    )(x)


def _gather(src, *, name):
    R, C = src.shape

    def body(src_ref, out_ref, send_sems, recv_sems, local_sem):
        x, y, c = lax.axis_index("x"), lax.axis_index("y"), lax.axis_index("c")
        me, sibling = (x, y, c), (x, y, 1 - c)
        chips = [(1 - x, y), (x, 1 - y), (1 - x, 1 - y)]

        def slab(px, py, pc):
            return out_ref.at[4 * px + 2 * py + pc]

        def copy(k, block, to, first_hand=False):
            return pltpu.make_async_remote_copy(
                src_ref=src_ref if first_hand else slab(*block), dst_ref=slab(*block),
                send_sem=send_sems.at[k], recv_sem=recv_sems.at[k],
                device_id=to, device_id_type=pl.DeviceIdType.MESH)

        mine = pltpu.make_async_copy(src_ref, slab(*me), local_sem)
        mine.start()
        first = [copy(0, me, sibling, True)] + [copy(1 + j, me, (*chip, c), True) for j, chip in enumerate(chips)]
        for cp in first:
            cp.start()
        passed = [copy(4 + j, (*chip, c), sibling) for j, chip in enumerate(chips)]
        for j, chip in enumerate(chips):
            copy(1 + j, (*chip, c), me).wait_recv()
            passed[j].start()
        copy(0, sibling, me).wait_recv()
        for j, chip in enumerate(chips):
            copy(4 + j, (*chip, 1 - c), me).wait_recv()
        for cp in first + passed:
            cp.wait_send()
        mine.wait()

    return pl.pallas_call(
        body, name=name,
        in_specs=[pl.BlockSpec(memory_space=pl.ANY)],
        out_specs=pl.BlockSpec(memory_space=pl.ANY),
        out_shape=jax.ShapeDtypeStruct((N_DEV, R, C), src.dtype),
        scratch_shapes=[pltpu.SemaphoreType.DMA((N_DEV - 1,)), pltpu.SemaphoreType.DMA((N_DEV - 1,)),
                        pltpu.SemaphoreType.DMA],
    )(src)


def _scatter_sum(send, *, name):
    n, R, C = send.shape
    nchip = N_DEV // 2

    def sibling_body(send_ref, out_ref, send_sems, recv_sems):
        x, y, c = lax.axis_index("x"), lax.axis_index("y"), lax.axis_index("c")
        copies = [pltpu.make_async_remote_copy(
            src_ref=send_ref.at[2 * j + (1 - c)], dst_ref=out_ref.at[j],
            send_sem=send_sems.at[j], recv_sem=recv_sems.at[j],
            device_id=(x, y, 1 - c), device_id_type=pl.DeviceIdType.MESH) for j in range(nchip)]
        for cp in copies:
            cp.start()
        for cp in copies:
            cp.wait_recv()
        for cp in copies:
            cp.wait_send()

    from_sibling = pl.pallas_call(
        sibling_body, name=name + "_d2d",
        in_specs=[pl.BlockSpec(memory_space=pl.ANY)], out_specs=pl.BlockSpec(memory_space=pl.ANY),
        out_shape=jax.ShapeDtypeStruct((nchip, R, C), send.dtype),
        scratch_shapes=[pltpu.SemaphoreType.DMA((nchip,)), pltpu.SemaphoreType.DMA((nchip,))],
    )(send)
    own = lax.dynamic_index_in_dim(send.reshape(nchip, 2, R, C), lax.axis_index("c"), axis=1, keepdims=False)

    tr = _row_block(R, C * nchip)

    def pair_body(a_ref, b_ref, o_ref):
        o_ref[...] = (a_ref[...].astype(F32) + b_ref[...].astype(F32)).astype(o_ref.dtype)

    spec = pl.BlockSpec((nchip, tr, C), lambda i: (0, i, 0))
    pair = pl.pallas_call(
        pair_body, name=name + "_pair_sum", grid=(R // tr,), in_specs=[spec, spec], out_specs=spec,
        out_shape=jax.ShapeDtypeStruct((nchip, R, C), send.dtype),
        compiler_params=pltpu.CompilerParams(dimension_semantics=("parallel",), vmem_limit_bytes=VMEM_LIMIT),
    )(own, from_sibling)

    def chips_body(src_ref, out_ref, send_sems, recv_sems, local_sem):
        x, y, c = lax.axis_index("x"), lax.axis_index("y"), lax.axis_index("c")
        mine = 2 * x + y
        local = pltpu.make_async_copy(src_ref.at[mine], out_ref.at[mine], local_sem)
        local.start()

        def copy(k, src_slab, dst_slab):
            return pltpu.make_async_remote_copy(
                src_ref=src_ref.at[src_slab], dst_ref=out_ref.at[dst_slab],
                send_sem=send_sems.at[k - 1], recv_sem=recv_sems.at[k - 1],
                device_id=(x ^ (k >> 1), y ^ (k & 1), c), device_id_type=pl.DeviceIdType.MESH)

        sends = [copy(k, mine ^ k, mine) for k in range(1, nchip)]
        for cp in sends:
            cp.start()
        for k in range(1, nchip):
            copy(k, mine ^ k, mine ^ k).wait_recv()
        for cp in sends:
            cp.wait_send()
        local.wait()

    from_chips = pl.pallas_call(
        chips_body, name=name + "_ici",
        in_specs=[pl.BlockSpec(memory_space=pl.ANY)], out_specs=pl.BlockSpec(memory_space=pl.ANY),
        out_shape=jax.ShapeDtypeStruct((nchip, R, C), send.dtype),
        scratch_shapes=[pltpu.SemaphoreType.DMA((nchip - 1,)), pltpu.SemaphoreType.DMA((nchip - 1,)),
                        pltpu.SemaphoreType.DMA],
    )(pair)
    return sum_slabs(from_chips, name=name + "_sum")


TB = 256


def st_norm(t0, x, g):
    return (rms(x.astype(F32), g),)


def _conv(xt, w, bias=None):
    kk = w.shape[0]
    acc = bias
    for i in range(kk):
        term = w[i:i + 1, :] * shift_rows(xt, kk - 1 - i)
        acc = term if acc is None else acc + term
    return acc


def st_a(t0, xa, cw, cb, wr, br, wi, bi, lam):
    xc = _conv(xa, cw, cb)
    r = sigmoid(bmm(xc, wr) + br)
    ig = sigmoid(bmm(xc, wi) + bi)
    log_a = -RG_C * r * softplus(-lam)
    row = lax.broadcasted_iota(jnp.int32, xc.shape, 0) + t0
    mult = jnp.where(row == 0, 1.0, jnp.sqrt(-expm1(2.0 * log_a)))
    return jnp.exp(log_a), mult * ig * xc


def st_a_cot(t0, lam, ha):
    return lam * shift_rows(ha, 1), lam


def _heads(x, n, w):
    return [x[:, h * w:(h + 1) * w] for h in range(n)]


def st_b(t0, q, k, v, pba, cw, alog, dtb):
    qc = silu(_conv(q, cw[:, 0:BW]))
    kc = silu(_conv(k, cw[:, BW:2 * BW]))
    vc = silu(_conv(v, cw[:, 2 * BW:3 * BW]))

    def l2n(x):
        return jnp.concatenate([s * lax.rsqrt(jnp.sum(s * s, axis=-1, keepdims=True) + EPS)
                                for s in _heads(x, B_HEADS, B_DK)], axis=-1)

    sg = sigmoid(pba)
    gg = -jnp.exp(alog) * softplus(pba + dtb)
    lane = lax.broadcasted_iota(jnp.int32, pba.shape, 1)

    def spread(x, first):
        return jnp.concatenate(
            [jnp.broadcast_to(jnp.sum(jnp.where(lane == first + h, x, 0.0), axis=-1, keepdims=True), (x.shape[0], B_DK))
             for h in range(B_HEADS)], axis=-1)

    return l2n(qc), l2n(kc), vc, spread(sg, 0), spread(gg, B_HEADS)


def st_m2(t0, ha, ga, o, z, y0, uc, bn4, cd, wglu, bglu):
    ya = ha * gelu(ga)
    yb = jnp.concatenate(
        [oh * lax.rsqrt(jnp.mean(oh * oh, axis=-1, keepdims=True) + EPS) * bh * silu(zh)
         for oh, bh, zh in zip(_heads(o, B_HEADS, B_DK), _heads(bn4, B_HEADS, B_DK), _heads(z, B_HEADS, B_DK))], axis=-1)
    yc0 = gelu(y0 + cd * uc)
    yc = yc0 * sigmoid(bmm(yc0, wglu) + bglu)
    return (jnp.concatenate([ya, yb, yc], axis=-1),)


def st_m2_cot(t0, d0, d1, d2):
    return (jnp.concatenate([d0, d1, d2], axis=-1),)


def st_m3(t0, g0, g1, g2, p0, p1, p2, bg):
    d = D_MODEL
    return (sigmoid(g0 + bg[:, 0:d]) * p0 + sigmoid(g1 + bg[:, d:2 * d]) * p1 + sigmoid(g2 + bg[:, 2 * d:3 * d]) * p2,)


def st_att(t0, q, kv):
    hs = range(X_HEADS)
    sc = [bmm(q[:, h * X_HD:(h + 1) * X_HD], kv[:, h * X_HD:(h + 1) * X_HD], False, True) * (X_HD ** -0.5) for h in hs]
    e = [jnp.exp(s - lax.stop_gradient(jnp.max(s, axis=-1, keepdims=True))) for s in sc]
    p = [x / jnp.sum(x, axis=-1, keepdims=True) for x in e]
    outs = [bmm(p[h], kv[:, D_MODEL + h * X_HD:D_MODEL + (h + 1) * X_HD]) for h in hs]
    return (jnp.concatenate(outs, axis=-1),)


def st_f2(t0, ug, uv, cwg, cwv, cbg, cbv):
    return (gelu(_conv(ug, cwg, cbg)) * _conv(uv, cwv, cbv),)


def st_s5step(t0, h, ar, ai):
    hp = shift_rows(h, 1)
    hr, hi = hp[:, :S5_CW], hp[:, S5_CW:]
    return (jnp.concatenate([ar * hr - ai * hi, ar * hi + ai * hr], axis=-1),)


SUB_ROWS = {st_f2: 32}


W_MAIN = 6 * BW
W_BA = W_MAIN + 2 * B_HEADS
W_UC = W_BA + BW
COL_GATES, COL_UC = W_MAIN, W_MAIN + 3 * D_MODEL


def split_w_in(wt):
    big = jnp.concatenate([wt[:W_MAIN], wt[W_UC:], wt[W_BA:W_UC]], axis=0)
    ba = jnp.pad(wt[W_MAIN:W_BA], ((0, 128 - 2 * B_HEADS), (0, 0)))
    return big, ba


def merge_w_in(big, ba):
    return jnp.concatenate([big[:W_MAIN], ba[:2 * B_HEADS], big[COL_UC:], big[COL_GATES:COL_UC]], axis=0)


def derive(r):
    L = r['a_w_r'].shape[0]
    eye_a = jnp.eye(A_HEADS, dtype=F32)
    eye_g = jnp.eye(C_GROUPS, dtype=F32)

    def blockdiag(w):
        return jnp.einsum('lhij,hg->lhigj', w, eye_a).reshape(L, BW, BW)

    def lanes(v, first):
        return jnp.pad(v, ((0, 0), (first, 128 - first - B_HEADS)))[:, None, :]

    lr, li = r['c_lam_re'], r['c_lam_im']
    dt = jnp.exp(r['c_log_dt'])[..., None]
    mag = jnp.exp(lr * dt)
    ar, ai = mag * jnp.cos(li * dt), mag * jnp.sin(li * dt)
    den = lr * lr + li * li
    fr = ((ar - 1.0) * lr + ai * li) / den
    fi = (ai * lr - (ar - 1.0) * li) / den
    br, bi = r['c_b_re'], r['c_b_im']
    bbr = fr[..., None] * br - fi[..., None] * bi
    bbi = fr[..., None] * bi + fi[..., None] * br
    ncol = C_CH // S5_CW

    def b_dense(bb):
        return jnp.einsum('lgpc,gh->lgchp', bb, eye_g).reshape(L, BW, ncol, S5_CW)

    bbig = jnp.concatenate([b_dense(bbr), b_dense(bbi)], axis=3).reshape(L, BW, 2 * C_CH)

    def c_dense(cc):
        return jnp.einsum('lgcp,gh->lgphc', cc, eye_g).reshape(L, ncol, S5_CW, BW)

    cbig = jnp.stack([c_dense(r['c_c_re']), -c_dense(r['c_c_im'])], axis=2).reshape(L, 2 * C_CH, BW)
    return dict(wr=blockdiag(r['a_w_r']), wi=blockdiag(r['a_w_i']),
                alog=lanes(r['b_a_log'], B_HEADS), dtb=lanes(r['b_dt_bias'], B_HEADS),
                bn4=jnp.tile(r['b_norm'], (1, B_HEADS))[:, None, :],
                ar=ar.reshape(L, 1, C_CH), ai=ai.reshape(L, 1, C_CH), bbig=bbig, cbig=cbig)


DERIVE_FROM = ['a_w_r', 'a_w_i', 'b_a_log', 'b_dt_bias', 'b_norm', 'c_lam_re', 'c_lam_im', 'c_log_dt',
               'c_b_re', 'c_b_im', 'c_c_re', 'c_c_im']


def _row(v):
    return v.reshape(1, -1)


def layer_params(w, dv, l):
    big, ba = split_w_in(w['w_inT'][l])
    return dict(
        mix_norm=_row(w['mix_norm'][l]), w_big=big, w_ba=ba, b_gate=_row(w['b_gate'][l]),
        a_conv_w=w['a_conv_w'][l], a_conv_b=_row(w['a_conv_b'][l]), wr=dv['wr'][l], b_r=_row(w['a_b_r'][l]),
        wi=dv['wi'][l], b_i=_row(w['a_b_i'][l]), lam=_row(w['a_lam'][l]),
        b_conv_w=w['b_conv_w'][l], alog=dv['alog'][l], dtb=dv['dtb'][l], bn4=dv['bn4'][l],
        ar=dv['ar'][l], ai=dv['ai'][l], bbig=dv['bbig'][l], cbig=dv['cbig'][l],
        c_d=_row(w['c_d'][l]), wglu=w['c_glu_w'][l].astype(F32), bglu=_row(w['c_glu_b'][l]),
        w_brT=w['w_brT'][l], w_out=w['w_out'][l],
        xa_norm=_row(w['xa_norm'][l]), mem_norm=_row(w['mem_norm'][l]),
        w_q=w['xa_w_q'][l], w_kvT=w['w_kvT'][l], w_o=w['xa_w_o'][l],
        ffn_norm=_row(w['ffn_norm'][l]), w_upT=w['w_upT'][l], ffn_conv_w=w['ffn_conv_w'][l],
        ffn_conv_b=_row(w['ffn_conv_b'][l]), w_down=w['ffn_w_down'][l])


def _a_pars(p):
    return [Par(p['a_conv_w']), Par(p['a_conv_b']), Par(p['wr']), Par(p['b_r']), Par(p['wi']), Par(p['b_i']), Par(p['lam'])]


def _f2_pars(p):
    d = D_MODEL
    return [Par(p['ffn_conv_w'], 'col', 0, d), Par(p['ffn_conv_w'], 'col', 3, d),
            Par(p['ffn_conv_b'], 'col', 0, d), Par(p['ffn_conv_b'], 'col', 3, d)]


def layer_fwd(x, mem, p, seq, mseq, l):
    d = D_MODEL
    nb = x.shape[0] // seq
    kw = dict(seq=seq, tb=TB)
    n = lambda s: f"{s}_l{l}"
    h, = stage_fwd(st_norm, [Tok(x, d)], [Par(p['mix_norm'])], [(d, MXU_DTYPE)], name=n("norm_mix"), **kw)
    P = mm(h, p['w_big'], tb=True, name=n("mm_in"))
    Pba = mm(h, p['w_ba'], tb=True, name=n("mm_in_ba"))
    a, bb = stage_fwd(st_a, [Tok(P, BW, 0, True)], _a_pars(p), [(BW, F32)] * 2, name=n("rglru_pre"), **kw)
    ha = scan_real(a, bb, seq=seq, reverse=False, name=n("rglru_scan"))
    qn, kn, vv, bB, gB = stage_fwd(
        st_b, [Tok(P, BW, 2, True), Tok(P, BW, 3, True), Tok(P, BW, 4, True), Tok(Pba, 128)],
        [Par(p['b_conv_w']), Par(p['alog']), Par(p['dtb'])], [(BW, F32)] * 5, name=n("delta_pre"), **kw)
    r3 = lambda t: t.reshape(nb, seq, BW)
    o3, states, tinvs = delta_fwd(r3(qn), r3(kn), r3(vv), r3(bB), r3(gB), name=n("delta"))
    o = o3.reshape(-1, BW)
    bu = mm(P, p['bbig'], a_cols=(COL_UC, BW), name=n("mm_s5_in"))
    hs = scan_cplx(p['ar'], p['ai'], bu, seq=seq, reverse=False, name=n("s5_scan"))
    y0 = mm(hs, p['cbig'], name=n("mm_s5_out"))
    m2_toks = [Tok(ha, BW), Tok(P, BW, 1), Tok(o, BW), Tok(P, BW, 5), Tok(y0, BW), Tok(P, BW, COL_UC // BW)]
    m2_pars = [Par(p['bn4']), Par(p['c_d']), Par(p['wglu']), Par(p['bglu'])]
    Y3, = stage_fwd(st_m2, m2_toks, m2_pars, [(3 * BW, MXU_DTYPE)], name=n("branches"), **kw)
    proj = [mm(Y3, p['w_brT'][k], tb=True, a_cols=(k * BW, BW), name=n(f"mm_branch{k}")) for k in range(3)]
    g0 = COL_GATES // d
    m3_toks = [Tok(P, d, g0), Tok(P, d, g0 + 1), Tok(P, d, g0 + 2)] + [Tok(t, d) for t in proj]
    mixed, = stage_fwd(st_m3, m3_toks, [Par(p['b_gate'])], [(d, MXU_DTYPE)], name=n("gate_mix"), **kw)
    x1 = mm(mixed, p['w_out'], residual=x, name=n("mm_out"))
    hx, = stage_fwd(st_norm, [Tok(x1, d)], [Par(p['xa_norm'])], [(d, MXU_DTYPE)], name=n("norm_xa"), **kw)
    mn, = stage_fwd(st_norm, [Tok(mem, d)], [Par(p['mem_norm'])], [(d, MXU_DTYPE)], seq=mseq, tb=TB, name=n("norm_mem"))
    qx = mm(hx, p['w_q'], name=n("mm_q"))
    kv = mm(mn, p['w_kvT'], tb=True, name=n("mm_kv"))
    kv3 = kv.reshape(nb, mseq, 2 * d)
    ox, = stage_fwd(st_att, [Tok(qx, d)], [Par(kv3, 'batch')], [(d, MXU_DTYPE)], name=n("attention"), **kw)
    x2 = mm(ox, p['w_o'], residual=x1, name=n("mm_o"))
    hf, = stage_fwd(st_norm, [Tok(x2, d)], [Par(p['ffn_norm'])], [(d, MXU_DTYPE)], name=n("norm_ffn"), **kw)
    U = mm(hf, p['w_upT'], tb=True, name=n("mm_up"))
    act, = stage_fwd(st_f2, [Tok(U, d, 0, True), Tok(U, d, 3, True)], _f2_pars(p), [(d, MXU_DTYPE)], ncol=3,
                     name=n("ffn_act"), **kw)
    x3 = mm(act, p['w_down'], residual=x2, name=n("mm_down"))
    sv = dict(x=x, h=h, P=P, Pba=Pba, a=a, ha=ha, qn=qn, kn=kn, vv=vv, bB=bB, gB=gB, states=states, tinvs=tinvs, o=o, hs=hs, y0=y0,
              Y3=Y3, proj=proj, mixed=mixed, x1=x1, hx=hx, mn=mn, qx=qx, kv3=kv3, ox=ox, x2=x2, hf=hf, U=U, act=act)
    return x3, sv


def layer_bwd(dx3, mem, p, sv, seq, mseq, l):
    d = D_MODEL
    nb = dx3.shape[0] // seq
    kw = dict(seq=seq, tb=TB)
    n = lambda s: f"{s}_l{l}"
    g = {}
    P, Pba = sv['P'], sv['Pba']
    dact = mm(dx3, p['w_down'], tb=True, name=n("bmm_down_x"))
    g['w_down'] = mm(sv['act'], dx3, ta=True, name=n("bmm_down_w"))
    (dUg, dUv), gp = stage_bwd(st_f2, [Tok(sv['U'], d, 0, True, MXU_DTYPE), Tok(sv['U'], d, 3, True, MXU_DTYPE)], _f2_pars(p),
                               [Tok(dact, d)], ncol=3, name=n("b_ffn_act"), **kw)
    g['ffn_conv_w'] = jnp.concatenate([gp[0], gp[1]], axis=1)
    g['ffn_conv_b'] = jnp.concatenate([gp[2], gp[3]], axis=1)
    dU = jnp.concatenate([dUg, dUv], axis=1)
    dhf = mm(dU, p['w_upT'], name=n("bmm_up_x"))
    g['w_upT'] = mm(dU, sv['hf'], ta=True, name=n("bmm_up_w"))
    (dx2,), (g['ffn_norm'],) = stage_bwd(st_norm, [Tok(sv['x2'], d, grad=F32, add=dx3)], [Par(p['ffn_norm'])],
                                         [Tok(dhf, d)], name=n("b_norm_ffn"), **kw)
    dox = mm(dx2, p['w_o'], tb=True, name=n("bmm_o_x"))
    g['w_o'] = mm(sv['ox'], dx2, ta=True, name=n("bmm_o_w"))
    (dqx,), (dkv3,) = stage_bwd(st_att, [Tok(sv['qx'], d, grad=MXU_DTYPE)], [Par(sv['kv3'], 'batch')], [Tok(dox, d)],
                                name=n("b_attention"), **kw)
    dkv = dkv3.reshape(-1, 2 * d)
    dhx = mm(dqx, p['w_q'], tb=True, name=n("bmm_q_x"))
    g['w_q'] = mm(sv['hx'], dqx, ta=True, name=n("bmm_q_w"))
    dmn = mm(dkv, p['w_kvT'], name=n("bmm_kv_x"))
    g['w_kvT'] = mm(dkv, sv['mn'], ta=True, name=n("bmm_kv_w"))
    _, (g['mem_norm'],) = stage_bwd(st_norm, [Tok(mem, d)], [Par(p['mem_norm'])], [Tok(dmn, d)], seq=mseq, tb=TB,
                                    name=n("b_norm_mem"))
    (dx1,), (g['xa_norm'],) = stage_bwd(st_norm, [Tok(sv['x1'], d, grad=F32, add=dx2)], [Par(p['xa_norm'])],
                                        [Tok(dhx, d)], name=n("b_norm_xa"), **kw)
    dmixed = mm(dx1, p['w_out'], tb=True, name=n("bmm_out_x"))
    g['w_out'] = mm(sv['mixed'], dx1, ta=True, name=n("bmm_out_w"))
    g0 = COL_GATES // d
    m3_toks = [Tok(P, d, g0 + k, grad=MXU_DTYPE) for k in range(3)] + [Tok(t, d, grad=MXU_DTYPE) for t in sv['proj']]
    dm3, (g['b_gate'],) = stage_bwd(st_m3, m3_toks, [Par(p['b_gate'])], [Tok(dmixed, d)], name=n("b_gate_mix"), **kw)
    dgates, dproj = dm3[:3], dm3[3:]
    dY = [mm(dproj[k], p['w_brT'][k], name=n(f"bmm_branch{k}_x")) for k in range(3)]
    g['w_brT'] = jnp.stack([mm(dproj[k], sv['Y3'], ta=True, b_cols=(k * BW, BW), name=n(f"bmm_branch{k}_w"))
                            for k in range(3)])
    m2_toks = [Tok(sv['ha'], BW, grad=F32), Tok(P, BW, 1, grad=MXU_DTYPE), Tok(sv['o'], BW, grad=F32), Tok(P, BW, 5, grad=MXU_DTYPE),
               Tok(sv['y0'], BW, grad=MXU_DTYPE), Tok(P, BW, COL_UC // BW, grad=F32)]
    m2_pars = [Par(p['bn4']), Par(p['c_d']), Par(p['wglu']), Par(p['bglu'])]
    (dha, dga, do, dz, dy0, duc0), (g['bn4'], g['c_d'], g['wglu'], g['bglu']) = stage_bwd(
        st_m2, m2_toks, m2_pars, [Tok(t, BW) for t in dY], cot_fn=st_m2_cot, name=n("b_branches"), **kw)
    dhs = mm(dy0, p['cbig'], tb=True, name=n("bmm_s5_out_x"))
    g['cbig'] = mm(sv['hs'], dy0, ta=True, name=n("bmm_s5_out_w"))
    lam_s = scan_cplx(p['ar'], -p['ai'], dhs, seq=seq, reverse=True, name=n("b_s5_scan"))
    _, (g['ar'], g['ai']) = stage_bwd(st_s5step, [Tok(sv['hs'], 2 * S5_CW, 0, True)],
                                      [Par(p['ar'], 'col', 0, S5_CW), Par(p['ai'], 'col', 0, S5_CW)],
                                      [Tok(lam_s, 2 * S5_CW)], ncol=C_CH // S5_CW, name=n("b_s5_decay"), **kw)
    duc = mm(lam_s, p['bbig'], tb=True, residual=duc0, out_dtype=MXU_DTYPE, name=n("bmm_s5_in_x"))
    g['bbig'] = mm(P, lam_s, ta=True, a_cols=(COL_UC, BW), name=n("bmm_s5_in_w"))
    lam_a = scan_real(sv['a'], dha, seq=seq, reverse=True, name=n("b_rglru_scan"))
    (dxa,), ga = stage_bwd(st_a, [Tok(P, BW, 0, True, MXU_DTYPE)], _a_pars(p), [Tok(lam_a, BW), Tok(sv['ha'], BW, 0, True)],
                           cot_fn=st_a_cot, name=n("b_rglru_pre"), **kw)
    g['a_conv_w'], g['a_conv_b'], g['wr'], g['b_r'], g['wi'], g['b_i'], g['lam'] = ga
    r3 = lambda t: t.reshape(nb, seq, BW)
    dd = delta_bwd(r3(sv['qn']), r3(sv['kn']), r3(sv['vv']), r3(sv['bB']), r3(sv['gB']), sv['states'], sv['tinvs'],
                   r3(do), name=n("b_delta"))
    (dq, dk, dv, dpba), (g['b_conv_w'], g['alog'], g['dtb']) = stage_bwd(
        st_b, [Tok(P, BW, 2, True, MXU_DTYPE), Tok(P, BW, 3, True, MXU_DTYPE), Tok(P, BW, 4, True, MXU_DTYPE), Tok(Pba, 128, grad=MXU_DTYPE)],
        [Par(p['b_conv_w']), Par(p['alog']), Par(p['dtb'])], [Tok(t.reshape(-1, BW), BW) for t in dd],
        name=n("b_delta_pre"), **kw)
    dP = jnp.concatenate([dxa, dga, dq, dk, dv, dz] + list(dgates) + [duc], axis=1)
    dh0 = mm(dpba, p['w_ba'], name=n("bmm_in_ba_x"))
    dh = mm(dP, p['w_big'], residual=dh0, name=n("bmm_in_x"))
    g['w_big'] = mm(dP, sv['h'], ta=True, name=n("bmm_in_w"))
    g['w_ba'] = mm(dpba, sv['h'], ta=True, name=n("bmm_in_ba_w"))
    (dx,), (g['mix_norm'],) = stage_bwd(st_norm, [Tok(sv['x'], d, grad=F32, add=dx1)], [Par(p['mix_norm'])],
                                        [Tok(dh, d)], name=n("b_norm_mix"), **kw)
    return dx, g


def local_step(x3d, mem3d, tgt3d, w, final_norm):
    nb, seq, d = x3d.shape
    mseq = mem3d.shape[1]
    x = x3d.reshape(nb * seq, d)
    mem = mem3d.reshape(nb * mseq, d)
    L = w['mix_norm'].shape[0]
    dv, dv_vjp = jax.vjp(derive, {k: w[k] for k in DERIVE_FROM})
    ps, svs = [], []
    for l in range(L):
        p = layer_params(w, dv, l)
        x, sv = layer_fwd(x, mem, p, seq, mseq, l)
        ps.append(p)
        svs.append(sv)
    loss, dx, g_final = loss_head(x, tgt3d.reshape(nb * seq, d), _row(final_norm), tb=TB, name="loss_head")
    gs = [None] * L
    for l in reversed(range(L)):
        dx, gs[l] = layer_bwd(dx, mem, ps[l], svs[l], seq, mseq, l)
    st = lambda k: jnp.stack([gs[l][k] for l in range(L)])
    vec = lambda k: st(k).reshape(L, -1)
    gd = dv_vjp({k: st(k) for k in ('wr', 'wi', 'alog', 'dtb', 'bn4', 'ar', 'ai', 'bbig', 'cbig')})[0]
    out = dict(gd)
    per = lambda k: [gs[l][k] for l in range(L)]
    out.update(
        mix_norm=vec('mix_norm'), w_inT=[merge_w_in(gs[l]['w_big'], gs[l]['w_ba']) for l in range(L)],
        b_gate=vec('b_gate'),
        a_conv_w=st('a_conv_w'), a_conv_b=vec('a_conv_b'), a_b_r=vec('b_r'), a_b_i=vec('b_i'), a_lam=vec('lam'),
        b_conv_w=st('b_conv_w'), c_d=vec('c_d'), c_glu_w=per('wglu'), c_glu_b=vec('bglu'),
        w_brT=per('w_brT'), w_out=per('w_out'), xa_norm=vec('xa_norm'), mem_norm=vec('mem_norm'),
        xa_w_q=per('w_q'), w_kvT=per('w_kvT'), xa_w_o=per('w_o'), ffn_norm=vec('ffn_norm'), w_upT=per('w_upT'),
        ffn_conv_w=st('ffn_conv_w'), ffn_conv_b=vec('ffn_conv_b'), ffn_w_down=per('w_down'),
        final_norm=g_final.reshape(-1))
    return loss, dx.reshape(nb, seq, d), out


LANES = 1024
ROW_PAD = 256


def _small_rows(shape):
    return -(-int(np.prod(shape)) // (LANES * SUBLANES)) * SUBLANES


def _pack(vecs, dtype):
    segs = []
    for v in vecs:
        rows = _small_rows(v.shape)
        flat = v.reshape(-1).astype(dtype)
        segs.append(jnp.pad(flat, (0, rows * LANES - flat.shape[0])).reshape(rows, LANES))
    total = sum(s.shape[0] for s in segs)
    tail = -total % ROW_PAD
    if tail:
        segs.append(jnp.zeros((tail, LANES), dtype))
    return jnp.concatenate(segs, axis=0)


BIG_LAYOUT = {'w_in': 'w_inT', 'xa_w_kv': 'w_kvT', 'ffn_w_up': 'w_upT', 'w_branch': 'w_brT'}
SEG_ALIGN = 16


def _to_layout(n, a):
    return jnp.swapaxes(a, -1, -2) if n in BIG_LAYOUT else a


def _seg_rows(shape):
    rows = int(np.prod(shape)) // LANES
    return rows, -(-rows // SEG_ALIGN) * SEG_ALIGN


def _pack_segments(mats, dtype):
    lead = mats[0].ndim - 2
    segs = []
    for m in mats:
        pad = -m.shape[-2] % SEG_ALIGN
        segs.append(jnp.pad(m.astype(dtype), [(0, 0)] * lead + [(0, pad), (0, 0)]))
    total = sum(s.shape[-2] for s in segs)
    tail = -total % ROW_PAD
    if tail:
        segs.append(jnp.zeros(segs[0].shape[:-2] + (tail, LANES), dtype))
    return jnp.concatenate(segs, axis=-2)


def _unpack_segments(buf, shapes):
    lead = buf.shape[:-2]
    out, off = [], 0
    for shp in shapes:
        rows, padded = _seg_rows(shp)
        out.append(buf[..., off:off + rows, :].reshape(lead + tuple(shp)))
        off += padded
    return out


def _unpack(flat, shapes):
    lead = flat.shape[:-2]
    out, off = [], 0
    for shp in shapes:
        cnt, rows = int(np.prod(shp)), _small_rows(shp)
        seg = flat[..., off:off + rows, :].reshape(lead + (-1,))
        out.append(seg[..., :cnt].reshape(lead + tuple(shp)))
        off += rows
    return out


def _join_shards(stacked, axis):
    t = jnp.moveaxis(stacked, 0, axis)
    shp = list(t.shape)
    return t.reshape(shp[:axis] + [shp[axis] * shp[axis + 1]] + shp[axis + 2:])


def _as2d(a):
    return a.reshape(-1, a.shape[-1])


def kernel(x, mem, mix_norm, w_in, b_gate, a_conv_w, a_conv_b, a_w_r, a_b_r, a_w_i, a_b_i, a_lam, b_conv_w, b_a_log, b_dt_bias, b_norm, c_lam_re, c_lam_im, c_log_dt, c_b_re, c_b_im, c_c_re, c_c_im, c_d, c_glu_w, c_glu_b, w_branch, w_out, xa_norm, mem_norm, xa_w_q, xa_w_kv, xa_w_o, ffn_norm, ffn_w_up, ffn_conv_w, ffn_conv_b, ffn_w_down, final_norm, loss_target, m_mix_norm, m_w_in, m_b_gate, m_a_conv_w, m_a_conv_b, m_a_w_r, m_a_b_r, m_a_w_i, m_a_b_i, m_a_lam, m_b_conv_w, m_b_a_log, m_b_dt_bias, m_b_norm, m_c_lam_re, m_c_lam_im, m_c_log_dt, m_c_b_re, m_c_b_im, m_c_c_re, m_c_c_im, m_c_d, m_c_glu_w, m_c_glu_b, m_w_branch, m_w_out, m_xa_norm, m_mem_norm, m_xa_w_q, m_xa_w_kv, m_xa_w_o, m_ffn_norm, m_ffn_w_up, m_ffn_conv_w, m_ffn_conv_b, m_ffn_w_down, m_final_norm, v_mix_norm, v_w_in, v_b_gate, v_a_conv_w, v_a_conv_b, v_a_w_r, v_a_b_r, v_a_w_i, v_a_b_i, v_a_lam, v_b_conv_w, v_b_a_log, v_b_dt_bias, v_b_norm, v_c_lam_re, v_c_lam_im, v_c_log_dt, v_c_b_re, v_c_b_im, v_c_c_re, v_c_c_im, v_c_d, v_c_glu_w, v_c_glu_b, v_w_branch, v_w_out, v_xa_norm, v_mem_norm, v_xa_w_q, v_xa_w_kv, v_xa_w_o, v_ffn_norm, v_ffn_w_up, v_ffn_conv_w, v_ffn_conv_b, v_ffn_w_down, v_final_norm):
    args = (x, mem, mix_norm, w_in, b_gate, a_conv_w, a_conv_b, a_w_r, a_b_r, a_w_i, a_b_i, a_lam, b_conv_w, b_a_log, b_dt_bias, b_norm, c_lam_re, c_lam_im, c_log_dt, c_b_re, c_b_im, c_c_re, c_c_im, c_d, c_glu_w, c_glu_b, w_branch, w_out, xa_norm, mem_norm, xa_w_q, xa_w_kv, xa_w_o, ffn_norm, ffn_w_up, ffn_conv_w, ffn_conv_b, ffn_w_down, final_norm, loss_target, m_mix_norm, m_w_in, m_b_gate, m_a_conv_w, m_a_conv_b, m_a_w_r, m_a_b_r, m_a_w_i, m_a_b_i, m_a_lam, m_b_conv_w, m_b_a_log, m_b_dt_bias, m_b_norm, m_c_lam_re, m_c_lam_im, m_c_log_dt, m_c_b_re, m_c_b_im, m_c_c_re, m_c_c_im, m_c_d, m_c_glu_w, m_c_glu_b, m_w_branch, m_w_out, m_xa_norm, m_mem_norm, m_xa_w_q, m_xa_w_kv, m_xa_w_o, m_ffn_norm, m_ffn_w_up, m_ffn_conv_w, m_ffn_conv_b, m_ffn_w_down, m_final_norm, v_mix_norm, v_w_in, v_b_gate, v_a_conv_w, v_a_conv_b, v_a_w_r, v_a_b_r, v_a_w_i, v_a_b_i, v_a_lam, v_b_conv_w, v_b_a_log, v_b_dt_bias, v_b_norm, v_c_lam_re, v_c_lam_im, v_c_log_dt, v_c_b_re, v_c_b_im, v_c_c_re, v_c_c_im, v_c_d, v_c_glu_w, v_c_glu_b, v_w_branch, v_w_out, v_xa_norm, v_mem_norm, v_xa_w_q, v_xa_w_kv, v_xa_w_o, v_ffn_norm, v_ffn_w_up, v_ffn_conv_w, v_ffn_conv_b, v_ffn_w_down, v_final_norm)
    nw = len(WEIGHTS)
    x, mem = args[0], args[1]
    w_loc = dict(zip(WEIGHTS, args[2:2 + nw]))
    tgt = args[2 + nw]
    m_loc = dict(zip(WEIGHTS, args[3 + nw:3 + 2 * nw]))
    v_loc = dict(zip(WEIGHTS, args[3 + 2 * nw:3 + 3 * nw]))
    me = 4 * lax.axis_index("x") + 2 * lax.axis_index("y") + lax.axis_index("c")

    lay = {n: _to_layout(n, w_loc[n]) for n in BIG}
    nl = DEPTH
    seg_keys = [(n, l) for n in BIG for l in range(nl)]
    seg_shapes = [lay[n].shape[1:] for n, _ in seg_keys]
    gathered = _gather(_pack_segments([lay[n][l].reshape(-1, LANES) for n, l in seg_keys], BF),
                       name="gather_matmul_weights")
    w = {n: a for n, a in w_loc.items() if n not in BIG}
    for (n, l), st in zip(seg_keys, _unpack_segments(gathered, seg_shapes)):
        t = jnp.moveaxis(st, 0, -3)
        w.setdefault(BIG_LAYOUT.get(n, n), []).append(t.reshape(t.shape[:-3] + (N_DEV * t.shape[-2], t.shape[-1])))
    ss_shapes = [w_loc[n].shape for n in SMALL_SHARDED]
    gathered_s = _gather(_pack([w_loc[n] for n in SMALL_SHARDED], F32), name="gather_conv_weights")
    for n, st in zip(SMALL_SHARDED, _unpack(gathered_s, ss_shapes)):
        w[n] = _join_shards(st, SHARD_AXIS[n])

    final_norm = w.pop('final_norm')
    loss, grad_x, g = local_step(x, mem, tgt, w, final_norm)

    def cut(full):
        t = full.reshape(full.shape[:-2] + (N_DEV, full.shape[-2] // N_DEV, full.shape[-1]))
        return jnp.moveaxis(t, -3, 0).reshape(N_DEV, -1, LANES)

    send = _pack_segments([cut(g[BIG_LAYOUT.get(n, n)][l]) for n, l in seg_keys], BF)
    g_seg = _unpack_segments(_scatter_sum(send, name="scatter_matmul_grads"), seg_shapes)
    g_big = {n: _to_layout(n, jnp.stack(g_seg[i * nl:(i + 1) * nl])) for i, n in enumerate(BIG)}
    small_full_shapes = [g[n].shape for n in SMALL]
    packed = _pack([g[n] for n in SMALL] + [loss[0, :1]], F32)
    everyones = _gather(packed, name="gather_small_grads")
    summed = _unpack(sum_slabs(everyones, name="sum_small_grads"), small_full_shapes + [(1,)])
    loss_total = summed[-1].reshape(())
    g_small = {}
    for n, full in zip(SMALL, summed[:-1]):
        full = full.reshape(small_full_shapes[SMALL.index(n)])
        if n in SMALL_SHARDED:
            ax = SHARD_AXIS[n]
            loc = w_loc[n].shape[ax]
            full = lax.dynamic_slice_in_dim(full, me * loc, loc, axis=ax)
        g_small[n] = full.reshape(w_loc[n].shape)

    grads, delta, new_m, new_v = {}, {}, {}, {}
    for n in BIG:
        shp = w_loc[n].shape
        grads[n] = g_big[n]
        dl, nm, nv = adamw(_as2d(w_loc[n]), _as2d(g_big[n]), _as2d(m_loc[n]), _as2d(v_loc[n]), name=f"adamw_{n}")
        delta[n], new_m[n], new_v[n] = dl.reshape(shp), nm.reshape(shp), nv.reshape(shp)
    small_shapes = [w_loc[n].shape for n in SMALL]
    flat = [_pack([d[n] for n in SMALL], F32) for d in (w_loc, g_small, m_loc, v_loc)]
    res = adamw(*flat, name="adamw_small")
    for d, r in zip((delta, new_m, new_v), res):
        for n, a in zip(SMALL, _unpack(r, small_shapes)):
            d[n] = a
    grads.update(g_small)
    return (loss_total, grad_x, *[grads[n] for n in WEIGHTS], *[delta[n] for n in WEIGHTS],
            *[new_m[n] for n in WEIGHTS], *[new_v[n] for n in WEIGHTS])
```

```python
import functools
from typing import Any, NamedTuple

import jax
import jax.numpy as jnp
import numpy as np
from jax import lax
from jax.experimental import pallas as pl
from jax.experimental.pallas import tpu as pltpu

F32 = jnp.float32
BF = jnp.bfloat16
MXU_DTYPE = BF

EPS = 1e-6
RG_C = 8.0
N_DEV = 8
DEPTH = 4
D_MODEL = 1024
BW = 512
A_HEADS, A_HD = 8, 64
B_HEADS, B_DK = 4, 128
B_CHUNK = 64
C_GROUPS, C_GROUP, C_STATE = 32, 16, 64
C_CH = C_GROUPS * C_STATE
S5_CW = 512
X_HEADS, X_HD = 4, 256
D_FF = 3 * D_MODEL
ADAM_LR, ADAM_B1, ADAM_B2, ADAM_EPS, ADAM_WD, ADAM_STEP = 0.001, 0.9, 0.999, 1e-08, 0.01, 10

SUBLANES = 8
VMEM_LIMIT = 56 * 1024 * 1024

WEIGHTS = ['mix_norm', 'w_in', 'b_gate', 'a_conv_w', 'a_conv_b', 'a_w_r', 'a_b_r', 'a_w_i', 'a_b_i', 'a_lam',
           'b_conv_w', 'b_a_log', 'b_dt_bias', 'b_norm', 'c_lam_re', 'c_lam_im', 'c_log_dt', 'c_b_re', 'c_b_im',
           'c_c_re', 'c_c_im', 'c_d', 'c_glu_w', 'c_glu_b', 'w_branch', 'w_out', 'xa_norm', 'mem_norm', 'xa_w_q',
           'xa_w_kv', 'xa_w_o', 'ffn_norm', 'ffn_w_up', 'ffn_conv_w', 'ffn_conv_b', 'ffn_w_down', 'final_norm']
SHARD_AXIS = {'w_in': 2, 'a_conv_w': 2, 'b_conv_w': 2, 'c_glu_w': 1, 'w_branch': 3, 'w_out': 1, 'xa_w_q': 1,
              'xa_w_kv': 2, 'xa_w_o': 1, 'ffn_w_up': 2, 'ffn_conv_w': 2, 'ffn_w_down': 1}
BIG = ['w_in', 'c_glu_w', 'w_branch', 'w_out', 'xa_w_q', 'xa_w_kv', 'xa_w_o', 'ffn_w_up', 'ffn_w_down']
SMALL_SHARDED = ['a_conv_w', 'b_conv_w', 'ffn_conv_w']
SMALL = [n for n in WEIGHTS if n not in BIG]


def _dot(x, y, tx, ty):
    cx = 0 if tx else 1
    cy = 1 if ty else 0
    return lax.dot_general(x.astype(MXU_DTYPE), y.astype(MXU_DTYPE), (((cx,), (cy,)), ((), ())),
                           preferred_element_type=F32)


@functools.partial(jax.custom_vjp, nondiff_argnums=(2, 3))
def bmm(a, b, ta=False, tb=False):
    return _dot(a, b, ta, tb)


def _bmm_fwd(a, b, ta, tb):
    return _dot(a, b, ta, tb), (a, b)


def _bmm_bwd(ta, tb, res, g):
    a, b = res
    da = _dot(b, g, tb, True) if ta else _dot(g, b, False, not tb)
    db = _dot(g, a, True, ta) if tb else _dot(a, g, not ta, False)
    return da.astype(a.dtype), db.astype(b.dtype)


bmm.defvjp(_bmm_fwd, _bmm_bwd)


def _dotx(x, y, tx, ty):
    cx = 0 if tx else 1
    cy = 1 if ty else 0

    def d(p, q):
        return lax.dot_general(p, q, (((cx,), (cy,)), ((), ())), preferred_element_type=F32)

    xh, yh = x.astype(BF), y.astype(BF)
    xl, yl = (x - xh.astype(F32)).astype(BF), (y - yh.astype(F32)).astype(BF)
    return d(xh, yh) + (d(xh, yl) + d(xl, yh))


@functools.partial(jax.custom_vjp, nondiff_argnums=(2, 3))
def xmm(a, b, ta=False, tb=False):
    return _dotx(a, b, ta, tb)


def _xmm_fwd(a, b, ta, tb):
    return _dotx(a, b, ta, tb), (a, b)


def _xmm_bwd(ta, tb, res, g):
    a, b = res
    da = _dotx(b, g, tb, True) if ta else _dotx(g, b, False, not tb)
    db = _dotx(g, a, True, ta) if tb else _dotx(a, g, not ta, False)
    return da, db


xmm.defvjp(_xmm_fwd, _xmm_bwd)


@functools.partial(jax.custom_vjp, nondiff_argnums=(1,))
def roll_rows(x, s):
    return pltpu.roll(x, s, 0)


def _roll_rows_fwd(x, s):
    return pltpu.roll(x, s, 0), None


def _roll_rows_bwd(s, _, g):
    return (pltpu.roll(g, (g.shape[0] - s) % g.shape[0], 0),)


roll_rows.defvjp(_roll_rows_fwd, _roll_rows_bwd)


def shift_rows(cur_tail, s):
    cur, tail = cur_tail
    if s == 0:
        return cur
    rolled = roll_rows(cur, s)
    row = lax.broadcasted_iota(jnp.int32, tail.shape, 0)
    top = jnp.where(row < s, roll_rows(tail, s), rolled[:SUBLANES])
    if cur.shape[0] == SUBLANES:
        return top
    return jnp.concatenate([top, rolled[SUBLANES:]], axis=0)


def softplus(x):
    return jnp.maximum(x, 0.0) + jnp.log(1.0 + jnp.exp(-jnp.abs(x)))


def expm1(x):
    series = x * (1.0 + x * (0.5 + x * (1.0 / 6.0 + x * (1.0 / 24.0 + x * (1.0 / 120.0)))))
    return jnp.where(jnp.abs(x) < 0.05, series, jnp.exp(x) - 1.0)


def sigmoid(x):
    return 1.0 / (1.0 + jnp.exp(-x))


def silu(x):
    return x * sigmoid(x)


def gelu(x):
    return 0.5 * x * (1.0 + jnp.tanh(0.7978845608028654 * (x + 0.044715 * (x * x * x))))


def rms(x, g):
    var = jnp.mean(x * x, axis=-1, keepdims=True)
    return x * lax.rsqrt(var + EPS) * g


def cumsum_rows(x):
    n = x.shape[0]
    row = lax.broadcasted_iota(jnp.int32, x.shape, 0)
    s = 1
    while s < n:
        x = x + jnp.where(row >= s, roll_rows(x, s), 0.0)
        s *= 2
    return x


MM_VMEM_BUDGET = 36 * 1024 * 1024


def _pick(n, prefs):
    for p in prefs:
        if n % p == 0:
            return p
    return n


def mm(a, b, *, ta=False, tb=False, a_cols=None, b_cols=None, residual=None, out_dtype=F32, name):
    a0, aw = a_cols if a_cols is not None else (0, a.shape[1])
    b0, bw = b_cols if b_cols is not None else (0, b.shape[1])
    if ta:
        K, M = a.shape[0], aw
    else:
        M, K = a.shape[0], aw
    if tb:
        N, Kb = b.shape[0], bw
    else:
        Kb, N = b.shape[0], bw
    assert K == Kb, (name, a.shape, b.shape, ta, tb)
    tm = _pick(M, (1024, 512, 256, 128))
    tk = _pick(K, (2048, 1024, 3328, 512, 256, 128))
    nk = K // tk

    def vmem_bytes(t):
        size = 2 * (tm * tk * a.dtype.itemsize + tk * t * b.dtype.itemsize) + 2 * tm * t * jnp.dtype(out_dtype).itemsize
        size += 2 * tm * t * residual.dtype.itemsize if residual is not None else 0
        return size + (tm * t * 4 if nk > 1 else 0)

    tn = next((t for t in (1024, 512, 256, 128) if N % t == 0 and vmem_bytes(t) <= MM_VMEM_BUDGET), N)

    def off(c0, t):
        assert c0 % t == 0, (name, c0, t)
        return c0 // t

    if ta:
        a_spec = pl.BlockSpec((tk, tm), lambda i, j, k, o=off(a0, tm): (k, i + o))
    else:
        a_spec = pl.BlockSpec((tm, tk), lambda i, j, k, o=off(a0, tk): (i, k + o))
    if tb:
        b_spec = pl.BlockSpec((tn, tk), lambda i, j, k, o=off(b0, tk): (j, k + o))
    else:
        b_spec = pl.BlockSpec((tk, tn), lambda i, j, k, o=off(b0, tn): (k, j + o))
    o_spec = pl.BlockSpec((tm, tn), lambda i, j, k: (i, j))
    in_specs = [a_spec, b_spec]
    args = [a, b]
    if residual is not None:
        in_specs.append(o_spec)
        args.append(residual)

    def body(*refs):
        a_ref, b_ref = refs[0], refs[1]
        r_ref = refs[2] if residual is not None else None
        o_ref = refs[3] if residual is not None else refs[2]
        part = _dot(a_ref[...], b_ref[...], ta, tb)

        def finish(acc):
            if r_ref is not None:
                acc = acc + r_ref[...].astype(F32)
            o_ref[...] = acc.astype(out_dtype)

        if nk == 1:
            finish(part)
        else:
            acc_ref = refs[-1]
            k = pl.program_id(2)

            @pl.when(k == 0)
            def _():
                acc_ref[...] = part

            @pl.when(k > 0)
            def _():
                acc_ref[...] += part

            @pl.when(k == nk - 1)
            def _():
                finish(acc_ref[...])

    return pl.pallas_call(
        body, name=name, grid=(M // tm, N // tn, nk),
        in_specs=in_specs, out_specs=o_spec,
        out_shape=jax.ShapeDtypeStruct((M, N), out_dtype),
        scratch_shapes=[pltpu.VMEM((tm, tn), F32)] if nk > 1 else [],
        compiler_params=pltpu.CompilerParams(dimension_semantics=("parallel", "parallel", "arbitrary"),
                                             vmem_limit_bytes=VMEM_LIMIT),
    )(*args)


class Tok(NamedTuple):
    arr: Any
    width: int
    col: int = 0
    halo: bool = False
    grad: Any = None
    add: Any = None


class Par(NamedTuple):
    arr: Any
    kind: str = 'const'
    col: int = 0
    width: int = 0
    grad: bool = True


def _tok_specs(toks, tb, rev, ntile):
    specs, args = [], []
    for t in toks:
        if rev:
            cur = lambda j, s, c=t.col: (ntile - 1 - s, c + j)
            tail = lambda j, s, c=t.col: (jnp.maximum((ntile - 1 - s) * (tb // SUBLANES) - 1, 0), c + j)
        else:
            cur = lambda j, s, c=t.col: (s, c + j)
            tail = lambda j, s, c=t.col: (jnp.maximum(s * (tb // SUBLANES) - 1, 0), c + j)
        specs.append(pl.BlockSpec((tb, t.width), cur))
        args.append(t.arr)
        if t.halo:
            specs.append(pl.BlockSpec((SUBLANES, t.width), tail))
            args.append(t.arr)
    return specs, args


def _par_specs(pars, tpb, rev, ntile):
    specs, args = [], []
    for p in pars:
        if p.kind == 'const':
            specs.append(pl.BlockSpec(p.arr.shape, lambda j, s: (0, 0)))
        elif p.kind == 'col':
            specs.append(pl.BlockSpec((p.arr.shape[0], p.width), lambda j, s, c=p.col: (0, c + j)))
        else:
            if rev:
                specs.append(pl.BlockSpec((None,) + p.arr.shape[1:], lambda j, s: ((ntile - 1 - s) // tpb, 0, 0)))
            else:
                specs.append(pl.BlockSpec((None,) + p.arr.shape[1:], lambda j, s: (s // tpb, 0, 0)))
        args.append(p.arr)
    return specs, args


def _rows(ref, r0, n):
    return ref[...] if isinstance(r0, int) else ref[pl.ds(r0, n), :]


def _read_toks(toks, refs, t0, r0, sub):
    vals, k = [], 0
    for t in toks:
        ref = refs[k]
        k += 1
        cur = _rows(ref, r0, sub)
        if t.halo:
            tail = jnp.where(t0 == 0, jnp.zeros_like(refs[k][...]), refs[k][...])
            k += 1
            if not isinstance(r0, int):
                before = ref[pl.ds(pl.multiple_of(jnp.maximum(r0 - SUBLANES, 0), SUBLANES), SUBLANES), :]
                tail = jnp.where(r0 == 0, tail, before)
            vals.append((cur, tail))
        else:
            vals.append(cur)
    return vals, k


def _row_blocks(tb, sub, reverse, block):
    if sub is None or sub >= tb:
        block(0)
        return
    nsub = tb // sub

    def step(n, carry):
        r = nsub - 1 - n if reverse else n
        block(pl.multiple_of(r * sub, sub))
        return carry

    lax.fori_loop(0, nsub, step, 0)


def stage_fwd(fn, toks, pars, outs, *, seq, tb, ncol=1, sub=None, name):
    T = toks[0].arr.shape[0]
    tb = min(tb, seq)
    ntile, tpb = T // tb, seq // tb
    tspecs, targs = _tok_specs(toks, tb, False, ntile)
    pspecs, pargs = _par_specs(pars, tpb, False, ntile)
    n_in = len(tspecs) + len(pspecs)

    def body(*refs):
        s = pl.program_id(1)
        t0 = (s % tpb) * tb
        pvals = [r[...] for r in refs[len(tspecs):n_in]]

        def block(r0):
            tvals, _ = _read_toks(toks, refs, t0, r0, sub)
            res = fn(t0 + r0, *tvals, *pvals)
            for r, v in zip(refs[n_in:], res):
                if isinstance(r0, int):
                    r[...] = v.astype(r.dtype)
                else:
                    r[pl.ds(r0, sub), :] = v.astype(r.dtype)

        _row_blocks(tb, sub, False, block)

    return pl.pallas_call(
        body, name=name, grid=(ncol, ntile),
        in_specs=tspecs + pspecs,
        out_specs=[pl.BlockSpec((tb, w), lambda j, s: (s, j)) for w, _ in outs],
        out_shape=[jax.ShapeDtypeStruct((T, w * ncol), d) for w, d in outs],
        compiler_params=pltpu.CompilerParams(dimension_semantics=("arbitrary", "arbitrary"),
                                             vmem_limit_bytes=VMEM_LIMIT),
    )(*targs, *pargs)


def stage_bwd(fn, toks, pars, cots, *, cot_fn=None, seq, tb, ncol=1, sub=None, name):
    T = toks[0].arr.shape[0]
    tb = min(tb, seq)
    sub = SUB_ROWS.get(fn) if sub is None else sub
    ntile, tpb = T // tb, seq // tb
    tspecs, targs = _tok_specs(toks, tb, True, ntile)
    pspecs, pargs = _par_specs(pars, tpb, True, ntile)
    cspecs, cargs = _tok_specs(cots, tb, True, ntile)
    adds = [t for t in toks if t.add is not None]
    assert all(t.grad is not None for t in adds)
    aspecs = [pl.BlockSpec((tb, t.width), lambda j, s: (ntile - 1 - s, j)) for t in adds]
    aargs = [t.add for t in adds]
    n_t, n_p, n_c = len(tspecs), len(pspecs), len(cspecs)

    gtoks = [t for t in toks if t.grad is not None]
    gpars = [p for p in pars if p.grad]
    out_specs, out_shape = [], []
    for t in gtoks:
        out_specs.append(pl.BlockSpec((tb, t.width), lambda j, s: (ntile - 1 - s, j)))
        out_shape.append(jax.ShapeDtypeStruct((T, t.width * ncol), t.grad))
    for p in gpars:
        if p.kind == 'const':
            out_specs.append(pl.BlockSpec(p.arr.shape, lambda j, s: (0, 0)))
            out_shape.append(jax.ShapeDtypeStruct(p.arr.shape, F32))
        elif p.kind == 'col':
            out_specs.append(pl.BlockSpec((p.arr.shape[0], p.width), lambda j, s: (0, j)))
            out_shape.append(jax.ShapeDtypeStruct((p.arr.shape[0], p.width * ncol), F32))
        else:
            out_specs.append(pl.BlockSpec((None,) + p.arr.shape[1:], lambda j, s: ((ntile - 1 - s) // tpb, 0, 0)))
            out_shape.append(jax.ShapeDtypeStruct(p.arr.shape, F32))
    carries = [t for t in gtoks if t.halo]
    scratch = [pltpu.VMEM((SUBLANES, t.width), F32) for t in carries]

    def body(*refs):
        j, s = pl.program_id(0), pl.program_id(1)
        i = ntile - 1 - s
        t0 = (i % tpb) * tb
        t_refs = refs[:n_t]
        p_refs = refs[n_t:n_t + n_p]
        c_refs = refs[n_t + n_p:n_t + n_p + n_c]
        a_refs = refs[n_t + n_p + n_c:n_t + n_p + n_c + len(adds)]
        o_refs = refs[n_t + n_p + n_c + len(adds):]
        gt_refs = o_refs[:len(gtoks)]
        gp_refs = o_refs[len(gtoks):len(gtoks) + len(gpars)]
        carry_refs = o_refs[len(gtoks) + len(gpars):]

        pvals = [r[...] for r in p_refs]

        @pl.when(s == 0)
        def _():
            for carry in carry_refs:
                carry[...] = jnp.zeros_like(carry)

        for p, ref in zip(gpars, gp_refs):
            if p.kind == 'const':
                first = jnp.logical_and(j == 0, s == 0)
            elif p.kind == 'col':
                first = s == 0
            else:
                first = s % tpb == 0

            @pl.when(first)
            def _(ref=ref):
                ref[...] = jnp.zeros_like(ref)

        def block(r0):
            rows = tb if isinstance(r0, int) else sub
            tt = t0 + r0
            tvals, _ = _read_toks(toks, t_refs, t0, r0, sub)
            cvals, _ = _read_toks(cots, c_refs, t0, r0, sub)

            def f(tv, pv):
                return tuple(fn(tt, *tv, *pv))

            res, vjp = jax.vjp(f, tvals, pvals)
            ct = cot_fn(tt, *cvals) if cot_fn is not None else tuple(cvals)
            ct = tuple(c.astype(r.dtype) for c, r in zip(ct, res))
            dt, dp = vjp(ct)

            ci = 0
            ai = 0
            gi = 0
            for t, d in zip(toks, dt):
                if t.grad is None:
                    continue
                ref = gt_refs[gi]
                gi += 1
                if t.halo:
                    dcur, dtail = d
                    carry = carry_refs[ci]
                    ci += 1
                    top = dcur[:rows - SUBLANES] if rows > SUBLANES else None
                    bot = dcur[rows - SUBLANES:] + carry[...]
                    dcur = bot if top is None else jnp.concatenate([top, bot], axis=0)
                    carry[...] = jnp.where(tt == 0, jnp.zeros_like(dtail), dtail)
                else:
                    dcur = d
                if t.add is not None:
                    dcur = dcur + _rows(a_refs[ai], r0, rows).astype(F32)
                    ai += 1
                if isinstance(r0, int):
                    ref[...] = dcur.astype(ref.dtype)
                else:
                    ref[pl.ds(r0, rows), :] = dcur.astype(ref.dtype)

            gi = 0
            for p, d in zip(pars, dp):
                if not p.grad:
                    continue
                gp_refs[gi][...] += d.astype(F32)
                gi += 1

        _row_blocks(tb, sub, True, block)

    res = pl.pallas_call(
        body, name=name, grid=(ncol, ntile),
        in_specs=tspecs + pspecs + cspecs + aspecs,
        out_specs=out_specs, out_shape=out_shape, scratch_shapes=scratch,
        compiler_params=pltpu.CompilerParams(dimension_semantics=("arbitrary", "arbitrary"),
                                             vmem_limit_bytes=VMEM_LIMIT),
    )(*targs, *pargs, *cargs, *aargs)
    return list(res[:len(gtoks)]), list(res[len(gtoks):])


def _bcast_row(x, r):
    return jnp.broadcast_to(x[r:r + 1, :], x.shape)


SCAN_TB = 512


def scan_real(a, b, *, seq, reverse, name):
    T, C = a.shape
    tb = min(SCAN_TB, seq)
    nb, nt, nblk = T // seq, seq // tb, tb // SUBLANES

    def body(a_ref, b_ref, h_ref, carry_h, carry_a):
        @pl.when(pl.program_id(1) == 0)
        def _():
            carry_h[...] = jnp.zeros_like(carry_h)
            carry_a[...] = jnp.zeros_like(carry_a)

        row = lax.broadcasted_iota(jnp.int32, (SUBLANES, C), 0)

        def blk(n, c):
            ch, ca = c
            k = nblk - 1 - n if reverse else n
            o = pl.multiple_of(k * SUBLANES, SUBLANES)
            A = a_ref[pl.ds(o, SUBLANES), :]
            B = b_ref[pl.ds(o, SUBLANES), :]
            if reverse:
                a_first = _bcast_row(A, 0)
                A = jnp.where(row == SUBLANES - 1, ca, pltpu.roll(A, SUBLANES - 1, 0))
                for s in (1, 2, 4):
                    keep = row < SUBLANES - s
                    Bs = jnp.where(keep, pltpu.roll(B, SUBLANES - s, 0), 0.0)
                    As = jnp.where(keep, pltpu.roll(A, SUBLANES - s, 0), 1.0)
                    B = B + A * Bs
                    A = A * As
                h = B + A * ch
                h_ref[pl.ds(o, SUBLANES), :] = h
                return _bcast_row(h, 0), a_first
            for s in (1, 2, 4):
                keep = row >= s
                Bs = jnp.where(keep, pltpu.roll(B, s, 0), 0.0)
                As = jnp.where(keep, pltpu.roll(A, s, 0), 1.0)
                B = B + A * Bs
                A = A * As
            h = B + A * ch
            h_ref[pl.ds(o, SUBLANES), :] = h
            return _bcast_row(h, SUBLANES - 1), ca

        ch, ca = lax.fori_loop(0, nblk, blk, (carry_h[...], carry_a[...]))
        carry_h[...] = ch
        carry_a[...] = ca

    if reverse:
        spec = pl.BlockSpec((tb, C), lambda bi, i: (bi * nt + nt - 1 - i, 0))
    else:
        spec = pl.BlockSpec((tb, C), lambda bi, i: (bi * nt + i, 0))
    return pl.pallas_call(
        body, name=name, grid=(nb, nt), in_specs=[spec, spec], out_specs=spec,
        out_shape=jax.ShapeDtypeStruct((T, C), F32),
        scratch_shapes=[pltpu.VMEM((SUBLANES, C), F32), pltpu.VMEM((SUBLANES, C), F32)],
        compiler_params=pltpu.CompilerParams(dimension_semantics=("arbitrary", "arbitrary"),
                                             vmem_limit_bytes=VMEM_LIMIT),
    )(a, b)


def scan_cplx(ar, ai, bu, *, seq, reverse, name):
    T = bu.shape[0]
    cw = S5_CW
    ncol = C_CH // cw
    tb = min(SCAN_TB, seq)
    nb, nt, nblk = T // seq, seq // tb, tb // SUBLANES

    def body(ar_ref, ai_ref, b_ref, h_ref, carry_r, carry_i):
        @pl.when(pl.program_id(2) == 0)
        def _():
            carry_r[...] = jnp.zeros_like(carry_r)
            carry_i[...] = jnp.zeros_like(carry_i)

        row = lax.broadcasted_iota(jnp.int32, (SUBLANES, cw), 0)
        Ar = jnp.broadcast_to(ar_ref[...], (SUBLANES, cw))
        Ai = jnp.broadcast_to(ai_ref[...], (SUBLANES, cw))
        levels = []
        for s in (1, 2, 4):
            keep = (row < SUBLANES - s) if reverse else (row >= s)
            sh = SUBLANES - s if reverse else s
            levels.append((Ar, Ai, keep, sh))
            Asr = jnp.where(keep, pltpu.roll(Ar, sh, 0), 1.0)
            Asi = jnp.where(keep, pltpu.roll(Ai, sh, 0), 0.0)
            Ar, Ai = Ar * Asr - Ai * Asi, Ar * Asi + Ai * Asr

        def blk(n, c):
            cr, ci = c
            k = nblk - 1 - n if reverse else n
            o = pl.multiple_of(k * SUBLANES, SUBLANES)
            Br = b_ref[pl.ds(o, SUBLANES), :cw]
            Bi = b_ref[pl.ds(o, SUBLANES), cw:]
            for lr, li, keep, sh in levels:
                Bsr = jnp.where(keep, pltpu.roll(Br, sh, 0), 0.0)
                Bsi = jnp.where(keep, pltpu.roll(Bi, sh, 0), 0.0)
                Br, Bi = Br + lr * Bsr - li * Bsi, Bi + lr * Bsi + li * Bsr
            hr = Br + Ar * cr - Ai * ci
            hi = Bi + Ar * ci + Ai * cr
            h_ref[pl.ds(o, SUBLANES), :cw] = hr
            h_ref[pl.ds(o, SUBLANES), cw:] = hi
            last = 0 if reverse else SUBLANES - 1
            return _bcast_row(hr, last), _bcast_row(hi, last)

        cr, ci = lax.fori_loop(0, nblk, blk, (carry_r[...], carry_i[...]))
        carry_r[...] = cr
        carry_i[...] = ci

    if reverse:
        spec = pl.BlockSpec((tb, 2 * cw), lambda bi, j, i: (bi * nt + nt - 1 - i, j))
    else:
        spec = pl.BlockSpec((tb, 2 * cw), lambda bi, j, i: (bi * nt + i, j))
    aspec = pl.BlockSpec((1, cw), lambda bi, j, i: (0, j))
    return pl.pallas_call(
        body, name=name, grid=(nb, ncol, nt), in_specs=[aspec, aspec, spec], out_specs=spec,
        out_shape=jax.ShapeDtypeStruct((T, 2 * C_CH), F32),
        scratch_shapes=[pltpu.VMEM((SUBLANES, cw), F32), pltpu.VMEM((SUBLANES, cw), F32)],
        compiler_params=pltpu.CompilerParams(dimension_semantics=("arbitrary", "arbitrary", "arbitrary"),
                                             vmem_limit_bytes=VMEM_LIMIT),
    )(ar, ai, bu)


def _col_sums(x):
    ones = jnp.ones((x.shape[0], x.shape[0]), BF)
    acc, rest = None, x
    for _ in range(3):
        piece = rest.astype(BF)
        rest = rest - piece.astype(F32)
        term = lax.dot_general(ones, piece, (((1,), (0,)), ((), ())), preferred_element_type=F32)
        acc = term if acc is None else acc + term
    return acc


@jax.custom_vjp
def col_sums(x):
    return _col_sums(x)


col_sums.defvjp(lambda x: (_col_sums(x), None), lambda _, g: (_col_sums(g),))


def tri_inv(As):
    c = As[0].shape[0]
    ii = lax.broadcasted_iota(jnp.int32, (c, c), 0)
    jj = lax.broadcasted_iota(jnp.int32, (c, c), 1)
    eye = jnp.where(ii == jj, 1.0, 0.0)
    T = [eye - a for a in As]
    P = [_dotx(a, a, False, False) for a in As]
    for _ in range(4):
        both = [_dotx(jnp.concatenate([t, p], axis=0), p, False, False) for t, p in zip(T, P)]
        T = [t + b[:c] for t, b in zip(T, both)]
        P = [b[c:] for b in both]
    return [t + _dotx(t, p, False, False) for t, p in zip(T, P)]


@jax.custom_vjp
def tri_inv_saved(a, t):
    return t


tri_inv_saved.defvjp(lambda a, t: (t, t),
                     lambda t, g: (-_dotx(_dotx(t, g, True, False), t, False, True), jnp.zeros_like(t)))


def delta_chunks(ins, tinvs=None):
    c = B_CHUNK
    Q, K, V, BB, GB, S = (list(t) for t in zip(*ins))
    n = range(len(ins))
    ii = lax.broadcasted_iota(jnp.int32, (c, c), 0)
    jj = lax.broadcasted_iota(jnp.int32, (c, c), 1)
    incl, strict, diag = ii >= jj, ii > jj, ii == jj
    qc = [q * (B_DK ** -0.5) for q in Q]
    gc = [cumsum_rows(g) for g in GB]
    gcol = [x[:, :c] for x in gc]
    grow = [col_sums(jnp.where(diag, x, 0.0)) for x in gcol]
    decay = [jnp.exp(jnp.where(incl, a - b, -1e30)) for a, b in zip(gcol, grow)]
    kb = [k * b for k, b in zip(K, BB)]
    kk = [bmm(kb[i], K[i], False, True) for i in n]
    a_mat = [jnp.where(strict, kk[i] * decay[i], 0.0) for i in n]
    tinv = tri_inv(a_mat) if tinvs is None else [tri_inv_saved(a, t) for a, t in zip(a_mat, tinvs)]
    eg = [jnp.exp(x) for x in gc]
    sol = [xmm(tinv[i], jnp.concatenate([V[i] * BB[i], kb[i] * eg[i]], axis=-1)) for i in n]
    qkr = [bmm(qc[i], K[i], False, True) for i in n]
    qk = [jnp.where(incl, qkr[i] * decay[i], 0.0) for i in n]
    glast = [x[c - 1:c, :] for x in gc]
    k_dec = [K[i] * jnp.exp(glast[i] - gc[i]) for i in n]
    ws = [bmm(sol[i][:, B_DK:], S[i]) for i in n]
    v_new = [sol[i][:, :B_DK] - ws[i] for i in n]
    o1 = [bmm(qc[i] * eg[i], S[i]) for i in n]
    o2 = [bmm(qk[i], v_new[i]) for i in n]
    kv = [bmm(k_dec[i], v_new[i], True, False) for i in n]
    o = [o1[i] + o2[i] for i in n]
    s_new = [S[i] * jnp.exp(glast[i]) + kv[i] for i in n]
    return o, s_new, tinv


def delta_fwd(q, k, v, bB, gB, *, name):
    nb, seq, _ = q.shape
    n = seq // B_CHUNK
    hd = B_DK

    def body(q_ref, k_ref, v_ref, b_ref, g_ref, o_ref, st_ref, ti_ref, state):
        @pl.when(pl.program_id(0) == 0)
        def _():
            state[...] = jnp.zeros_like(state)

        pairs = [(b, h) for b in range(nb) for h in range(B_HEADS)]
        sls = [slice(h * hd, (h + 1) * hd) for _, h in pairs]
        ins = [(q_ref[b, :, sl], k_ref[b, :, sl], v_ref[b, :, sl], b_ref[b, :, sl], g_ref[b, :, sl],
                state[b * B_HEADS + h]) for (b, h), sl in zip(pairs, sls)]
        os, s_news, tinvs = delta_chunks(ins)
        for (b, h), sl, a, o, s_new, tinv in zip(pairs, sls, ins, os, s_news, tinvs):
            st_ref[b, h] = a[5]
            o_ref[b, :, sl] = o
            ti_ref[b, h] = tinv
            state[b * B_HEADS + h] = s_new

    spec = pl.BlockSpec((nb, B_CHUNK, BW), lambda i: (0, i, 0))
    return pl.pallas_call(
        body, name=name, grid=(n,), in_specs=[spec] * 5,
        out_specs=[spec, pl.BlockSpec((nb, None, B_HEADS, hd, hd), lambda i: (0, i, 0, 0, 0)),
                   pl.BlockSpec((nb, None, B_HEADS, B_CHUNK, B_CHUNK), lambda i: (0, i, 0, 0, 0))],
        out_shape=[jax.ShapeDtypeStruct((nb, seq, BW), F32), jax.ShapeDtypeStruct((nb, n, B_HEADS, hd, hd), F32),
                   jax.ShapeDtypeStruct((nb, n, B_HEADS, B_CHUNK, B_CHUNK), F32)],
        scratch_shapes=[pltpu.VMEM((nb * B_HEADS, hd, hd), F32)],
        compiler_params=pltpu.CompilerParams(dimension_semantics=("arbitrary",), vmem_limit_bytes=VMEM_LIMIT),
    )(q, k, v, bB, gB)


def delta_bwd(q, k, v, bB, gB, states, tinvs, do, *, name):
    nb, seq, _ = q.shape
    n = seq // B_CHUNK
    hd = B_DK

    def body(q_ref, k_ref, v_ref, b_ref, g_ref, st_ref, ti_ref, do_ref, dq_ref, dk_ref, dv_ref, db_ref, dg_ref, dstate):
        @pl.when(pl.program_id(0) == 0)
        def _():
            dstate[...] = jnp.zeros_like(dstate)

        pairs = [(b, h) for b in range(nb) for h in range(B_HEADS)]
        sls = [slice(h * hd, (h + 1) * hd) for _, h in pairs]
        ins = [(q_ref[b, :, sl], k_ref[b, :, sl], v_ref[b, :, sl], b_ref[b, :, sl], g_ref[b, :, sl], st_ref[b, h])
               for (b, h), sl in zip(pairs, sls)]
        saved = [ti_ref[b, h] for b, h in pairs]
        d_o = [do_ref[b, :, sl] for (b, h), sl in zip(pairs, sls)]
        d_s = [dstate[b * B_HEADS + h] for b, h in pairs]

        def f(xs):
            return tuple(delta_chunks(xs, saved)[:2])

        grads, = jax.vjp(f, ins)[1]((d_o, d_s))
        for (b, h), sl, (dq, dk, dv, db, dg, ds) in zip(pairs, sls, grads):
            dq_ref[b, :, sl] = dq
            dk_ref[b, :, sl] = dk
            dv_ref[b, :, sl] = dv
            db_ref[b, :, sl] = db
            dg_ref[b, :, sl] = dg
            dstate[b * B_HEADS + h] = ds

    spec = pl.BlockSpec((nb, B_CHUNK, BW), lambda i: (0, n - 1 - i, 0))
    sspec = pl.BlockSpec((nb, None, B_HEADS, hd, hd), lambda i: (0, n - 1 - i, 0, 0, 0))
    tspec = pl.BlockSpec((nb, None, B_HEADS, B_CHUNK, B_CHUNK), lambda i: (0, n - 1 - i, 0, 0, 0))
    return pl.pallas_call(
        body, name=name, grid=(n,), in_specs=[spec] * 5 + [sspec, tspec, spec],
        out_specs=[spec] * 5, out_shape=[jax.ShapeDtypeStruct((nb, seq, BW), F32)] * 5,
        scratch_shapes=[pltpu.VMEM((nb * B_HEADS, hd, hd), F32)],
        compiler_params=pltpu.CompilerParams(dimension_semantics=("arbitrary",), vmem_limit_bytes=VMEM_LIMIT),
    )(q, k, v, bB, gB, states, tinvs, do)


def loss_head(x, tgt, g, *, tb, name):
    T, D = x.shape
    tb = min(tb, T)
    nt = T // tb

    def body(x_ref, t_ref, g_ref, l_ref, dx_ref, dg_ref):
        tg = t_ref[...]

        def f(xv, gv):
            err = rms(xv, gv) - tg
            return 0.5 * jnp.mean(err * err, axis=-1, keepdims=True)

        rows, vjp = jax.vjp(f, x_ref[...], g_ref[...])
        dx, dg = vjp(jnp.ones_like(rows))
        dx_ref[...] = dx
        tot = jnp.broadcast_to(jnp.sum(rows, axis=0, keepdims=True), (1, 128))

        @pl.when(pl.program_id(0) == 0)
        def _():
            l_ref[...] = tot
            dg_ref[...] = dg

        @pl.when(pl.program_id(0) > 0)
        def _():
            l_ref[...] += tot
            dg_ref[...] += dg

    tok = pl.BlockSpec((tb, D), lambda i: (i, 0))
    return pl.pallas_call(
        body, name=name, grid=(nt,),
        in_specs=[tok, tok, pl.BlockSpec((1, D), lambda i: (0, 0))],
        out_specs=[pl.BlockSpec((1, 128), lambda i: (0, 0)), tok, pl.BlockSpec((1, D), lambda i: (0, 0))],
        out_shape=[jax.ShapeDtypeStruct((1, 128), F32), jax.ShapeDtypeStruct((T, D), F32), jax.ShapeDtypeStruct((1, D), F32)],
        compiler_params=pltpu.CompilerParams(dimension_semantics=("arbitrary",), vmem_limit_bytes=VMEM_LIMIT),
    )(x, tgt, g)


def _row_block(rows, cols):
    budget = 256 * 1024
    tr = max(SUBLANES, min(rows, budget // max(cols, 1)) // SUBLANES * SUBLANES)
    while rows % tr:
        tr -= SUBLANES
        if tr <= 0:
            return rows
    return tr


def adamw(w, g, m, v, *, name):
    R, C = w.shape
    tr = _row_block(R, C)
    c1 = 1.0 / (1.0 - ADAM_B1 ** ADAM_STEP)
    c2 = 1.0 / (1.0 - ADAM_B2 ** ADAM_STEP)

    def body(w_ref, g_ref, m_ref, v_ref, d_ref, mo_ref, vo_ref):
        gv = g_ref[...]
        mn = ADAM_B1 * m_ref[...] + (1.0 - ADAM_B1) * gv
        vn = ADAM_B2 * v_ref[...] + (1.0 - ADAM_B2) * (gv * gv)
        d_ref[...] = -ADAM_LR * ((mn * c1) / (jnp.sqrt(vn * c2) + ADAM_EPS) + ADAM_WD * w_ref[...])
        mo_ref[...] = mn
        vo_ref[...] = vn

    spec = pl.BlockSpec((tr, C), lambda i: (i, 0))
    return pl.pallas_call(
        body, name=name, grid=(R // tr,), in_specs=[spec] * 4, out_specs=[spec] * 3,
        out_shape=[jax.ShapeDtypeStruct((R, C), F32)] * 3,
        compiler_params=pltpu.CompilerParams(dimension_semantics=("parallel",), vmem_limit_bytes=VMEM_LIMIT),
    )(w, g, m, v)


def sum_slabs(x, *, name):
    n, R, C = x.shape
    tr = _row_block(R, C * 2)

    def body(x_ref, o_ref):
        acc = x_ref[0].astype(F32)
        for d in range(1, n):
            acc = acc + x_ref[d].astype(F32)
        o_ref[...] = acc

    return pl.pallas_call(
        body, name=name, grid=(R // tr,),
        in_specs=[pl.BlockSpec((n, tr, C), lambda i: (0, i, 0))],
        out_specs=pl.BlockSpec((tr, C), lambda i: (i, 0)),
        out_shape=jax.ShapeDtypeStruct((R, C), F32),
        compiler_params=pltpu.CompilerParams(dimension_semantics=("parallel",), vmem_limit_bytes=VMEM_LIMIT),
    )(x)


def _gather(src, *, name):
    R, C = src.shape

    def body(src_ref, out_ref, send_sems, recv_sems, local_sem):
        x, y, c = lax.axis_index("x"), lax.axis_index("y"), lax.axis_index("c")
        me, sibling = (x, y, c), (x, y, 1 - c)
        chips = [(1 - x, y), (x, 1 - y), (1 - x, 1 - y)]

        def slab(px, py, pc):
            return out_ref.at[4 * px + 2 * py + pc]

        def copy(k, block, to, first_hand=False):
            return pltpu.make_async_remote_copy(
                src_ref=src_ref if first_hand else slab(*block), dst_ref=slab(*block),
                send_sem=send_sems.at[k], recv_sem=recv_sems.at[k],
                device_id=to, device_id_type=pl.DeviceIdType.MESH)

        mine = pltpu.make_async_copy(src_ref, slab(*me), local_sem)
        mine.start()
        first = [copy(0, me, sibling, True)] + [copy(1 + j, me, (*chip, c), True) for j, chip in enumerate(chips)]
        for cp in first:
            cp.start()
        passed = [copy(4 + j, (*chip, c), sibling) for j, chip in enumerate(chips)]
        for j, chip in enumerate(chips):
            copy(1 + j, (*chip, c), me).wait_recv()
            passed[j].start()
        copy(0, sibling, me).wait_recv()
        for j, chip in enumerate(chips):
            copy(4 + j, (*chip, 1 - c), me).wait_recv()
        for cp in first + passed:
            cp.wait_send()
        mine.wait()

    return pl.pallas_call(
        body, name=name,
        in_specs=[pl.BlockSpec(memory_space=pl.ANY)],
        out_specs=pl.BlockSpec(memory_space=pl.ANY),
        out_shape=jax.ShapeDtypeStruct((N_DEV, R, C), src.dtype),
        scratch_shapes=[pltpu.SemaphoreType.DMA((N_DEV - 1,)), pltpu.SemaphoreType.DMA((N_DEV - 1,)),
                        pltpu.SemaphoreType.DMA],
    )(src)


def _scatter_sum(send, *, name):
    n, R, C = send.shape
    nchip = N_DEV // 2

    def sibling_body(send_ref, out_ref, send_sems, recv_sems):
        x, y, c = lax.axis_index("x"), lax.axis_index("y"), lax.axis_index("c")
        copies = [pltpu.make_async_remote_copy(
            src_ref=send_ref.at[2 * j + (1 - c)], dst_ref=out_ref.at[j],
            send_sem=send_sems.at[j], recv_sem=recv_sems.at[j],
            device_id=(x, y, 1 - c), device_id_type=pl.DeviceIdType.MESH) for j in range(nchip)]
        for cp in copies:
            cp.start()
        for cp in copies:
            cp.wait_recv()
        for cp in copies:
            cp.wait_send()

    from_sibling = pl.pallas_call(
        sibling_body, name=name + "_d2d",
        in_specs=[pl.BlockSpec(memory_space=pl.ANY)], out_specs=pl.BlockSpec(memory_space=pl.ANY),
        out_shape=jax.ShapeDtypeStruct((nchip, R, C), send.dtype),
        scratch_shapes=[pltpu.SemaphoreType.DMA((nchip,)), pltpu.SemaphoreType.DMA((nchip,))],
    )(send)
    own = lax.dynamic_index_in_dim(send.reshape(nchip, 2, R, C), lax.axis_index("c"), axis=1, keepdims=False)

    tr = _row_block(R, C * nchip)

    def pair_body(a_ref, b_ref, o_ref):
        o_ref[...] = (a_ref[...].astype(F32) + b_ref[...].astype(F32)).astype(o_ref.dtype)

    spec = pl.BlockSpec((nchip, tr, C), lambda i: (0, i, 0))
    pair = pl.pallas_call(
        pair_body, name=name + "_pair_sum", grid=(R // tr,), in_specs=[spec, spec], out_specs=spec,
        out_shape=jax.ShapeDtypeStruct((nchip, R, C), send.dtype),
        compiler_params=pltpu.CompilerParams(dimension_semantics=("parallel",), vmem_limit_bytes=VMEM_LIMIT),
    )(own, from_sibling)

    def chips_body(src_ref, out_ref, send_sems, recv_sems, local_sem):
        x, y, c = lax.axis_index("x"), lax.axis_index("y"), lax.axis_index("c")
        mine = 2 * x + y
        local = pltpu.make_async_copy(src_ref.at[mine], out_ref.at[mine], local_sem)
        local.start()

        def copy(k, src_slab, dst_slab):
            return pltpu.make_async_remote_copy(
                src_ref=src_ref.at[src_slab], dst_ref=out_ref.at[dst_slab],
                send_sem=send_sems.at[k - 1], recv_sem=recv_sems.at[k - 1],
                device_id=(x ^ (k >> 1), y ^ (k & 1), c), device_id_type=pl.DeviceIdType.MESH)

        sends = [copy(k, mine ^ k, mine) for k in range(1, nchip)]
        for cp in sends:
            cp.start()
        for k in range(1, nchip):
            copy(k, mine ^ k, mine ^ k).wait_recv()
        for cp in sends:
            cp.wait_send()
        local.wait()

    from_chips = pl.pallas_call(
        chips_body, name=name + "_ici",
        in_specs=[pl.BlockSpec(memory_space=pl.ANY)], out_specs=pl.BlockSpec(memory_space=pl.ANY),
        out_shape=jax.ShapeDtypeStruct((nchip, R, C), send.dtype),
        scratch_shapes=[pltpu.SemaphoreType.DMA((nchip - 1,)), pltpu.SemaphoreType.DMA((nchip - 1,)),
                        pltpu.SemaphoreType.DMA],
    )(pair)
    return sum_slabs(from_chips, name=name + "_sum")


TB = 256


def st_norm(t0, x, g):
    return (rms(x.astype(F32), g),)


def _conv(xt, w, bias=None):
    kk = w.shape[0]
    acc = bias
    for i in range(kk):
        term = w[i:i + 1, :] * shift_rows(xt, kk - 1 - i)
        acc = term if acc is None else acc + term
    return acc


def st_a(t0, xa, cw, cb, wr, br, wi, bi, lam):
    xc = _conv(xa, cw, cb)
    r = sigmoid(bmm(xc, wr) + br)
    ig = sigmoid(bmm(xc, wi) + bi)
    log_a = -RG_C * r * softplus(-lam)
    row = lax.broadcasted_iota(jnp.int32, xc.shape, 0) + t0
    mult = jnp.where(row == 0, 1.0, jnp.sqrt(-expm1(2.0 * log_a)))
    return jnp.exp(log_a), mult * ig * xc


def st_a_cot(t0, lam, ha):
    return lam * shift_rows(ha, 1), lam


def _heads(x, n, w):
    return [x[:, h * w:(h + 1) * w] for h in range(n)]


def st_b(t0, q, k, v, pba, cw, alog, dtb):
    qc = silu(_conv(q, cw[:, 0:BW]))
    kc = silu(_conv(k, cw[:, BW:2 * BW]))
    vc = silu(_conv(v, cw[:, 2 * BW:3 * BW]))

    def l2n(x):
        return jnp.concatenate([s * lax.rsqrt(jnp.sum(s * s, axis=-1, keepdims=True) + EPS)
                                for s in _heads(x, B_HEADS, B_DK)], axis=-1)

    sg = sigmoid(pba)
    gg = -jnp.exp(alog) * softplus(pba + dtb)
    lane = lax.broadcasted_iota(jnp.int32, pba.shape, 1)

    def spread(x, first):
        return jnp.concatenate(
            [jnp.broadcast_to(jnp.sum(jnp.where(lane == first + h, x, 0.0), axis=-1, keepdims=True), (x.shape[0], B_DK))
             for h in range(B_HEADS)], axis=-1)

    return l2n(qc), l2n(kc), vc, spread(sg, 0), spread(gg, B_HEADS)


def st_m2(t0, ha, ga, o, z, y0, uc, bn4, cd, wglu, bglu):
    ya = ha * gelu(ga)
    yb = jnp.concatenate(
        [oh * lax.rsqrt(jnp.mean(oh * oh, axis=-1, keepdims=True) + EPS) * bh * silu(zh)
         for oh, bh, zh in zip(_heads(o, B_HEADS, B_DK), _heads(bn4, B_HEADS, B_DK), _heads(z, B_HEADS, B_DK))], axis=-1)
    yc0 = gelu(y0 + cd * uc)
    yc = yc0 * sigmoid(bmm(yc0, wglu) + bglu)
    return (jnp.concatenate([ya, yb, yc], axis=-1),)


def st_m2_cot(t0, d0, d1, d2):
    return (jnp.concatenate([d0, d1, d2], axis=-1),)


def st_m3(t0, g0, g1, g2, p0, p1, p2, bg):
    d = D_MODEL
    return (sigmoid(g0 + bg[:, 0:d]) * p0 + sigmoid(g1 + bg[:, d:2 * d]) * p1 + sigmoid(g2 + bg[:, 2 * d:3 * d]) * p2,)


def st_att(t0, q, kv):
    hs = range(X_HEADS)
    sc = [bmm(q[:, h * X_HD:(h + 1) * X_HD], kv[:, h * X_HD:(h + 1) * X_HD], False, True) * (X_HD ** -0.5) for h in hs]
    e = [jnp.exp(s - lax.stop_gradient(jnp.max(s, axis=-1, keepdims=True))) for s in sc]
    p = [x / jnp.sum(x, axis=-1, keepdims=True) for x in e]
    outs = [bmm(p[h], kv[:, D_MODEL + h * X_HD:D_MODEL + (h + 1) * X_HD]) for h in hs]
    return (jnp.concatenate(outs, axis=-1),)


def st_f2(t0, ug, uv, cwg, cwv, cbg, cbv):
    return (gelu(_conv(ug, cwg, cbg)) * _conv(uv, cwv, cbv),)


def st_s5step(t0, h, ar, ai):
    hp = shift_rows(h, 1)
    hr, hi = hp[:, :S5_CW], hp[:, S5_CW:]
    return (jnp.concatenate([ar * hr - ai * hi, ar * hi + ai * hr], axis=-1),)


SUB_ROWS = {st_f2: 32}


W_MAIN = 6 * BW
W_BA = W_MAIN + 2 * B_HEADS
W_UC = W_BA + BW
COL_GATES, COL_UC = W_MAIN, W_MAIN + 3 * D_MODEL


def split_w_in(wt):
    big = jnp.concatenate([wt[:W_MAIN], wt[W_UC:], wt[W_BA:W_UC]], axis=0)
    ba = jnp.pad(wt[W_MAIN:W_BA], ((0, 128 - 2 * B_HEADS), (0, 0)))
    return big, ba


def merge_w_in(big, ba):
    return jnp.concatenate([big[:W_MAIN], ba[:2 * B_HEADS], big[COL_UC:], big[COL_GATES:COL_UC]], axis=0)


def derive(r):
    L = r['a_w_r'].shape[0]
    eye_a = jnp.eye(A_HEADS, dtype=F32)
    eye_g = jnp.eye(C_GROUPS, dtype=F32)

    def blockdiag(w):
        return jnp.einsum('lhij,hg->lhigj', w, eye_a).reshape(L, BW, BW)

    def lanes(v, first):
        return jnp.pad(v, ((0, 0), (first, 128 - first - B_HEADS)))[:, None, :]

    lr, li = r['c_lam_re'], r['c_lam_im']
    dt = jnp.exp(r['c_log_dt'])[..., None]
    mag = jnp.exp(lr * dt)
    ar, ai = mag * jnp.cos(li * dt), mag * jnp.sin(li * dt)
    den = lr * lr + li * li
    fr = ((ar - 1.0) * lr + ai * li) / den
    fi = (ai * lr - (ar - 1.0) * li) / den
    br, bi = r['c_b_re'], r['c_b_im']
    bbr = fr[..., None] * br - fi[..., None] * bi
    bbi = fr[..., None] * bi + fi[..., None] * br
    ncol = C_CH // S5_CW

    def b_dense(bb):
        return jnp.einsum('lgpc,gh->lgchp', bb, eye_g).reshape(L, BW, ncol, S5_CW)

    bbig = jnp.concatenate([b_dense(bbr), b_dense(bbi)], axis=3).reshape(L, BW, 2 * C_CH)

    def c_dense(cc):
        return jnp.einsum('lgcp,gh->lgphc', cc, eye_g).reshape(L, ncol, S5_CW, BW)

    cbig = jnp.stack([c_dense(r['c_c_re']), -c_dense(r['c_c_im'])], axis=2).reshape(L, 2 * C_CH, BW)
    return dict(wr=blockdiag(r['a_w_r']), wi=blockdiag(r['a_w_i']),
                alog=lanes(r['b_a_log'], B_HEADS), dtb=lanes(r['b_dt_bias'], B_HEADS),
                bn4=jnp.tile(r['b_norm'], (1, B_HEADS))[:, None, :],
                ar=ar.reshape(L, 1, C_CH), ai=ai.reshape(L, 1, C_CH), bbig=bbig, cbig=cbig)


DERIVE_FROM = ['a_w_r', 'a_w_i', 'b_a_log', 'b_dt_bias', 'b_norm', 'c_lam_re', 'c_lam_im', 'c_log_dt',
               'c_b_re', 'c_b_im', 'c_c_re', 'c_c_im']


def _row(v):
    return v.reshape(1, -1)


def layer_params(w, dv, l):
    big, ba = split_w_in(w['w_inT'][l])
    return dict(
        mix_norm=_row(w['mix_norm'][l]), w_big=big, w_ba=ba, b_gate=_row(w['b_gate'][l]),
        a_conv_w=w['a_conv_w'][l], a_conv_b=_row(w['a_conv_b'][l]), wr=dv['wr'][l], b_r=_row(w['a_b_r'][l]),
        wi=dv['wi'][l], b_i=_row(w['a_b_i'][l]), lam=_row(w['a_lam'][l]),
        b_conv_w=w['b_conv_w'][l], alog=dv['alog'][l], dtb=dv['dtb'][l], bn4=dv['bn4'][l],
        ar=dv['ar'][l], ai=dv['ai'][l], bbig=dv['bbig'][l], cbig=dv['cbig'][l],
        c_d=_row(w['c_d'][l]), wglu=w['c_glu_w'][l].astype(F32), bglu=_row(w['c_glu_b'][l]),
        w_brT=w['w_brT'][l], w_out=w['w_out'][l],
        xa_norm=_row(w['xa_norm'][l]), mem_norm=_row(w['mem_norm'][l]),
        w_q=w['xa_w_q'][l], w_kvT=w['w_kvT'][l], w_o=w['xa_w_o'][l],
        ffn_norm=_row(w['ffn_norm'][l]), w_upT=w['w_upT'][l], ffn_conv_w=w['ffn_conv_w'][l],
        ffn_conv_b=_row(w['ffn_conv_b'][l]), w_down=w['ffn_w_down'][l])


def _a_pars(p):
    return [Par(p['a_conv_w']), Par(p['a_conv_b']), Par(p['wr']), Par(p['b_r']), Par(p['wi']), Par(p['b_i']), Par(p['lam'])]


def _f2_pars(p):
    d = D_MODEL
    return [Par(p['ffn_conv_w'], 'col', 0, d), Par(p['ffn_conv_w'], 'col', 3, d),
            Par(p['ffn_conv_b'], 'col', 0, d), Par(p['ffn_conv_b'], 'col', 3, d)]


def layer_fwd(x, mem, p, seq, mseq, l):
    d = D_MODEL
    nb = x.shape[0] // seq
    kw = dict(seq=seq, tb=TB)
    n = lambda s: f"{s}_l{l}"
    h, = stage_fwd(st_norm, [Tok(x, d)], [Par(p['mix_norm'])], [(d, MXU_DTYPE)], name=n("norm_mix"), **kw)
    P = mm(h, p['w_big'], tb=True, name=n("mm_in"))
    Pba = mm(h, p['w_ba'], tb=True, name=n("mm_in_ba"))
    a, bb = stage_fwd(st_a, [Tok(P, BW, 0, True)], _a_pars(p), [(BW, F32)] * 2, name=n("rglru_pre"), **kw)
    ha = scan_real(a, bb, seq=seq, reverse=False, name=n("rglru_scan"))
    qn, kn, vv, bB, gB = stage_fwd(
        st_b, [Tok(P, BW, 2, True), Tok(P, BW, 3, True), Tok(P, BW, 4, True), Tok(Pba, 128)],
        [Par(p['b_conv_w']), Par(p['alog']), Par(p['dtb'])], [(BW, F32)] * 5, name=n("delta_pre"), **kw)
    r3 = lambda t: t.reshape(nb, seq, BW)
    o3, states, tinvs = delta_fwd(r3(qn), r3(kn), r3(vv), r3(bB), r3(gB), name=n("delta"))
    o = o3.reshape(-1, BW)
    bu = mm(P, p['bbig'], a_cols=(COL_UC, BW), name=n("mm_s5_in"))
    hs = scan_cplx(p['ar'], p['ai'], bu, seq=seq, reverse=False, name=n("s5_scan"))
    y0 = mm(hs, p['cbig'], name=n("mm_s5_out"))
    m2_toks = [Tok(ha, BW), Tok(P, BW, 1), Tok(o, BW), Tok(P, BW, 5), Tok(y0, BW), Tok(P, BW, COL_UC // BW)]
    m2_pars = [Par(p['bn4']), Par(p['c_d']), Par(p['wglu']), Par(p['bglu'])]
    Y3, = stage_fwd(st_m2, m2_toks, m2_pars, [(3 * BW, MXU_DTYPE)], name=n("branches"), **kw)
    proj = [mm(Y3, p['w_brT'][k], tb=True, a_cols=(k * BW, BW), name=n(f"mm_branch{k}")) for k in range(3)]
    g0 = COL_GATES // d
    m3_toks = [Tok(P, d, g0), Tok(P, d, g0 + 1), Tok(P, d, g0 + 2)] + [Tok(t, d) for t in proj]
    mixed, = stage_fwd(st_m3, m3_toks, [Par(p['b_gate'])], [(d, MXU_DTYPE)], name=n("gate_mix"), **kw)
    x1 = mm(mixed, p['w_out'], residual=x, name=n("mm_out"))
    hx, = stage_fwd(st_norm, [Tok(x1, d)], [Par(p['xa_norm'])], [(d, MXU_DTYPE)], name=n("norm_xa"), **kw)
    mn, = stage_fwd(st_norm, [Tok(mem, d)], [Par(p['mem_norm'])], [(d, MXU_DTYPE)], seq=mseq, tb=TB, name=n("norm_mem"))
    qx = mm(hx, p['w_q'], name=n("mm_q"))
    kv = mm(mn, p['w_kvT'], tb=True, name=n("mm_kv"))
    kv3 = kv.reshape(nb, mseq, 2 * d)
    ox, = stage_fwd(st_att, [Tok(qx, d)], [Par(kv3, 'batch')], [(d, MXU_DTYPE)], name=n("attention"), **kw)
    x2 = mm(ox, p['w_o'], residual=x1, name=n("mm_o"))
    hf, = stage_fwd(st_norm, [Tok(x2, d)], [Par(p['ffn_norm'])], [(d, MXU_DTYPE)], name=n("norm_ffn"), **kw)
    U = mm(hf, p['w_upT'], tb=True, name=n("mm_up"))
    act, = stage_fwd(st_f2, [Tok(U, d, 0, True), Tok(U, d, 3, True)], _f2_pars(p), [(d, MXU_DTYPE)], ncol=3,
                     name=n("ffn_act"), **kw)
    x3 = mm(act, p['w_down'], residual=x2, name=n("mm_down"))
    sv = dict(x=x, h=h, P=P, Pba=Pba, a=a, ha=ha, qn=qn, kn=kn, vv=vv, bB=bB, gB=gB, states=states, tinvs=tinvs, o=o, hs=hs, y0=y0,
              Y3=Y3, proj=proj, mixed=mixed, x1=x1, hx=hx, mn=mn, qx=qx, kv3=kv3, ox=ox, x2=x2, hf=hf, U=U, act=act)
    return x3, sv


def layer_bwd(dx3, mem, p, sv, seq, mseq, l):
    d = D_MODEL
    nb = dx3.shape[0] // seq
    kw = dict(seq=seq, tb=TB)
    n = lambda s: f"{s}_l{l}"
    g = {}
    P, Pba = sv['P'], sv['Pba']
    dact = mm(dx3, p['w_down'], tb=True, name=n("bmm_down_x"))
    g['w_down'] = mm(sv['act'], dx3, ta=True, name=n("bmm_down_w"))
    (dUg, dUv), gp = stage_bwd(st_f2, [Tok(sv['U'], d, 0, True, MXU_DTYPE), Tok(sv['U'], d, 3, True, MXU_DTYPE)], _f2_pars(p),
                               [Tok(dact, d)], ncol=3, name=n("b_ffn_act"), **kw)
    g['ffn_conv_w'] = jnp.concatenate([gp[0], gp[1]], axis=1)
    g['ffn_conv_b'] = jnp.concatenate([gp[2], gp[3]], axis=1)
    dU = jnp.concatenate([dUg, dUv], axis=1)
    dhf = mm(dU, p['w_upT'], name=n("bmm_up_x"))
    g['w_upT'] = mm(dU, sv['hf'], ta=True, name=n("bmm_up_w"))
    (dx2,), (g['ffn_norm'],) = stage_bwd(st_norm, [Tok(sv['x2'], d, grad=F32, add=dx3)], [Par(p['ffn_norm'])],
                                         [Tok(dhf, d)], name=n("b_norm_ffn"), **kw)
    dox = mm(dx2, p['w_o'], tb=True, name=n("bmm_o_x"))
    g['w_o'] = mm(sv['ox'], dx2, ta=True, name=n("bmm_o_w"))
    (dqx,), (dkv3,) = stage_bwd(st_att, [Tok(sv['qx'], d, grad=MXU_DTYPE)], [Par(sv['kv3'], 'batch')], [Tok(dox, d)],
                                name=n("b_attention"), **kw)
    dkv = dkv3.reshape(-1, 2 * d)
    dhx = mm(dqx, p['w_q'], tb=True, name=n("bmm_q_x"))
    g['w_q'] = mm(sv['hx'], dqx, ta=True, name=n("bmm_q_w"))
    dmn = mm(dkv, p['w_kvT'], name=n("bmm_kv_x"))
    g['w_kvT'] = mm(dkv, sv['mn'], ta=True, name=n("bmm_kv_w"))
    _, (g['mem_norm'],) = stage_bwd(st_norm, [Tok(mem, d)], [Par(p['mem_norm'])], [Tok(dmn, d)], seq=mseq, tb=TB,
                                    name=n("b_norm_mem"))
    (dx1,), (g['xa_norm'],) = stage_bwd(st_norm, [Tok(sv['x1'], d, grad=F32, add=dx2)], [Par(p['xa_norm'])],
                                        [Tok(dhx, d)], name=n("b_norm_xa"), **kw)
    dmixed = mm(dx1, p['w_out'], tb=True, name=n("bmm_out_x"))
    g['w_out'] = mm(sv['mixed'], dx1, ta=True, name=n("bmm_out_w"))
    g0 = COL_GATES // d
    m3_toks = [Tok(P, d, g0 + k, grad=MXU_DTYPE) for k in range(3)] + [Tok(t, d, grad=MXU_DTYPE) for t in sv['proj']]
    dm3, (g['b_gate'],) = stage_bwd(st_m3, m3_toks, [Par(p['b_gate'])], [Tok(dmixed, d)], name=n("b_gate_mix"), **kw)
    dgates, dproj = dm3[:3], dm3[3:]
    dY = [mm(dproj[k], p['w_brT'][k], name=n(f"bmm_branch{k}_x")) for k in range(3)]
    g['w_brT'] = jnp.stack([mm(dproj[k], sv['Y3'], ta=True, b_cols=(k * BW, BW), name=n(f"bmm_branch{k}_w"))
                            for k in range(3)])
    m2_toks = [Tok(sv['ha'], BW, grad=F32), Tok(P, BW, 1, grad=MXU_DTYPE), Tok(sv['o'], BW, grad=F32), Tok(P, BW, 5, grad=MXU_DTYPE),
               Tok(sv['y0'], BW, grad=MXU_DTYPE), Tok(P, BW, COL_UC // BW, grad=F32)]
    m2_pars = [Par(p['bn4']), Par(p['c_d']), Par(p['wglu']), Par(p['bglu'])]
    (dha, dga, do, dz, dy0, duc0), (g['bn4'], g['c_d'], g['wglu'], g['bglu']) = stage_bwd(
        st_m2, m2_toks, m2_pars, [Tok(t, BW) for t in dY], cot_fn=st_m2_cot, name=n("b_branches"), **kw)
    dhs = mm(dy0, p['cbig'], tb=True, name=n("bmm_s5_out_x"))
    g['cbig'] = mm(sv['hs'], dy0, ta=True, name=n("bmm_s5_out_w"))
    lam_s = scan_cplx(p['ar'], -p['ai'], dhs, seq=seq, reverse=True, name=n("b_s5_scan"))
    _, (g['ar'], g['ai']) = stage_bwd(st_s5step, [Tok(sv['hs'], 2 * S5_CW, 0, True)],
                                      [Par(p['ar'], 'col', 0, S5_CW), Par(p['ai'], 'col', 0, S5_CW)],
                                      [Tok(lam_s, 2 * S5_CW)], ncol=C_CH // S5_CW, name=n("b_s5_decay"), **kw)
    duc = mm(lam_s, p['bbig'], tb=True, residual=duc0, out_dtype=MXU_DTYPE, name=n("bmm_s5_in_x"))
    g['bbig'] = mm(P, lam_s, ta=True, a_cols=(COL_UC, BW), name=n("bmm_s5_in_w"))
    lam_a = scan_real(sv['a'], dha, seq=seq, reverse=True, name=n("b_rglru_scan"))
    (dxa,), ga = stage_bwd(st_a, [Tok(P, BW, 0, True, MXU_DTYPE)], _a_pars(p), [Tok(lam_a, BW), Tok(sv['ha'], BW, 0, True)],
                           cot_fn=st_a_cot, name=n("b_rglru_pre"), **kw)
    g['a_conv_w'], g['a_conv_b'], g['wr'], g['b_r'], g['wi'], g['b_i'], g['lam'] = ga
    r3 = lambda t: t.reshape(nb, seq, BW)
    dd = delta_bwd(r3(sv['qn']), r3(sv['kn']), r3(sv['vv']), r3(sv['bB']), r3(sv['gB']), sv['states'], sv['tinvs'],
                   r3(do), name=n("b_delta"))
    (dq, dk, dv, dpba), (g['b_conv_w'], g['alog'], g['dtb']) = stage_bwd(
        st_b, [Tok(P, BW, 2, True, MXU_DTYPE), Tok(P, BW, 3, True, MXU_DTYPE), Tok(P, BW, 4, True, MXU_DTYPE), Tok(Pba, 128, grad=MXU_DTYPE)],
        [Par(p['b_conv_w']), Par(p['alog']), Par(p['dtb'])], [Tok(t.reshape(-1, BW), BW) for t in dd],
        name=n("b_delta_pre"), **kw)
    dP = jnp.concatenate([dxa, dga, dq, dk, dv, dz] + list(dgates) + [duc], axis=1)
    dh0 = mm(dpba, p['w_ba'], name=n("bmm_in_ba_x"))
    dh = mm(dP, p['w_big'], residual=dh0, name=n("bmm_in_x"))
    g['w_big'] = mm(dP, sv['h'], ta=True, name=n("bmm_in_w"))
    g['w_ba'] = mm(dpba, sv['h'], ta=True, name=n("bmm_in_ba_w"))
    (dx,), (g['mix_norm'],) = stage_bwd(st_norm, [Tok(sv['x'], d, grad=F32, add=dx1)], [Par(p['mix_norm'])],
                                        [Tok(dh, d)], name=n("b_norm_mix"), **kw)
    return dx, g


def local_step(x3d, mem3d, tgt3d, w, final_norm):
    nb, seq, d = x3d.shape
    mseq = mem3d.shape[1]
    x = x3d.reshape(nb * seq, d)
    mem = mem3d.reshape(nb * mseq, d)
    L = w['mix_norm'].shape[0]
    dv, dv_vjp = jax.vjp(derive, {k: w[k] for k in DERIVE_FROM})
    ps, svs = [], []
    for l in range(L):
        p = layer_params(w, dv, l)
        x, sv = layer_fwd(x, mem, p, seq, mseq, l)
        ps.append(p)
        svs.append(sv)
    loss, dx, g_final = loss_head(x, tgt3d.reshape(nb * seq, d), _row(final_norm), tb=TB, name="loss_head")
    gs = [None] * L
    for l in reversed(range(L)):
        dx, gs[l] = layer_bwd(dx, mem, ps[l], svs[l], seq, mseq, l)
    st = lambda k: jnp.stack([gs[l][k] for l in range(L)])
    vec = lambda k: st(k).reshape(L, -1)
    gd = dv_vjp({k: st(k) for k in ('wr', 'wi', 'alog', 'dtb', 'bn4', 'ar', 'ai', 'bbig', 'cbig')})[0]
    out = dict(gd)
    per = lambda k: [gs[l][k] for l in range(L)]
    out.update(
        mix_norm=vec('mix_norm'), w_inT=[merge_w_in(gs[l]['w_big'], gs[l]['w_ba']) for l in range(L)],
        b_gate=vec('b_gate'),
        a_conv_w=st('a_conv_w'), a_conv_b=vec('a_conv_b'), a_b_r=vec('b_r'), a_b_i=vec('b_i'), a_lam=vec('lam'),
        b_conv_w=st('b_conv_w'), c_d=vec('c_d'), c_glu_w=per('wglu'), c_glu_b=vec('bglu'),
        w_brT=per('w_brT'), w_out=per('w_out'), xa_norm=vec('xa_norm'), mem_norm=vec('mem_norm'),
        xa_w_q=per('w_q'), w_kvT=per('w_kvT'), xa_w_o=per('w_o'), ffn_norm=vec('ffn_norm'), w_upT=per('w_upT'),
        ffn_conv_w=st('ffn_conv_w'), ffn_conv_b=vec('ffn_conv_b'), ffn_w_down=per('w_down'),
        final_norm=g_final.reshape(-1))
    return loss, dx.reshape(nb, seq, d), out


LANES = 1024
ROW_PAD = 256


def _small_rows(shape):
    return -(-int(np.prod(shape)) // (LANES * SUBLANES)) * SUBLANES


def _pack(vecs, dtype):
    segs = []
    for v in vecs:
        rows = _small_rows(v.shape)
        flat = v.reshape(-1).astype(dtype)
        segs.append(jnp.pad(flat, (0, rows * LANES - flat.shape[0])).reshape(rows, LANES))
    total = sum(s.shape[0] for s in segs)
    tail = -total % ROW_PAD
    if tail:
        segs.append(jnp.zeros((tail, LANES), dtype))
    return jnp.concatenate(segs, axis=0)


BIG_LAYOUT = {'w_in': 'w_inT', 'xa_w_kv': 'w_kvT', 'ffn_w_up': 'w_upT', 'w_branch': 'w_brT'}
SEG_ALIGN = 16


def _to_layout(n, a):
    return jnp.swapaxes(a, -1, -2) if n in BIG_LAYOUT else a


def _seg_rows(shape):
    rows = int(np.prod(shape)) // LANES
    return rows, -(-rows // SEG_ALIGN) * SEG_ALIGN


def _pack_segments(mats, dtype):
    lead = mats[0].ndim - 2
    segs = []
    for m in mats:
        pad = -m.shape[-2] % SEG_ALIGN
        segs.append(jnp.pad(m.astype(dtype), [(0, 0)] * lead + [(0, pad), (0, 0)]))
    total = sum(s.shape[-2] for s in segs)
    tail = -total % ROW_PAD
    if tail:
        segs.append(jnp.zeros(segs[0].shape[:-2] + (tail, LANES), dtype))
    return jnp.concatenate(segs, axis=-2)


def _unpack_segments(buf, shapes):
    lead = buf.shape[:-2]
    out, off = [], 0
    for shp in shapes:
        rows, padded = _seg_rows(shp)
        out.append(buf[..., off:off + rows, :].reshape(lead + tuple(shp)))
        off += padded
    return out


def _unpack(flat, shapes):
    lead = flat.shape[:-2]
    out, off = [], 0
    for shp in shapes:
        cnt, rows = int(np.prod(shp)), _small_rows(shp)
        seg = flat[..., off:off + rows, :].reshape(lead + (-1,))
        out.append(seg[..., :cnt].reshape(lead + tuple(shp)))
        off += rows
    return out


def _join_shards(stacked, axis):
    t = jnp.moveaxis(stacked, 0, axis)
    shp = list(t.shape)
    return t.reshape(shp[:axis] + [shp[axis] * shp[axis + 1]] + shp[axis + 2:])


def _as2d(a):
    return a.reshape(-1, a.shape[-1]) if a.ndim > 1 else a.reshape(1, -1)


def kernel(x, mem, mix_norm, w_in, b_gate, a_conv_w, a_conv_b, a_w_r, a_b_r, a_w_i, a_b_i, a_lam, b_conv_w, b_a_log, b_dt_bias, b_norm, c_lam_re, c_lam_im, c_log_dt, c_b_re, c_b_im, c_c_re, c_c_im, c_d, c_glu_w, c_glu_b, w_branch, w_out, xa_norm, mem_norm, xa_w_q, xa_w_kv, xa_w_o, ffn_norm, ffn_w_up, ffn_conv_w, ffn_conv_b, ffn_w_down, final_norm, loss_target, m_mix_norm, m_w_in, m_b_gate, m_a_conv_w, m_a_conv_b, m_a_w_r, m_a_b_r, m_a_w_i, m_a_b_i, m_a_lam, m_b_conv_w, m_b_a_log, m_b_dt_bias, m_b_norm, m_c_lam_re, m_c_lam_im, m_c_log_dt, m_c_b_re, m_c_b_im, m_c_c_re, m_c_c_im, m_c_d, m_c_glu_w, m_c_glu_b, m_w_branch, m_w_out, m_xa_norm, m_mem_norm, m_xa_w_q, m_xa_w_kv, m_xa_w_o, m_ffn_norm, m_ffn_w_up, m_ffn_conv_w, m_ffn_conv_b, m_ffn_w_down, m_final_norm, v_mix_norm, v_w_in, v_b_gate, v_a_conv_w, v_a_conv_b, v_a_w_r, v_a_b_r, v_a_w_i, v_a_b_i, v_a_lam, v_b_conv_w, v_b_a_log, v_b_dt_bias, v_b_norm, v_c_lam_re, v_c_lam_im, v_c_log_dt, v_c_b_re, v_c_b_im, v_c_c_re, v_c_c_im, v_c_d, v_c_glu_w, v_c_glu_b, v_w_branch, v_w_out, v_xa_norm, v_mem_norm, v_xa_w_q, v_xa_w_kv, v_xa_w_o, v_ffn_norm, v_ffn_w_up, v_ffn_conv_w, v_ffn_conv_b, v_ffn_w_down, v_final_norm):
    args = (x, mem, mix_norm, w_in, b_gate, a_conv_w, a_conv_b, a_w_r, a_b_r, a_w_i, a_b_i, a_lam, b_conv_w, b_a_log, b_dt_bias, b_norm, c_lam_re, c_lam_im, c_log_dt, c_b_re, c_b_im, c_c_re, c_c_im, c_d, c_glu_w, c_glu_b, w_branch, w_out, xa_norm, mem_norm, xa_w_q, xa_w_kv, xa_w_o, ffn_norm, ffn_w_up, ffn_conv_w, ffn_conv_b, ffn_w_down, final_norm, loss_target, m_mix_norm, m_w_in, m_b_gate, m_a_conv_w, m_a_conv_b, m_a_w_r, m_a_b_r, m_a_w_i, m_a_b_i, m_a_lam, m_b_conv_w, m_b_a_log, m_b_dt_bias, m_b_norm, m_c_lam_re, m_c_lam_im, m_c_log_dt, m_c_b_re, m_c_b_im, m_c_c_re, m_c_c_im, m_c_d, m_c_glu_w, m_c_glu_b, m_w_branch, m_w_out, m_xa_norm, m_mem_norm, m_xa_w_q, m_xa_w_kv, m_xa_w_o, m_ffn_norm, m_ffn_w_up, m_ffn_conv_w, m_ffn_conv_b, m_ffn_w_down, m_final_norm, v_mix_norm, v_w_in, v_b_gate, v_a_conv_w, v_a_conv_b, v_a_w_r, v_a_b_r, v_a_w_i, v_a_b_i, v_a_lam, v_b_conv_w, v_b_a_log, v_b_dt_bias, v_b_norm, v_c_lam_re, v_c_lam_im, v_c_log_dt, v_c_b_re, v_c_b_im, v_c_c_re, v_c_c_im, v_c_d, v_c_glu_w, v_c_glu_b, v_w_branch, v_w_out, v_xa_norm, v_mem_norm, v_xa_w_q, v_xa_w_kv, v_xa_w_o, v_ffn_norm, v_ffn_w_up, v_ffn_conv_w, v_ffn_conv_b, v_ffn_w_down, v_final_norm)
    nw = len(WEIGHTS)
    x, mem = args[0], args[1]
    w_loc = dict(zip(WEIGHTS, args[2:2 + nw]))
    tgt = args[2 + nw]
    m_loc = dict(zip(WEIGHTS, args[3 + nw:3 + 2 * nw]))
    v_loc = dict(zip(WEIGHTS, args[3 + 2 * nw:3 + 3 * nw]))
    me = 4 * lax.axis_index("x") + 2 * lax.axis_index("y") + lax.axis_index("c")

    lay = {n: _to_layout(n, w_loc[n]) for n in BIG}
    nl = DEPTH
    seg_keys = [(n, l) for n in BIG for l in range(nl)]
    seg_shapes = [lay[n].shape[1:] for n, _ in seg_keys]
    gathered = _gather(_pack_segments([lay[n][l].reshape(-1, LANES) for n, l in seg_keys], BF),
                       name="gather_matmul_weights")
    w = {n: a for n, a in w_loc.items() if n not in BIG}
    for (n, l), st in zip(seg_keys, _unpack_segments(gathered, seg_shapes)):
        t = jnp.moveaxis(st, 0, -3)
        w.setdefault(BIG_LAYOUT.get(n, n), []).append(t.reshape(t.shape[:-3] + (N_DEV * t.shape[-2], t.shape[-1])))
    ss_shapes = [w_loc[n].shape for n in SMALL_SHARDED]
    gathered_s = _gather(_pack([w_loc[n] for n in SMALL_SHARDED], F32), name="gather_conv_weights")
    for n, st in zip(SMALL_SHARDED, _unpack(gathered_s, ss_shapes)):
        w[n] = _join_shards(st, SHARD_AXIS[n])

    final_norm = w.pop('final_norm')
    loss, grad_x, g = local_step(x, mem, tgt, w, final_norm)

    def cut(full):
        t = full.reshape(full.shape[:-2] + (N_DEV, full.shape[-2] // N_DEV, full.shape[-1]))
        return jnp.moveaxis(t, -3, 0).reshape(N_DEV, -1, LANES)

    send = _pack_segments([cut(g[BIG_LAYOUT.get(n, n)][l]) for n, l in seg_keys], BF)
    g_seg = _unpack_segments(_scatter_sum(send, name="scatter_matmul_grads"), seg_shapes)
    g_big = {n: _to_layout(n, jnp.stack(g_seg[i * nl:(i + 1) * nl])) for i, n in enumerate(BIG)}
    small_full_shapes = [g[n].shape for n in SMALL]
    packed = _pack([g[n] for n in SMALL] + [loss[0, :1]], F32)
    everyones = _gather(packed, name="gather_small_grads")
    summed = _unpack(sum_slabs(everyones, name="sum_small_grads"), small_full_shapes + [(1,)])
    loss_total = summed[-1].reshape(())
    g_small = {}
    for n, full in zip(SMALL, summed[:-1]):
        full = full.reshape(small_full_shapes[SMALL.index(n)])
        if n in SMALL_SHARDED:
            ax = SHARD_AXIS[n]
            loc = w_loc[n].shape[ax]
            full = lax.dynamic_slice_in_dim(full, me * loc, loc, axis=ax)
        g_small[n] = full.reshape(w_loc[n].shape)

    grads, delta, new_m, new_v = {}, {}, {}, {}
    grads.update(g_big)
    grads.update(g_small)
    for n in WEIGHTS:
        shp = w_loc[n].shape
        dl, nm, nv = adamw(_as2d(w_loc[n]), _as2d(grads[n]), _as2d(m_loc[n]), _as2d(v_loc[n]), name=f"adamw_{n}")
        delta[n], new_m[n], new_v[n] = dl.reshape(shp), nm.reshape(shp), nv.reshape(shp)
    return (loss_total, grad_x, *[grads[n] for n in WEIGHTS], *[delta[n] for n in WEIGHTS],
            *[new_m[n] for n in WEIGHTS], *[new_v[n] for n in WEIGHTS])
```

```python
import functools
from typing import Any, NamedTuple

import jax
import jax.numpy as jnp
import numpy as np
from jax import lax
from jax.experimental import pallas as pl
from jax.experimental.pallas import tpu as pltpu

F32 = jnp.float32
BF = jnp.bfloat16
MXU_DTYPE = BF

EPS = 1e-6
RG_C = 8.0
N_DEV = 8
DEPTH = 4
D_MODEL = 1024
BW = 512
A_HEADS, A_HD = 8, 64
B_HEADS, B_DK = 4, 128
B_CHUNK = 64
C_GROUPS, C_GROUP, C_STATE = 32, 16, 64
C_CH = C_GROUPS * C_STATE
S5_CW = 512
X_HEADS, X_HD = 4, 256
D_FF = 3 * D_MODEL
ADAM_LR, ADAM_B1, ADAM_B2, ADAM_EPS, ADAM_WD, ADAM_STEP = 0.001, 0.9, 0.999, 1e-08, 0.01, 10

SUBLANES = 8
VMEM_LIMIT = 56 * 1024 * 1024

WEIGHTS = ['mix_norm', 'w_in', 'b_gate', 'a_conv_w', 'a_conv_b', 'a_w_r', 'a_b_r', 'a_w_i', 'a_b_i', 'a_lam',
           'b_conv_w', 'b_a_log', 'b_dt_bias', 'b_norm', 'c_lam_re', 'c_lam_im', 'c_log_dt', 'c_b_re', 'c_b_im',
           'c_c_re', 'c_c_im', 'c_d', 'c_glu_w', 'c_glu_b', 'w_branch', 'w_out', 'xa_norm', 'mem_norm', 'xa_w_q',
           'xa_w_kv', 'xa_w_o', 'ffn_norm', 'ffn_w_up', 'ffn_conv_w', 'ffn_conv_b', 'ffn_w_down', 'final_norm']
SHARD_AXIS = {'w_in': 2, 'a_conv_w': 2, 'b_conv_w': 2, 'c_glu_w': 1, 'w_branch': 3, 'w_out': 1, 'xa_w_q': 1,
              'xa_w_kv': 2, 'xa_w_o': 1, 'ffn_w_up': 2, 'ffn_conv_w': 2, 'ffn_w_down': 1}
BIG = ['w_in', 'c_glu_w', 'w_branch', 'w_out', 'xa_w_q', 'xa_w_kv', 'xa_w_o', 'ffn_w_up', 'ffn_w_down']
SMALL_SHARDED = ['a_conv_w', 'b_conv_w', 'ffn_conv_w']
SMALL = [n for n in WEIGHTS if n not in BIG]


def _dot(x, y, tx, ty):
    cx = 0 if tx else 1
    cy = 1 if ty else 0
    return lax.dot_general(x.astype(MXU_DTYPE), y.astype(MXU_DTYPE), (((cx,), (cy,)), ((), ())),
                           preferred_element_type=F32)


@functools.partial(jax.custom_vjp, nondiff_argnums=(2, 3))
def bmm(a, b, ta=False, tb=False):
    return _dot(a, b, ta, tb)


def _bmm_fwd(a, b, ta, tb):
    return _dot(a, b, ta, tb), (a, b)


def _bmm_bwd(ta, tb, res, g):
    a, b = res
    da = _dot(b, g, tb, True) if ta else _dot(g, b, False, not tb)
    db = _dot(g, a, True, ta) if tb else _dot(a, g, not ta, False)
    return da.astype(a.dtype), db.astype(b.dtype)


bmm.defvjp(_bmm_fwd, _bmm_bwd)


def _dotx(x, y, tx, ty):
    cx = 0 if tx else 1
    cy = 1 if ty else 0

    def d(p, q):
        return lax.dot_general(p, q, (((cx,), (cy,)), ((), ())), preferred_element_type=F32)

    xh, yh = x.astype(BF), y.astype(BF)
    xl, yl = (x - xh.astype(F32)).astype(BF), (y - yh.astype(F32)).astype(BF)
    return d(xh, yh) + (d(xh, yl) + d(xl, yh))


@functools.partial(jax.custom_vjp, nondiff_argnums=(2, 3))
def xmm(a, b, ta=False, tb=False):
    return _dotx(a, b, ta, tb)


def _xmm_fwd(a, b, ta, tb):
    return _dotx(a, b, ta, tb), (a, b)


def _xmm_bwd(ta, tb, res, g):
    a, b = res
    da = _dotx(b, g, tb, True) if ta else _dotx(g, b, False, not tb)
    db = _dotx(g, a, True, ta) if tb else _dotx(a, g, not ta, False)
    return da, db


xmm.defvjp(_xmm_fwd, _xmm_bwd)


@functools.partial(jax.custom_vjp, nondiff_argnums=(1,))
def roll_rows(x, s):
    return pltpu.roll(x, s, 0)


def _roll_rows_fwd(x, s):
    return pltpu.roll(x, s, 0), None


def _roll_rows_bwd(s, _, g):
    return (pltpu.roll(g, (g.shape[0] - s) % g.shape[0], 0),)


roll_rows.defvjp(_roll_rows_fwd, _roll_rows_bwd)


def shift_rows(cur_tail, s):
    cur, tail = cur_tail
    if s == 0:
        return cur
    rolled = roll_rows(cur, s)
    row = lax.broadcasted_iota(jnp.int32, tail.shape, 0)
    top = jnp.where(row < s, roll_rows(tail, s), rolled[:SUBLANES])
    if cur.shape[0] == SUBLANES:
        return top
    return jnp.concatenate([top, rolled[SUBLANES:]], axis=0)


def softplus(x):
    return jnp.maximum(x, 0.0) + jnp.log(1.0 + jnp.exp(-jnp.abs(x)))


def expm1(x):
    series = x * (1.0 + x * (0.5 + x * (1.0 / 6.0 + x * (1.0 / 24.0 + x * (1.0 / 120.0)))))
    return jnp.where(jnp.abs(x) < 0.05, series, jnp.exp(x) - 1.0)


def sigmoid(x):
    return 1.0 / (1.0 + jnp.exp(-x))


def silu(x):
    return x * sigmoid(x)


def gelu(x):
    return 0.5 * x * (1.0 + jnp.tanh(0.7978845608028654 * (x + 0.044715 * (x * x * x))))


def rms(x, g):
    var = jnp.mean(x * x, axis=-1, keepdims=True)
    return x * lax.rsqrt(var + EPS) * g


def cumsum_rows(x):
    n = x.shape[0]
    row = lax.broadcasted_iota(jnp.int32, x.shape, 0)
    s = 1
    while s < n:
        x = x + jnp.where(row >= s, roll_rows(x, s), 0.0)
        s *= 2
    return x


MM_VMEM_BUDGET = 36 * 1024 * 1024


def _pick(n, prefs):
    for p in prefs:
        if n % p == 0:
            return p
    return n


def mm(a, b, *, ta=False, tb=False, a_cols=None, b_cols=None, residual=None, out_dtype=F32, name):
    a0, aw = a_cols if a_cols is not None else (0, a.shape[1])
    b0, bw = b_cols if b_cols is not None else (0, b.shape[1])
    if ta:
        K, M = a.shape[0], aw
    else:
        M, K = a.shape[0], aw
    if tb:
        N, Kb = b.shape[0], bw
    else:
        Kb, N = b.shape[0], bw
    assert K == Kb, (name, a.shape, b.shape, ta, tb)
    tm = _pick(M, (1024, 512, 256, 128))
    tk = _pick(K, (2048, 1024, 3328, 512, 256, 128))
    nk = K // tk

    def vmem_bytes(t):
        size = 2 * (tm * tk * a.dtype.itemsize + tk * t * b.dtype.itemsize) + 2 * tm * t * jnp.dtype(out_dtype).itemsize
        size += 2 * tm * t * residual.dtype.itemsize if residual is not None else 0
        return size + (tm * t * 4 if nk > 1 else 0)

    tn = next((t for t in (1024, 512, 256, 128) if N % t == 0 and vmem_bytes(t) <= MM_VMEM_BUDGET), N)

    def off(c0, t):
        assert c0 % t == 0, (name, c0, t)
        return c0 // t

    if ta:
        a_spec = pl.BlockSpec((tk, tm), lambda i, j, k, o=off(a0, tm): (k, i + o))
    else:
        a_spec = pl.BlockSpec((tm, tk), lambda i, j, k, o=off(a0, tk): (i, k + o))
    if tb:
        b_spec = pl.BlockSpec((tn, tk), lambda i, j, k, o=off(b0, tk): (j, k + o))
    else:
        b_spec = pl.BlockSpec((tk, tn), lambda i, j, k, o=off(b0, tn): (k, j + o))
    o_spec = pl.BlockSpec((tm, tn), lambda i, j, k: (i, j))
    in_specs = [a_spec, b_spec]
    args = [a, b]
    if residual is not None:
        in_specs.append(o_spec)
        args.append(residual)

    def body(*refs):
        a_ref, b_ref = refs[0], refs[1]
        r_ref = refs[2] if residual is not None else None
        o_ref = refs[3] if residual is not None else refs[2]
        part = _dot(a_ref[...], b_ref[...], ta, tb)

        def finish(acc):
            if r_ref is not None:
                acc = acc + r_ref[...].astype(F32)
            o_ref[...] = acc.astype(out_dtype)

        if nk == 1:
            finish(part)
        else:
            acc_ref = refs[-1]
            k = pl.program_id(2)

            @pl.when(k == 0)
            def _():
                acc_ref[...] = part

            @pl.when(k > 0)
            def _():
                acc_ref[...] += part

            @pl.when(k == nk - 1)
            def _():
                finish(acc_ref[...])

    return pl.pallas_call(
        body, name=name, grid=(M // tm, N // tn, nk),
        in_specs=in_specs, out_specs=o_spec,
        out_shape=jax.ShapeDtypeStruct((M, N), out_dtype),
        scratch_shapes=[pltpu.VMEM((tm, tn), F32)] if nk > 1 else [],
        compiler_params=pltpu.CompilerParams(dimension_semantics=("parallel", "parallel", "arbitrary"),
                                             vmem_limit_bytes=VMEM_LIMIT),
    )(*args)


class Tok(NamedTuple):
    arr: Any
    width: int
    col: int = 0
    halo: bool = False
    grad: Any = None
    add: Any = None


class Par(NamedTuple):
    arr: Any
    kind: str = 'const'
    col: int = 0
    width: int = 0
    grad: bool = True


def _tok_specs(toks, tb, rev, ntile):
    specs, args = [], []
    for t in toks:
        if rev:
            cur = lambda j, s, c=t.col: (ntile - 1 - s, c + j)
            tail = lambda j, s, c=t.col: (jnp.maximum((ntile - 1 - s) * (tb // SUBLANES) - 1, 0), c + j)
        else:
            cur = lambda j, s, c=t.col: (s, c + j)
            tail = lambda j, s, c=t.col: (jnp.maximum(s * (tb // SUBLANES) - 1, 0), c + j)
        specs.append(pl.BlockSpec((tb, t.width), cur))
        args.append(t.arr)
        if t.halo:
            specs.append(pl.BlockSpec((SUBLANES, t.width), tail))
            args.append(t.arr)
    return specs, args


def _par_specs(pars, tpb, rev, ntile):
    specs, args = [], []
    for p in pars:
        if p.kind == 'const':
            specs.append(pl.BlockSpec(p.arr.shape, lambda j, s: (0, 0)))
        elif p.kind == 'col':
            specs.append(pl.BlockSpec((p.arr.shape[0], p.width), lambda j, s, c=p.col: (0, c + j)))
        else:
            if rev:
                specs.append(pl.BlockSpec((None,) + p.arr.shape[1:], lambda j, s: ((ntile - 1 - s) // tpb, 0, 0)))
            else:
                specs.append(pl.BlockSpec((None,) + p.arr.shape[1:], lambda j, s: (s // tpb, 0, 0)))
        args.append(p.arr)
    return specs, args


def _rows(ref, r0, n):
    return ref[...] if isinstance(r0, int) else ref[pl.ds(r0, n), :]


def _read_toks(toks, refs, t0, r0, sub):
    vals, k = [], 0
    for t in toks:
        ref = refs[k]
        k += 1
        cur = _rows(ref, r0, sub)
        if t.halo:
            tail = jnp.where(t0 == 0, jnp.zeros_like(refs[k][...]), refs[k][...])
            k += 1
            if not isinstance(r0, int):
                before = ref[pl.ds(pl.multiple_of(jnp.maximum(r0 - SUBLANES, 0), SUBLANES), SUBLANES), :]
                tail = jnp.where(r0 == 0, tail, before)
            vals.append((cur, tail))
        else:
            vals.append(cur)
    return vals, k


def _row_blocks(tb, sub, reverse, block):
    if sub is None or sub >= tb:
        block(0)
        return
    nsub = tb // sub

    def step(n, carry):
        r = nsub - 1 - n if reverse else n
        block(pl.multiple_of(r * sub, sub))
        return carry

    lax.fori_loop(0, nsub, step, 0)


def stage_fwd(fn, toks, pars, outs, *, seq, tb, ncol=1, sub=None, name):
    T = toks[0].arr.shape[0]
    tb = min(tb, seq)
    ntile, tpb = T // tb, seq // tb
    tspecs, targs = _tok_specs(toks, tb, False, ntile)
    pspecs, pargs = _par_specs(pars, tpb, False, ntile)
    n_in = len(tspecs) + len(pspecs)

    def body(*refs):
        s = pl.program_id(1)
        t0 = (s % tpb) * tb
        pvals = [r[...] for r in refs[len(tspecs):n_in]]

        def block(r0):
            tvals, _ = _read_toks(toks, refs, t0, r0, sub)
            res = fn(t0 + r0, *tvals, *pvals)
            for r, v in zip(refs[n_in:], res):
                if isinstance(r0, int):
                    r[...] = v.astype(r.dtype)
                else:
                    r[pl.ds(r0, sub), :] = v.astype(r.dtype)

        _row_blocks(tb, sub, False, block)

    return pl.pallas_call(
        body, name=name, grid=(ncol, ntile),
        in_specs=tspecs + pspecs,
        out_specs=[pl.BlockSpec((tb, w), lambda j, s: (s, j)) for w, _ in outs],
        out_shape=[jax.ShapeDtypeStruct((T, w * ncol), d) for w, d in outs],
        compiler_params=pltpu.CompilerParams(dimension_semantics=("arbitrary", "arbitrary"),
                                             vmem_limit_bytes=VMEM_LIMIT),
    )(*targs, *pargs)


def stage_bwd(fn, toks, pars, cots, *, cot_fn=None, seq, tb, ncol=1, sub=None, name):
    T = toks[0].arr.shape[0]
    tb = min(tb, seq)
    sub = SUB_ROWS.get(fn) if sub is None else sub
    ntile, tpb = T // tb, seq // tb
    tspecs, targs = _tok_specs(toks, tb, True, ntile)
    pspecs, pargs = _par_specs(pars, tpb, True, ntile)
    cspecs, cargs = _tok_specs(cots, tb, True, ntile)
    adds = [t for t in toks if t.add is not None]
    assert all(t.grad is not None for t in adds)
    aspecs = [pl.BlockSpec((tb, t.width), lambda j, s: (ntile - 1 - s, j)) for t in adds]
    aargs = [t.add for t in adds]
    n_t, n_p, n_c = len(tspecs), len(pspecs), len(cspecs)

    gtoks = [t for t in toks if t.grad is not None]
    gpars = [p for p in pars if p.grad]
    out_specs, out_shape = [], []
    for t in gtoks:
        out_specs.append(pl.BlockSpec((tb, t.width), lambda j, s: (ntile - 1 - s, j)))
        out_shape.append(jax.ShapeDtypeStruct((T, t.width * ncol), t.grad))
    for p in gpars:
        if p.kind == 'const':
            out_specs.append(pl.BlockSpec(p.arr.shape, lambda j, s: (0, 0)))
            out_shape.append(jax.ShapeDtypeStruct(p.arr.shape, F32))
        elif p.kind == 'col':
            out_specs.append(pl.BlockSpec((p.arr.shape[0], p.width), lambda j, s: (0, j)))
            out_shape.append(jax.ShapeDtypeStruct((p.arr.shape[0], p.width * ncol), F32))
        else:
            out_specs.append(pl.BlockSpec((None,) + p.arr.shape[1:], lambda j, s: ((ntile - 1 - s) // tpb, 0, 0)))
            out_shape.append(jax.ShapeDtypeStruct(p.arr.shape, F32))
    carries = [t for t in gtoks if t.halo]
    scratch = [pltpu.VMEM((SUBLANES, t.width), F32) for t in carries]

    def body(*refs):
        j, s = pl.program_id(0), pl.program_id(1)
        i = ntile - 1 - s
        t0 = (i % tpb) * tb
        t_refs = refs[:n_t]
        p_refs = refs[n_t:n_t + n_p]
        c_refs = refs[n_t + n_p:n_t + n_p + n_c]
        a_refs = refs[n_t + n_p + n_c:n_t + n_p + n_c + len(adds)]
        o_refs = refs[n_t + n_p + n_c + len(adds):]
        gt_refs = o_refs[:len(gtoks)]
        gp_refs = o_refs[len(gtoks):len(gtoks) + len(gpars)]
        carry_refs = o_refs[len(gtoks) + len(gpars):]

        pvals = [r[...] for r in p_refs]

        @pl.when(s == 0)
        def _():
            for carry in carry_refs:
                carry[...] = jnp.zeros_like(carry)

        for p, ref in zip(gpars, gp_refs):
            if p.kind == 'const':
                first = jnp.logical_and(j == 0, s == 0)
            elif p.kind == 'col':
                first = s == 0
            else:
                first = s % tpb == 0

            @pl.when(first)
            def _(ref=ref):
                ref[...] = jnp.zeros_like(ref)

        def block(r0):
            rows = tb if isinstance(r0, int) else sub
            tt = t0 + r0
            tvals, _ = _read_toks(toks, t_refs, t0, r0, sub)
            cvals, _ = _read_toks(cots, c_refs, t0, r0, sub)

            def f(tv, pv):
                return tuple(fn(tt, *tv, *pv))

            res, vjp = jax.vjp(f, tvals, pvals)
            ct = cot_fn(tt, *cvals) if cot_fn is not None else tuple(cvals)
            ct = tuple(c.astype(r.dtype) for c, r in zip(ct, res))
            dt, dp = vjp(ct)

            ci = 0
            ai = 0
            gi = 0
            for t, d in zip(toks, dt):
                if t.grad is None:
                    continue
                ref = gt_refs[gi]
                gi += 1
                if t.halo:
                    dcur, dtail = d
                    carry = carry_refs[ci]
                    ci += 1
                    top = dcur[:rows - SUBLANES] if rows > SUBLANES else None
                    bot = dcur[rows - SUBLANES:] + carry[...]
                    dcur = bot if top is None else jnp.concatenate([top, bot], axis=0)
                    carry[...] = jnp.where(tt == 0, jnp.zeros_like(dtail), dtail)
                else:
                    dcur = d
                if t.add is not None:
                    dcur = dcur + _rows(a_refs[ai], r0, rows).astype(F32)
                    ai += 1
                if isinstance(r0, int):
                    ref[...] = dcur.astype(ref.dtype)
                else:
                    ref[pl.ds(r0, rows), :] = dcur.astype(ref.dtype)

            gi = 0
            for p, d in zip(pars, dp):
                if not p.grad:
                    continue
                gp_refs[gi][...] += d.astype(F32)
                gi += 1

        _row_blocks(tb, sub, True, block)

    res = pl.pallas_call(
        body, name=name, grid=(ncol, ntile),
        in_specs=tspecs + pspecs + cspecs + aspecs,
        out_specs=out_specs, out_shape=out_shape, scratch_shapes=scratch,
        compiler_params=pltpu.CompilerParams(dimension_semantics=("arbitrary", "arbitrary"),
                                             vmem_limit_bytes=VMEM_LIMIT),
    )(*targs, *pargs, *cargs, *aargs)
    return list(res[:len(gtoks)]), list(res[len(gtoks):])


def _bcast_row(x, r):
    return jnp.broadcast_to(x[r:r + 1, :], x.shape)


SCAN_TB = 512


def scan_real(a, b, *, seq, reverse, name):
    T, C = a.shape
    tb = min(SCAN_TB, seq)
    nb, nt, nblk = T // seq, seq // tb, tb // SUBLANES

    def body(a_ref, b_ref, h_ref, carry_h, carry_a):
        @pl.when(pl.program_id(1) == 0)
        def _():
            carry_h[...] = jnp.zeros_like(carry_h)
            carry_a[...] = jnp.zeros_like(carry_a)

        row = lax.broadcasted_iota(jnp.int32, (SUBLANES, C), 0)

        def blk(n, c):
            ch, ca = c
            k = nblk - 1 - n if reverse else n
            o = pl.multiple_of(k * SUBLANES, SUBLANES)
            A = a_ref[pl.ds(o, SUBLANES), :]
            B = b_ref[pl.ds(o, SUBLANES), :]
            if reverse:
                a_first = _bcast_row(A, 0)
                A = jnp.where(row == SUBLANES - 1, ca, pltpu.roll(A, SUBLANES - 1, 0))
                for s in (1, 2, 4):
                    keep = row < SUBLANES - s
                    Bs = jnp.where(keep, pltpu.roll(B, SUBLANES - s, 0), 0.0)
                    As = jnp.where(keep, pltpu.roll(A, SUBLANES - s, 0), 1.0)
                    B = B + A * Bs
                    A = A * As
                h = B + A * ch
                h_ref[pl.ds(o, SUBLANES), :] = h
                return _bcast_row(h, 0), a_first
            for s in (1, 2, 4):
                keep = row >= s
                Bs = jnp.where(keep, pltpu.roll(B, s, 0), 0.0)
                As = jnp.where(keep, pltpu.roll(A, s, 0), 1.0)
                B = B + A * Bs
                A = A * As
            h = B + A * ch
            h_ref[pl.ds(o, SUBLANES), :] = h
            return _bcast_row(h, SUBLANES - 1), ca

        ch, ca = lax.fori_loop(0, nblk, blk, (carry_h[...], carry_a[...]))
        carry_h[...] = ch
        carry_a[...] = ca

    if reverse:
        spec = pl.BlockSpec((tb, C), lambda bi, i: (bi * nt + nt - 1 - i, 0))
    else:
        spec = pl.BlockSpec((tb, C), lambda bi, i: (bi * nt + i, 0))
    return pl.pallas_call(
        body, name=name, grid=(nb, nt), in_specs=[spec, spec], out_specs=spec,
        out_shape=jax.ShapeDtypeStruct((T, C), F32),
        scratch_shapes=[pltpu.VMEM((SUBLANES, C), F32), pltpu.VMEM((SUBLANES, C), F32)],
        compiler_params=pltpu.CompilerParams(dimension_semantics=("arbitrary", "arbitrary"),
                                             vmem_limit_bytes=VMEM_LIMIT),
    )(a, b)


def scan_cplx(ar, ai, bu, *, seq, reverse, name):
    T = bu.shape[0]
    cw = S5_CW
    ncol = C_CH // cw
    tb = min(SCAN_TB, seq)
    nb, nt, nblk = T // seq, seq // tb, tb // SUBLANES

    def body(ar_ref, ai_ref, b_ref, h_ref, carry_r, carry_i):
        @pl.when(pl.program_id(2) == 0)
        def _():
            carry_r[...] = jnp.zeros_like(carry_r)
            carry_i[...] = jnp.zeros_like(carry_i)

        row = lax.broadcasted_iota(jnp.int32, (SUBLANES, cw), 0)
        Ar = jnp.broadcast_to(ar_ref[...], (SUBLANES, cw))
        Ai = jnp.broadcast_to(ai_ref[...], (SUBLANES, cw))
        levels = []
        for s in (1, 2, 4):
            keep = (row < SUBLANES - s) if reverse else (row >= s)
            sh = SUBLANES - s if reverse else s
            levels.append((Ar, Ai, keep, sh))
            Asr = jnp.where(keep, pltpu.roll(Ar, sh, 0), 1.0)
            Asi = jnp.where(keep, pltpu.roll(Ai, sh, 0), 0.0)
            Ar, Ai = Ar * Asr - Ai * Asi, Ar * Asi + Ai * Asr

        def blk(n, c):
            cr, ci = c
            k = nblk - 1 - n if reverse else n
            o = pl.multiple_of(k * SUBLANES, SUBLANES)
            Br = b_ref[pl.ds(o, SUBLANES), :cw]
            Bi = b_ref[pl.ds(o, SUBLANES), cw:]
            for lr, li, keep, sh in levels:
                Bsr = jnp.where(keep, pltpu.roll(Br, sh, 0), 0.0)
                Bsi = jnp.where(keep, pltpu.roll(Bi, sh, 0), 0.0)
                Br, Bi = Br + lr * Bsr - li * Bsi, Bi + lr * Bsi + li * Bsr
            hr = Br + Ar * cr - Ai * ci
            hi = Bi + Ar * ci + Ai * cr
            h_ref[pl.ds(o, SUBLANES), :cw] = hr
            h_ref[pl.ds(o, SUBLANES), cw:] = hi
            last = 0 if reverse else SUBLANES - 1
            return _bcast_row(hr, last), _bcast_row(hi, last)

        cr, ci = lax.fori_loop(0, nblk, blk, (carry_r[...], carry_i[...]))
        carry_r[...] = cr
        carry_i[...] = ci

    if reverse:
        spec = pl.BlockSpec((tb, 2 * cw), lambda bi, j, i: (bi * nt + nt - 1 - i, j))
    else:
        spec = pl.BlockSpec((tb, 2 * cw), lambda bi, j, i: (bi * nt + i, j))
    aspec = pl.BlockSpec((1, cw), lambda bi, j, i: (0, j))
    return pl.pallas_call(
        body, name=name, grid=(nb, ncol, nt), in_specs=[aspec, aspec, spec], out_specs=spec,
        out_shape=jax.ShapeDtypeStruct((T, 2 * C_CH), F32),
        scratch_shapes=[pltpu.VMEM((SUBLANES, cw), F32), pltpu.VMEM((SUBLANES, cw), F32)],
        compiler_params=pltpu.CompilerParams(dimension_semantics=("arbitrary", "arbitrary", "arbitrary"),
                                             vmem_limit_bytes=VMEM_LIMIT),
    )(ar, ai, bu)


def _col_sums(x):
    ones = jnp.ones((x.shape[0], x.shape[0]), BF)
    acc, rest = None, x
    for _ in range(3):
        piece = rest.astype(BF)
        rest = rest - piece.astype(F32)
        term = lax.dot_general(ones, piece, (((1,), (0,)), ((), ())), preferred_element_type=F32)
        acc = term if acc is None else acc + term
    return acc


@jax.custom_vjp
def col_sums(x):
    return _col_sums(x)


col_sums.defvjp(lambda x: (_col_sums(x), None), lambda _, g: (_col_sums(g),))


def tri_inv(As):
    c = As[0].shape[0]
    ii = lax.broadcasted_iota(jnp.int32, (c, c), 0)
    jj = lax.broadcasted_iota(jnp.int32, (c, c), 1)
    eye = jnp.where(ii == jj, 1.0, 0.0)
    T = [eye - a for a in As]
    P = [_dotx(a, a, False, False) for a in As]
    for _ in range(4):
        both = [_dotx(jnp.concatenate([t, p], axis=0), p, False, False) for t, p in zip(T, P)]
        T = [t + b[:c] for t, b in zip(T, both)]
        P = [b[c:] for b in both]
    return [t + _dotx(t, p, False, False) for t, p in zip(T, P)]


@jax.custom_vjp
def tri_inv_saved(a, t):
    return t


tri_inv_saved.defvjp(lambda a, t: (t, t),
                     lambda t, g: (-_dotx(_dotx(t, g, True, False), t, False, True), jnp.zeros_like(t)))


def delta_chunks(ins, tinvs=None):
    c = B_CHUNK
    Q, K, V, BB, GB, S = (list(t) for t in zip(*ins))
    n = range(len(ins))
    ii = lax.broadcasted_iota(jnp.int32, (c, c), 0)
    jj = lax.broadcasted_iota(jnp.int32, (c, c), 1)
    incl, strict, diag = ii >= jj, ii > jj, ii == jj
    qc = [q * (B_DK ** -0.5) for q in Q]
    gc = [cumsum_rows(g) for g in GB]
    gcol = [x[:, :c] for x in gc]
    grow = [col_sums(jnp.where(diag, x, 0.0)) for x in gcol]
    decay = [jnp.exp(jnp.where(incl, a - b, -1e30)) for a, b in zip(gcol, grow)]
    kb = [k * b for k, b in zip(K, BB)]
    kk = [bmm(kb[i], K[i], False, True) for i in n]
    a_mat = [jnp.where(strict, kk[i] * decay[i], 0.0) for i in n]
    tinv = tri_inv(a_mat) if tinvs is None else [tri_inv_saved(a, t) for a, t in zip(a_mat, tinvs)]
    eg = [jnp.exp(x) for x in gc]
    sol = [xmm(tinv[i], jnp.concatenate([V[i] * BB[i], kb[i] * eg[i]], axis=-1)) for i in n]
    qkr = [bmm(qc[i], K[i], False, True) for i in n]
    qk = [jnp.where(incl, qkr[i] * decay[i], 0.0) for i in n]
    glast = [x[c - 1:c, :] for x in gc]
    k_dec = [K[i] * jnp.exp(glast[i] - gc[i]) for i in n]
    ws = [bmm(sol[i][:, B_DK:], S[i]) for i in n]
    v_new = [sol[i][:, :B_DK] - ws[i] for i in n]
    o1 = [bmm(qc[i] * eg[i], S[i]) for i in n]
    o2 = [bmm(qk[i], v_new[i]) for i in n]
    kv = [bmm(k_dec[i], v_new[i], True, False) for i in n]
    o = [o1[i] + o2[i] for i in n]
    s_new = [S[i] * jnp.exp(glast[i]) + kv[i] for i in n]
    return o, s_new, tinv


def delta_fwd(q, k, v, bB, gB, *, name):
    nb, seq, _ = q.shape
    n = seq // B_CHUNK
    hd = B_DK

    def body(q_ref, k_ref, v_ref, b_ref, g_ref, o_ref, st_ref, ti_ref, state):
        @pl.when(pl.program_id(0) == 0)
        def _():
            state[...] = jnp.zeros_like(state)

        pairs = [(b, h) for b in range(nb) for h in range(B_HEADS)]
        sls = [slice(h * hd, (h + 1) * hd) for _, h in pairs]
        ins = [(q_ref[b, :, sl], k_ref[b, :, sl], v_ref[b, :, sl], b_ref[b, :, sl], g_ref[b, :, sl],
                state[b * B_HEADS + h]) for (b, h), sl in zip(pairs, sls)]
        os, s_news, tinvs = delta_chunks(ins)
        for (b, h), sl, a, o, s_new, tinv in zip(pairs, sls, ins, os, s_news, tinvs):
            st_ref[b, h] = a[5]
            o_ref[b, :, sl] = o
            ti_ref[b, h] = tinv
            state[b * B_HEADS + h] = s_new

    spec = pl.BlockSpec((nb, B_CHUNK, BW), lambda i: (0, i, 0))
    return pl.pallas_call(
        body, name=name, grid=(n,), in_specs=[spec] * 5,
        out_specs=[spec, pl.BlockSpec((nb, None, B_HEADS, hd, hd), lambda i: (0, i, 0, 0, 0)),
                   pl.BlockSpec((nb, None, B_HEADS, B_CHUNK, B_CHUNK), lambda i: (0, i, 0, 0, 0))],
        out_shape=[jax.ShapeDtypeStruct((nb, seq, BW), F32), jax.ShapeDtypeStruct((nb, n, B_HEADS, hd, hd), F32),
                   jax.ShapeDtypeStruct((nb, n, B_HEADS, B_CHUNK, B_CHUNK), F32)],
        scratch_shapes=[pltpu.VMEM((nb * B_HEADS, hd, hd), F32)],
        compiler_params=pltpu.CompilerParams(dimension_semantics=("arbitrary",), vmem_limit_bytes=VMEM_LIMIT),
    )(q, k, v, bB, gB)


def delta_bwd(q, k, v, bB, gB, states, tinvs, do, *, name):
    nb, seq, _ = q.shape
    n = seq // B_CHUNK
    hd = B_DK

    def body(q_ref, k_ref, v_ref, b_ref, g_ref, st_ref, ti_ref, do_ref, dq_ref, dk_ref, dv_ref, db_ref, dg_ref, dstate):
        @pl.when(pl.program_id(0) == 0)
        def _():
            dstate[...] = jnp.zeros_like(dstate)

        pairs = [(b, h) for b in range(nb) for h in range(B_HEADS)]
        sls = [slice(h * hd, (h + 1) * hd) for _, h in pairs]
        ins = [(q_ref[b, :, sl], k_ref[b, :, sl], v_ref[b, :, sl], b_ref[b, :, sl], g_ref[b, :, sl], st_ref[b, h])
               for (b, h), sl in zip(pairs, sls)]
        saved = [ti_ref[b, h] for b, h in pairs]
        d_o = [do_ref[b, :, sl] for (b, h), sl in zip(pairs, sls)]
        d_s = [dstate[b * B_HEADS + h] for b, h in pairs]

        def f(xs):
            return tuple(delta_chunks(xs, saved)[:2])

        grads, = jax.vjp(f, ins)[1]((d_o, d_s))
        for (b, h), sl, (dq, dk, dv, db, dg, ds) in zip(pairs, sls, grads):
            dq_ref[b, :, sl] = dq
            dk_ref[b, :, sl] = dk
            dv_ref[b, :, sl] = dv
            db_ref[b, :, sl] = db
            dg_ref[b, :, sl] = dg
            dstate[b * B_HEADS + h] = ds

    spec = pl.BlockSpec((nb, B_CHUNK, BW), lambda i: (0, n - 1 - i, 0))
    sspec = pl.BlockSpec((nb, None, B_HEADS, hd, hd), lambda i: (0, n - 1 - i, 0, 0, 0))
    tspec = pl.BlockSpec((nb, None, B_HEADS, B_CHUNK, B_CHUNK), lambda i: (0, n - 1 - i, 0, 0, 0))
    return pl.pallas_call(
        body, name=name, grid=(n,), in_specs=[spec] * 5 + [sspec, tspec, spec],
        out_specs=[spec] * 5, out_shape=[jax.ShapeDtypeStruct((nb, seq, BW), F32)] * 5,
        scratch_shapes=[pltpu.VMEM((nb * B_HEADS, hd, hd), F32)],
        compiler_params=pltpu.CompilerParams(dimension_semantics=("arbitrary",), vmem_limit_bytes=VMEM_LIMIT),
    )(q, k, v, bB, gB, states, tinvs, do)


def loss_head(x, tgt, g, *, tb, name):
    T, D = x.shape
    tb = min(tb, T)
    nt = T // tb

    def body(x_ref, t_ref, g_ref, l_ref, dx_ref, dg_ref):
        tg = t_ref[...]

        def f(xv, gv):
            err = rms(xv, gv) - tg
            return 0.5 * jnp.mean(err * err, axis=-1, keepdims=True)

        rows, vjp = jax.vjp(f, x_ref[...], g_ref[...])
        dx, dg = vjp(jnp.ones_like(rows))
        dx_ref[...] = dx
        tot = jnp.broadcast_to(jnp.sum(rows, axis=0, keepdims=True), (1, 128))

        @pl.when(pl.program_id(0) == 0)
        def _():
            l_ref[...] = tot
            dg_ref[...] = dg

        @pl.when(pl.program_id(0) > 0)
        def _():
            l_ref[...] += tot
            dg_ref[...] += dg

    tok = pl.BlockSpec((tb, D), lambda i: (i, 0))
    return pl.pallas_call(
        body, name=name, grid=(nt,),
        in_specs=[tok, tok, pl.BlockSpec((1, D), lambda i: (0, 0))],
        out_specs=[pl.BlockSpec((1, 128), lambda i: (0, 0)), tok, pl.BlockSpec((1, D), lambda i: (0, 0))],
        out_shape=[jax.ShapeDtypeStruct((1, 128), F32), jax.ShapeDtypeStruct((T, D), F32), jax.ShapeDtypeStruct((1, D), F32)],
        compiler_params=pltpu.CompilerParams(dimension_semantics=("arbitrary",), vmem_limit_bytes=VMEM_LIMIT),
    )(x, tgt, g)


def _row_block(rows, cols):
    budget = 256 * 1024
    tr = max(SUBLANES, min(rows, budget // max(cols, 1)) // SUBLANES * SUBLANES)
    while rows % tr:
        tr -= SUBLANES
        if tr <= 0:
            return rows
    return tr


def adamw(w, g, m, v, *, name):
    R, C = w.shape
    tr = _row_block(R, C)
    c1 = 1.0 / (1.0 - ADAM_B1 ** ADAM_STEP)
    c2 = 1.0 / (1.0 - ADAM_B2 ** ADAM_STEP)

    def body(w_ref, g_ref, m_ref, v_ref, d_ref, mo_ref, vo_ref):
        gv = g_ref[...]
        mn = ADAM_B1 * m_ref[...] + (1.0 - ADAM_B1) * gv
        vn = ADAM_B2 * v_ref[...] + (1.0 - ADAM_B2) * (gv * gv)
        d_ref[...] = -ADAM_LR * ((mn * c1) / (jnp.sqrt(vn * c2) + ADAM_EPS) + ADAM_WD * w_ref[...])
        mo_ref[...] = mn
        vo_ref[...] = vn

    spec = pl.BlockSpec((tr, C), lambda i: (i, 0))
    return pl.pallas_call(
        body, name=name, grid=(R // tr,), in_specs=[spec] * 4, out_specs=[spec] * 3,
        out_shape=[jax.ShapeDtypeStruct((R, C), F32)] * 3,
        compiler_params=pltpu.CompilerParams(dimension_semantics=("parallel",), vmem_limit_bytes=VMEM_LIMIT),
    )(w, g, m, v)


def sum_slabs(x, *, name):
    n, R, C = x.shape
    tr = _row_block(R, C * 2)

    def body(x_ref, o_ref):
        acc = x_ref[0].astype(F32)
        for d in range(1, n):
            acc = acc + x_ref[d].astype(F32)
        o_ref[...] = acc

    return pl.pallas_call(
        body, name=name, grid=(R // tr,),
        in_specs=[pl.BlockSpec((n, tr, C), lambda i: (0, i, 0))],
        out_specs=pl.BlockSpec((tr, C), lambda i: (i, 0)),
        out_shape=jax.ShapeDtypeStruct((R, C), F32),
        compiler_params=pltpu.CompilerParams(dimension_semantics=("parallel",), vmem_limit_bytes=VMEM_LIMIT),
    )(x)


def _gather(src, *, name):
    R, C = src.shape

    def body(src_ref, out_ref, send_sems, recv_sems, local_sem):
        x, y, c = lax.axis_index("x"), lax.axis_index("y"), lax.axis_index("c")
        me, sibling = (x, y, c), (x, y, 1 - c)
        chips = [(1 - x, y), (x, 1 - y), (1 - x, 1 - y)]

        def slab(px, py, pc):
            return out_ref.at[4 * px + 2 * py + pc]

        def copy(k, block, to, first_hand=False):
            return pltpu.make_async_remote_copy(
                src_ref=src_ref if first_hand else slab(*block), dst_ref=slab(*block),
                send_sem=send_sems.at[k], recv_sem=recv_sems.at[k],
                device_id=to, device_id_type=pl.DeviceIdType.MESH)

        mine = pltpu.make_async_copy(src_ref, slab(*me), local_sem)
        mine.start()
        first = [copy(0, me, sibling, True)] + [copy(1 + j, me, (*chip, c), True) for j, chip in enumerate(chips)]
        for cp in first:
            cp.start()
        passed = [copy(4 + j, (*chip, c), sibling) for j, chip in enumerate(chips)]
        for j, chip in enumerate(chips):
            copy(1 + j, (*chip, c), me).wait_recv()
            passed[j].start()
        copy(0, sibling, me).wait_recv()
        for j, chip in enumerate(chips):
            copy(4 + j, (*chip, 1 - c), me).wait_recv()
        for cp in first + passed:
            cp.wait_send()
        mine.wait()

    return pl.pallas_call(
        body, name=name,
        in_specs=[pl.BlockSpec(memory_space=pl.ANY)],
        out_specs=pl.BlockSpec(memory_space=pl.ANY),
        out_shape=jax.ShapeDtypeStruct((N_DEV, R, C), src.dtype),
        scratch_shapes=[pltpu.SemaphoreType.DMA((N_DEV - 1,)), pltpu.SemaphoreType.DMA((N_DEV - 1,)),
                        pltpu.SemaphoreType.DMA],
    )(src)


def _scatter_sum(send, *, name):
    n, R, C = send.shape
    nchip = N_DEV // 2

    def sibling_body(send_ref, out_ref, send_sems, recv_sems):
        x, y, c = lax.axis_index("x"), lax.axis_index("y"), lax.axis_index("c")
        copies = [pltpu.make_async_remote_copy(
            src_ref=send_ref.at[2 * j + (1 - c)], dst_ref=out_ref.at[j],
            send_sem=send_sems.at[j], recv_sem=recv_sems.at[j],
            device_id=(x, y, 1 - c), device_id_type=pl.DeviceIdType.MESH) for j in range(nchip)]
        for cp in copies:
            cp.start()
        for cp in copies:
            cp.wait_recv()
        for cp in copies:
            cp.wait_send()

    from_sibling = pl.pallas_call(
        sibling_body, name=name + "_d2d",
        in_specs=[pl.BlockSpec(memory_space=pl.ANY)], out_specs=pl.BlockSpec(memory_space=pl.ANY),
        out_shape=jax.ShapeDtypeStruct((nchip, R, C), send.dtype),
        scratch_shapes=[pltpu.SemaphoreType.DMA((nchip,)), pltpu.SemaphoreType.DMA((nchip,))],
    )(send)
    own = lax.dynamic_index_in_dim(send.reshape(nchip, 2, R, C), lax.axis_index("c"), axis=1, keepdims=False)

    tr = _row_block(R, C * nchip)

    def pair_body(a_ref, b_ref, o_ref):
        o_ref[...] = (a_ref[...].astype(F32) + b_ref[...].astype(F32)).astype(o_ref.dtype)

    spec = pl.BlockSpec((nchip, tr, C), lambda i: (0, i, 0))
    pair = pl.pallas_call(
        pair_body, name=name + "_pair_sum", grid=(R // tr,), in_specs=[spec, spec], out_specs=spec,
        out_shape=jax.ShapeDtypeStruct((nchip, R, C), send.dtype),
        compiler_params=pltpu.CompilerParams(dimension_semantics=("parallel",), vmem_limit_bytes=VMEM_LIMIT),
    )(own, from_sibling)

    def chips_body(src_ref, out_ref, send_sems, recv_sems, local_sem):
        x, y, c = lax.axis_index("x"), lax.axis_index("y"), lax.axis_index("c")
        mine = 2 * x + y
        local = pltpu.make_async_copy(src_ref.at[mine], out_ref.at[mine], local_sem)
        local.start()

        def copy(k, src_slab, dst_slab):
            return pltpu.make_async_remote_copy(
                src_ref=src_ref.at[src_slab], dst_ref=out_ref.at[dst_slab],
                send_sem=send_sems.at[k - 1], recv_sem=recv_sems.at[k - 1],
                device_id=(x ^ (k >> 1), y ^ (k & 1), c), device_id_type=pl.DeviceIdType.MESH)

        sends = [copy(k, mine ^ k, mine) for k in range(1, nchip)]
        for cp in sends:
            cp.start()
        for k in range(1, nchip):
            copy(k, mine ^ k, mine ^ k).wait_recv()
        for cp in sends:
            cp.wait_send()
        local.wait()

    from_chips = pl.pallas_call(
        chips_body, name=name + "_ici",
        in_specs=[pl.BlockSpec(memory_space=pl.ANY)], out_specs=pl.BlockSpec(memory_space=pl.ANY),
        out_shape=jax.ShapeDtypeStruct((nchip, R, C), send.dtype),
        scratch_shapes=[pltpu.SemaphoreType.DMA((nchip - 1,)), pltpu.SemaphoreType.DMA((nchip - 1,)),
                        pltpu.SemaphoreType.DMA],
    )(pair)
    return sum_slabs(from_chips, name=name + "_sum")


TB = 512


def st_norm(t0, x, g):
    return (rms(x.astype(F32), g),)


def _conv(xt, w, bias=None):
    kk = w.shape[0]
    acc = bias
    for i in range(kk):
        term = w[i:i + 1, :] * shift_rows(xt, kk - 1 - i)
        acc = term if acc is None else acc + term
    return acc


def st_a(t0, xa, cw, cb, wr, br, wi, bi, lam):
    xc = _conv(xa, cw, cb)
    r = sigmoid(bmm(xc, wr) + br)
    ig = sigmoid(bmm(xc, wi) + bi)
    log_a = -RG_C * r * softplus(-lam)
    row = lax.broadcasted_iota(jnp.int32, xc.shape, 0) + t0
    mult = jnp.where(row == 0, 1.0, jnp.sqrt(-expm1(2.0 * log_a)))
    return jnp.exp(log_a), mult * ig * xc


def st_a_cot(t0, lam, ha):
    return lam * shift_rows(ha, 1), lam


def _heads(x, n, w):
    return [x[:, h * w:(h + 1) * w] for h in range(n)]


def st_b(t0, q, k, v, pba, cw, alog, dtb):
    qc = silu(_conv(q, cw[:, 0:BW]))
    kc = silu(_conv(k, cw[:, BW:2 * BW]))
    vc = silu(_conv(v, cw[:, 2 * BW:3 * BW]))

    def l2n(x):
        return jnp.concatenate([s * lax.rsqrt(jnp.sum(s * s, axis=-1, keepdims=True) + EPS)
                                for s in _heads(x, B_HEADS, B_DK)], axis=-1)

    sg = sigmoid(pba)
    gg = -jnp.exp(alog) * softplus(pba + dtb)
    lane = lax.broadcasted_iota(jnp.int32, pba.shape, 1)

    def spread(x, first):
        return jnp.concatenate(
            [jnp.broadcast_to(jnp.sum(jnp.where(lane == first + h, x, 0.0), axis=-1, keepdims=True), (x.shape[0], B_DK))
             for h in range(B_HEADS)], axis=-1)

    return l2n(qc), l2n(kc), vc, spread(sg, 0), spread(gg, B_HEADS)


def st_m2(t0, ha, ga, o, z, y0, uc, bn4, cd, wglu, bglu):
    ya = ha * gelu(ga)
    yb = jnp.concatenate(
        [oh * lax.rsqrt(jnp.mean(oh * oh, axis=-1, keepdims=True) + EPS) * bh * silu(zh)
         for oh, bh, zh in zip(_heads(o, B_HEADS, B_DK), _heads(bn4, B_HEADS, B_DK), _heads(z, B_HEADS, B_DK))], axis=-1)
    yc0 = gelu(y0 + cd * uc)
    yc = yc0 * sigmoid(bmm(yc0, wglu) + bglu)
    return (jnp.concatenate([ya, yb, yc], axis=-1),)


def st_m2_cot(t0, d0, d1, d2):
    return (jnp.concatenate([d0, d1, d2], axis=-1),)


def st_m3(t0, g0, g1, g2, p0, p1, p2, bg):
    d = D_MODEL
    return (sigmoid(g0 + bg[:, 0:d]) * p0 + sigmoid(g1 + bg[:, d:2 * d]) * p1 + sigmoid(g2 + bg[:, 2 * d:3 * d]) * p2,)


def st_att(t0, q, kv):
    hs = range(X_HEADS)
    sc = [bmm(q[:, h * X_HD:(h + 1) * X_HD], kv[:, h * X_HD:(h + 1) * X_HD], False, True) * (X_HD ** -0.5) for h in hs]
    e = [jnp.exp(s - lax.stop_gradient(jnp.max(s, axis=-1, keepdims=True))) for s in sc]
    p = [x / jnp.sum(x, axis=-1, keepdims=True) for x in e]
    outs = [bmm(p[h], kv[:, D_MODEL + h * X_HD:D_MODEL + (h + 1) * X_HD]) for h in hs]
    return (jnp.concatenate(outs, axis=-1),)


def st_f2(t0, ug, uv, cwg, cwv, cbg, cbv):
    return (gelu(_conv(ug, cwg, cbg)) * _conv(uv, cwv, cbv),)


def st_s5step(t0, h, ar, ai):
    hp = shift_rows(h, 1)
    hr, hi = hp[:, :S5_CW], hp[:, S5_CW:]
    return (jnp.concatenate([ar * hr - ai * hi, ar * hi + ai * hr], axis=-1),)


SUB_ROWS = {st_f2: 32}


W_MAIN = 6 * BW
W_BA = W_MAIN + 2 * B_HEADS
W_UC = W_BA + BW
COL_GATES, COL_UC = W_MAIN, W_MAIN + 3 * D_MODEL


def split_w_in(wt):
    big = jnp.concatenate([wt[:W_MAIN], wt[W_UC:], wt[W_BA:W_UC]], axis=0)
    ba = jnp.pad(wt[W_MAIN:W_BA], ((0, 128 - 2 * B_HEADS), (0, 0)))
    return big, ba


def merge_w_in(big, ba):
    return jnp.concatenate([big[:W_MAIN], ba[:2 * B_HEADS], big[COL_UC:], big[COL_GATES:COL_UC]], axis=0)


def derive(r):
    L = r['a_w_r'].shape[0]
    eye_a = jnp.eye(A_HEADS, dtype=F32)
    eye_g = jnp.eye(C_GROUPS, dtype=F32)

    def blockdiag(w):
        return jnp.einsum('lhij,hg->lhigj', w, eye_a).reshape(L, BW, BW)

    def lanes(v, first):
        return jnp.pad(v, ((0, 0), (first, 128 - first - B_HEADS)))[:, None, :]

    lr, li = r['c_lam_re'], r['c_lam_im']
    dt = jnp.exp(r['c_log_dt'])[..., None]
    mag = jnp.exp(lr * dt)
    ar, ai = mag * jnp.cos(li * dt), mag * jnp.sin(li * dt)
    den = lr * lr + li * li
    fr = ((ar - 1.0) * lr + ai * li) / den
    fi = (ai * lr - (ar - 1.0) * li) / den
    br, bi = r['c_b_re'], r['c_b_im']
    bbr = fr[..., None] * br - fi[..., None] * bi
    bbi = fr[..., None] * bi + fi[..., None] * br
    ncol = C_CH // S5_CW

    def b_dense(bb):
        return jnp.einsum('lgpc,gh->lgchp', bb, eye_g).reshape(L, BW, ncol, S5_CW)

    bbig = jnp.concatenate([b_dense(bbr), b_dense(bbi)], axis=3).reshape(L, BW, 2 * C_CH)

    def c_dense(cc):
        return jnp.einsum('lgcp,gh->lgphc', cc, eye_g).reshape(L, ncol, S5_CW, BW)

    cbig = jnp.stack([c_dense(r['c_c_re']), -c_dense(r['c_c_im'])], axis=2).reshape(L, 2 * C_CH, BW)
    return dict(wr=blockdiag(r['a_w_r']), wi=blockdiag(r['a_w_i']),
                alog=lanes(r['b_a_log'], B_HEADS), dtb=lanes(r['b_dt_bias'], B_HEADS),
                bn4=jnp.tile(r['b_norm'], (1, B_HEADS))[:, None, :],
                ar=ar.reshape(L, 1, C_CH), ai=ai.reshape(L, 1, C_CH), bbig=bbig, cbig=cbig)


DERIVE_FROM = ['a_w_r', 'a_w_i', 'b_a_log', 'b_dt_bias', 'b_norm', 'c_lam_re', 'c_lam_im', 'c_log_dt',
               'c_b_re', 'c_b_im', 'c_c_re', 'c_c_im']


def _row(v):
    return v.reshape(1, -1)


def layer_params(w, dv, l):
    big, ba = split_w_in(w['w_inT'][l])
    return dict(
        mix_norm=_row(w['mix_norm'][l]), w_big=big, w_ba=ba, b_gate=_row(w['b_gate'][l]),
        a_conv_w=w['a_conv_w'][l], a_conv_b=_row(w['a_conv_b'][l]), wr=dv['wr'][l], b_r=_row(w['a_b_r'][l]),
        wi=dv['wi'][l], b_i=_row(w['a_b_i'][l]), lam=_row(w['a_lam'][l]),
        b_conv_w=w['b_conv_w'][l], alog=dv['alog'][l], dtb=dv['dtb'][l], bn4=dv['bn4'][l],
        ar=dv['ar'][l], ai=dv['ai'][l], bbig=dv['bbig'][l], cbig=dv['cbig'][l],
        c_d=_row(w['c_d'][l]), wglu=w['c_glu_w'][l].astype(F32), bglu=_row(w['c_glu_b'][l]),
        w_brT=w['w_brT'][l], w_out=w['w_out'][l],
        xa_norm=_row(w['xa_norm'][l]), mem_norm=_row(w['mem_norm'][l]),
        w_q=w['xa_w_q'][l], w_kvT=w['w_kvT'][l], w_o=w['xa_w_o'][l],
        ffn_norm=_row(w['ffn_norm'][l]), w_upT=w['w_upT'][l], ffn_conv_w=w['ffn_conv_w'][l],
        ffn_conv_b=_row(w['ffn_conv_b'][l]), w_down=w['ffn_w_down'][l])


def _a_pars(p):
    return [Par(p['a_conv_w']), Par(p['a_conv_b']), Par(p['wr']), Par(p['b_r']), Par(p['wi']), Par(p['b_i']), Par(p['lam'])]


def _f2_pars(p):
    d = D_MODEL
    return [Par(p['ffn_conv_w'], 'col', 0, d), Par(p['ffn_conv_w'], 'col', 3, d),
            Par(p['ffn_conv_b'], 'col', 0, d), Par(p['ffn_conv_b'], 'col', 3, d)]


def layer_fwd(x, mem, p, seq, mseq, l):
    d = D_MODEL
    nb = x.shape[0] // seq
    kw = dict(seq=seq, tb=TB)
    n = lambda s: f"{s}_l{l}"
    h, = stage_fwd(st_norm, [Tok(x, d)], [Par(p['mix_norm'])], [(d, MXU_DTYPE)], name=n("norm_mix"), **kw)
    P = mm(h, p['w_big'], tb=True, name=n("mm_in"))
    Pba = mm(h, p['w_ba'], tb=True, name=n("mm_in_ba"))
    a, bb = stage_fwd(st_a, [Tok(P, BW, 0, True)], _a_pars(p), [(BW, F32)] * 2, name=n("rglru_pre"), **kw)
    ha = scan_real(a, bb, seq=seq, reverse=False, name=n("rglru_scan"))
    qn, kn, vv, bB, gB = stage_fwd(
        st_b, [Tok(P, BW, 2, True), Tok(P, BW, 3, True), Tok(P, BW, 4, True), Tok(Pba, 128)],
        [Par(p['b_conv_w']), Par(p['alog']), Par(p['dtb'])], [(BW, F32)] * 5, name=n("delta_pre"), **kw)
    r3 = lambda t: t.reshape(nb, seq, BW)
    o3, states, tinvs = delta_fwd(r3(qn), r3(kn), r3(vv), r3(bB), r3(gB), name=n("delta"))
    o = o3.reshape(-1, BW)
    bu = mm(P, p['bbig'], a_cols=(COL_UC, BW), name=n("mm_s5_in"))
    hs = scan_cplx(p['ar'], p['ai'], bu, seq=seq, reverse=False, name=n("s5_scan"))
    y0 = mm(hs, p['cbig'], name=n("mm_s5_out"))
    m2_toks = [Tok(ha, BW), Tok(P, BW, 1), Tok(o, BW), Tok(P, BW, 5), Tok(y0, BW), Tok(P, BW, COL_UC // BW)]
    m2_pars = [Par(p['bn4']), Par(p['c_d']), Par(p['wglu']), Par(p['bglu'])]
    Y3, = stage_fwd(st_m2, m2_toks, m2_pars, [(3 * BW, MXU_DTYPE)], name=n("branches"), **kw)
    proj = [mm(Y3, p['w_brT'][k], tb=True, a_cols=(k * BW, BW), name=n(f"mm_branch{k}")) for k in range(3)]
    g0 = COL_GATES // d
    m3_toks = [Tok(P, d, g0), Tok(P, d, g0 + 1), Tok(P, d, g0 + 2)] + [Tok(t, d) for t in proj]
    mixed, = stage_fwd(st_m3, m3_toks, [Par(p['b_gate'])], [(d, MXU_DTYPE)], name=n("gate_mix"), **kw)
    x1 = mm(mixed, p['w_out'], residual=x, name=n("mm_out"))
    hx, = stage_fwd(st_norm, [Tok(x1, d)], [Par(p['xa_norm'])], [(d, MXU_DTYPE)], name=n("norm_xa"), **kw)
    mn, = stage_fwd(st_norm, [Tok(mem, d)], [Par(p['mem_norm'])], [(d, MXU_DTYPE)], seq=mseq, tb=TB, name=n("norm_mem"))
    qx = mm(hx, p['w_q'], name=n("mm_q"))
    kv = mm(mn, p['w_kvT'], tb=True, name=n("mm_kv"))
    kv3 = kv.reshape(nb, mseq, 2 * d)
    ox, = stage_fwd(st_att, [Tok(qx, d)], [Par(kv3, 'batch')], [(d, MXU_DTYPE)], name=n("attention"), **kw)
    x2 = mm(ox, p['w_o'], residual=x1, name=n("mm_o"))
    hf, = stage_fwd(st_norm, [Tok(x2, d)], [Par(p['ffn_norm'])], [(d, MXU_DTYPE)], name=n("norm_ffn"), **kw)
    U = mm(hf, p['w_upT'], tb=True, name=n("mm_up"))
    act, = stage_fwd(st_f2, [Tok(U, d, 0, True), Tok(U, d, 3, True)], _f2_pars(p), [(d, MXU_DTYPE)], ncol=3,
                     name=n("ffn_act"), **kw)
    x3 = mm(act, p['w_down'], residual=x2, name=n("mm_down"))
    sv = dict(x=x, h=h, P=P, Pba=Pba, a=a, ha=ha, qn=qn, kn=kn, vv=vv, bB=bB, gB=gB, states=states, tinvs=tinvs, o=o, hs=hs, y0=y0,
              Y3=Y3, proj=proj, mixed=mixed, x1=x1, hx=hx, mn=mn, qx=qx, kv3=kv3, ox=ox, x2=x2, hf=hf, U=U, act=act)
    return x3, sv


def layer_bwd(dx3, mem, p, sv, seq, mseq, l):
    d = D_MODEL
    nb = dx3.shape[0] // seq
    kw = dict(seq=seq, tb=TB)
    n = lambda s: f"{s}_l{l}"
    g = {}
    P, Pba = sv['P'], sv['Pba']
    dact = mm(dx3, p['w_down'], tb=True, name=n("bmm_down_x"))
    g['w_down'] = mm(sv['act'], dx3, ta=True, name=n("bmm_down_w"))
    (dUg, dUv), gp = stage_bwd(st_f2, [Tok(sv['U'], d, 0, True, MXU_DTYPE), Tok(sv['U'], d, 3, True, MXU_DTYPE)], _f2_pars(p),
                               [Tok(dact, d)], ncol=3, name=n("b_ffn_act"), **kw)
    g['ffn_conv_w'] = jnp.concatenate([gp[0], gp[1]], axis=1)
    g['ffn_conv_b'] = jnp.concatenate([gp[2], gp[3]], axis=1)
    dU = jnp.concatenate([dUg, dUv], axis=1)
    dhf = mm(dU, p['w_upT'], name=n("bmm_up_x"))
    g['w_upT'] = mm(dU, sv['hf'], ta=True, name=n("bmm_up_w"))
    (dx2,), (g['ffn_norm'],) = stage_bwd(st_norm, [Tok(sv['x2'], d, grad=F32, add=dx3)], [Par(p['ffn_norm'])],
                                         [Tok(dhf, d)], name=n("b_norm_ffn"), **kw)
    dox = mm(dx2, p['w_o'], tb=True, name=n("bmm_o_x"))
    g['w_o'] = mm(sv['ox'], dx2, ta=True, name=n("bmm_o_w"))
    (dqx,), (dkv3,) = stage_bwd(st_att, [Tok(sv['qx'], d, grad=MXU_DTYPE)], [Par(sv['kv3'], 'batch')], [Tok(dox, d)],
                                name=n("b_attention"), **kw)
    dkv = dkv3.reshape(-1, 2 * d)
    dhx = mm(dqx, p['w_q'], tb=True, name=n("bmm_q_x"))
    g['w_q'] = mm(sv['hx'], dqx, ta=True, name=n("bmm_q_w"))
    dmn = mm(dkv, p['w_kvT'], name=n("bmm_kv_x"))
    g['w_kvT'] = mm(dkv, sv['mn'], ta=True, name=n("bmm_kv_w"))
    _, (g['mem_norm'],) = stage_bwd(st_norm, [Tok(mem, d)], [Par(p['mem_norm'])], [Tok(dmn, d)], seq=mseq, tb=TB,
                                    name=n("b_norm_mem"))
    (dx1,), (g['xa_norm'],) = stage_bwd(st_norm, [Tok(sv['x1'], d, grad=F32, add=dx2)], [Par(p['xa_norm'])],
                                        [Tok(dhx, d)], name=n("b_norm_xa"), **kw)
    dmixed = mm(dx1, p['w_out'], tb=True, name=n("bmm_out_x"))
    g['w_out'] = mm(sv['mixed'], dx1, ta=True, name=n("bmm_out_w"))
    g0 = COL_GATES // d
    m3_toks = [Tok(P, d, g0 + k, grad=MXU_DTYPE) for k in range(3)] + [Tok(t, d, grad=MXU_DTYPE) for t in sv['proj']]
    dm3, (g['b_gate'],) = stage_bwd(st_m3, m3_toks, [Par(p['b_gate'])], [Tok(dmixed, d)], name=n("b_gate_mix"), **kw)
    dgates, dproj = dm3[:3], dm3[3:]
    dY = [mm(dproj[k], p['w_brT'][k], name=n(f"bmm_branch{k}_x")) for k in range(3)]
    g['w_brT'] = jnp.stack([mm(dproj[k], sv['Y3'], ta=True, b_cols=(k * BW, BW), name=n(f"bmm_branch{k}_w"))
                            for k in range(3)])
    m2_toks = [Tok(sv['ha'], BW, grad=F32), Tok(P, BW, 1, grad=MXU_DTYPE), Tok(sv['o'], BW, grad=F32), Tok(P, BW, 5, grad=MXU_DTYPE),
               Tok(sv['y0'], BW, grad=MXU_DTYPE), Tok(P, BW, COL_UC // BW, grad=F32)]
    m2_pars = [Par(p['bn4']), Par(p['c_d']), Par(p['wglu']), Par(p['bglu'])]
    (dha, dga, do, dz, dy0, duc0), (g['bn4'], g['c_d'], g['wglu'], g['bglu']) = stage_bwd(
        st_m2, m2_toks, m2_pars, [Tok(t, BW) for t in dY], cot_fn=st_m2_cot, name=n("b_branches"), **kw)
    dhs = mm(dy0, p['cbig'], tb=True, name=n("bmm_s5_out_x"))
    g['cbig'] = mm(sv['hs'], dy0, ta=True, name=n("bmm_s5_out_w"))
    lam_s = scan_cplx(p['ar'], -p['ai'], dhs, seq=seq, reverse=True, name=n("b_s5_scan"))
    _, (g['ar'], g['ai']) = stage_bwd(st_s5step, [Tok(sv['hs'], 2 * S5_CW, 0, True)],
                                      [Par(p['ar'], 'col', 0, S5_CW), Par(p['ai'], 'col', 0, S5_CW)],
                                      [Tok(lam_s, 2 * S5_CW)], ncol=C_CH // S5_CW, name=n("b_s5_decay"), **kw)
    duc = mm(lam_s, p['bbig'], tb=True, residual=duc0, out_dtype=MXU_DTYPE, name=n("bmm_s5_in_x"))
    g['bbig'] = mm(P, lam_s, ta=True, a_cols=(COL_UC, BW), name=n("bmm_s5_in_w"))
    lam_a = scan_real(sv['a'], dha, seq=seq, reverse=True, name=n("b_rglru_scan"))
    (dxa,), ga = stage_bwd(st_a, [Tok(P, BW, 0, True, MXU_DTYPE)], _a_pars(p), [Tok(lam_a, BW), Tok(sv['ha'], BW, 0, True)],
                           cot_fn=st_a_cot, name=n("b_rglru_pre"), **kw)
    g['a_conv_w'], g['a_conv_b'], g['wr'], g['b_r'], g['wi'], g['b_i'], g['lam'] = ga
    r3 = lambda t: t.reshape(nb, seq, BW)
    dd = delta_bwd(r3(sv['qn']), r3(sv['kn']), r3(sv['vv']), r3(sv['bB']), r3(sv['gB']), sv['states'], sv['tinvs'],
                   r3(do), name=n("b_delta"))
    (dq, dk, dv, dpba), (g['b_conv_w'], g['alog'], g['dtb']) = stage_bwd(
        st_b, [Tok(P, BW, 2, True, MXU_DTYPE), Tok(P, BW, 3, True, MXU_DTYPE), Tok(P, BW, 4, True, MXU_DTYPE), Tok(Pba, 128, grad=MXU_DTYPE)],
        [Par(p['b_conv_w']), Par(p['alog']), Par(p['dtb'])], [Tok(t.reshape(-1, BW), BW) for t in dd],
        name=n("b_delta_pre"), **kw)
    dP = jnp.concatenate([dxa, dga, dq, dk, dv, dz] + list(dgates) + [duc], axis=1)
    dh0 = mm(dpba, p['w_ba'], name=n("bmm_in_ba_x"))
    dh = mm(dP, p['w_big'], residual=dh0, name=n("bmm_in_x"))
    g['w_big'] = mm(dP, sv['h'], ta=True, name=n("bmm_in_w"))
    g['w_ba'] = mm(dpba, sv['h'], ta=True, name=n("bmm_in_ba_w"))
    (dx,), (g['mix_norm'],) = stage_bwd(st_norm, [Tok(sv['x'], d, grad=F32, add=dx1)], [Par(p['mix_norm'])],
                                        [Tok(dh, d)], name=n("b_norm_mix"), **kw)
    return dx, g


def local_step(x3d, mem3d, tgt3d, w, final_norm):
    nb, seq, d = x3d.shape
    mseq = mem3d.shape[1]
    x = x3d.reshape(nb * seq, d)
    mem = mem3d.reshape(nb * mseq, d)
    L = w['mix_norm'].shape[0]
    dv, dv_vjp = jax.vjp(derive, {k: w[k] for k in DERIVE_FROM})
    ps, svs = [], []
    for l in range(L):
        p = layer_params(w, dv, l)
        x, sv = layer_fwd(x, mem, p, seq, mseq, l)
        ps.append(p)
        svs.append(sv)
    loss, dx, g_final = loss_head(x, tgt3d.reshape(nb * seq, d), _row(final_norm), tb=TB, name="loss_head")
    gs = [None] * L
    for l in reversed(range(L)):
        dx, gs[l] = layer_bwd(dx, mem, ps[l], svs[l], seq, mseq, l)
    st = lambda k: jnp.stack([gs[l][k] for l in range(L)])
    vec = lambda k: st(k).reshape(L, -1)
    gd = dv_vjp({k: st(k) for k in ('wr', 'wi', 'alog', 'dtb', 'bn4', 'ar', 'ai', 'bbig', 'cbig')})[0]
    out = dict(gd)
    per = lambda k: [gs[l][k] for l in range(L)]
    out.update(
        mix_norm=vec('mix_norm'), w_inT=[merge_w_in(gs[l]['w_big'], gs[l]['w_ba']) for l in range(L)],
        b_gate=vec('b_gate'),
        a_conv_w=st('a_conv_w'), a_conv_b=vec('a_conv_b'), a_b_r=vec('b_r'), a_b_i=vec('b_i'), a_lam=vec('lam'),
        b_conv_w=st('b_conv_w'), c_d=vec('c_d'), c_glu_w=per('wglu'), c_glu_b=vec('bglu'),
        w_brT=per('w_brT'), w_out=per('w_out'), xa_norm=vec('xa_norm'), mem_norm=vec('mem_norm'),
        xa_w_q=per('w_q'), w_kvT=per('w_kvT'), xa_w_o=per('w_o'), ffn_norm=vec('ffn_norm'), w_upT=per('w_upT'),
        ffn_conv_w=st('ffn_conv_w'), ffn_conv_b=vec('ffn_conv_b'), ffn_w_down=per('w_down'),
        final_norm=g_final.reshape(-1))
    return loss, dx.reshape(nb, seq, d), out


LANES = 1024
ROW_PAD = 256


def _small_rows(shape):
    return -(-int(np.prod(shape)) // (LANES * SUBLANES)) * SUBLANES


def _pack(vecs, dtype):
    segs = []
    for v in vecs:
        rows = _small_rows(v.shape)
        flat = v.reshape(-1).astype(dtype)
        segs.append(jnp.pad(flat, (0, rows * LANES - flat.shape[0])).reshape(rows, LANES))
    total = sum(s.shape[0] for s in segs)
    tail = -total % ROW_PAD
    if tail:
        segs.append(jnp.zeros((tail, LANES), dtype))
    return jnp.concatenate(segs, axis=0)


BIG_LAYOUT = {'w_in': 'w_inT', 'xa_w_kv': 'w_kvT', 'ffn_w_up': 'w_upT', 'w_branch': 'w_brT'}
SEG_ALIGN = 16


def _to_layout(n, a):
    return jnp.swapaxes(a, -1, -2) if n in BIG_LAYOUT else a


def _seg_rows(shape):
    rows = int(np.prod(shape)) // LANES
    return rows, -(-rows // SEG_ALIGN) * SEG_ALIGN


def _pack_segments(mats, dtype):
    lead = mats[0].ndim - 2
    segs = []
    for m in mats:
        pad = -m.shape[-2] % SEG_ALIGN
        segs.append(jnp.pad(m.astype(dtype), [(0, 0)] * lead + [(0, pad), (0, 0)]))
    total = sum(s.shape[-2] for s in segs)
    tail = -total % ROW_PAD
    if tail:
        segs.append(jnp.zeros(segs[0].shape[:-2] + (tail, LANES), dtype))
    return jnp.concatenate(segs, axis=-2)


def _unpack_segments(buf, shapes):
    lead = buf.shape[:-2]
    out, off = [], 0
    for shp in shapes:
        rows, padded = _seg_rows(shp)
        out.append(buf[..., off:off + rows, :].reshape(lead + tuple(shp)))
        off += padded
    return out


def _unpack(flat, shapes):
    lead = flat.shape[:-2]
    out, off = [], 0
    for shp in shapes:
        cnt, rows = int(np.prod(shp)), _small_rows(shp)
        seg = flat[..., off:off + rows, :].reshape(lead + (-1,))
        out.append(seg[..., :cnt].reshape(lead + tuple(shp)))
        off += rows
    return out


def _join_shards(stacked, axis):
    t = jnp.moveaxis(stacked, 0, axis)
    shp = list(t.shape)
    return t.reshape(shp[:axis] + [shp[axis] * shp[axis + 1]] + shp[axis + 2:])


def _as2d(a):
    return a.reshape(-1, a.shape[-1]) if a.ndim > 1 else a.reshape(1, -1)


def kernel(x, mem, mix_norm, w_in, b_gate, a_conv_w, a_conv_b, a_w_r, a_b_r, a_w_i, a_b_i, a_lam, b_conv_w, b_a_log, b_dt_bias, b_norm, c_lam_re, c_lam_im, c_log_dt, c_b_re, c_b_im, c_c_re, c_c_im, c_d, c_glu_w, c_glu_b, w_branch, w_out, xa_norm, mem_norm, xa_w_q, xa_w_kv, xa_w_o, ffn_norm, ffn_w_up, ffn_conv_w, ffn_conv_b, ffn_w_down, final_norm, loss_target, m_mix_norm, m_w_in, m_b_gate, m_a_conv_w, m_a_conv_b, m_a_w_r, m_a_b_r, m_a_w_i, m_a_b_i, m_a_lam, m_b_conv_w, m_b_a_log, m_b_dt_bias, m_b_norm, m_c_lam_re, m_c_lam_im, m_c_log_dt, m_c_b_re, m_c_b_im, m_c_c_re, m_c_c_im, m_c_d, m_c_glu_w, m_c_glu_b, m_w_branch, m_w_out, m_xa_norm, m_mem_norm, m_xa_w_q, m_xa_w_kv, m_xa_w_o, m_ffn_norm, m_ffn_w_up, m_ffn_conv_w, m_ffn_conv_b, m_ffn_w_down, m_final_norm, v_mix_norm, v_w_in, v_b_gate, v_a_conv_w, v_a_conv_b, v_a_w_r, v_a_b_r, v_a_w_i, v_a_b_i, v_a_lam, v_b_conv_w, v_b_a_log, v_b_dt_bias, v_b_norm, v_c_lam_re, v_c_lam_im, v_c_log_dt, v_c_b_re, v_c_b_im, v_c_c_re, v_c_c_im, v_c_d, v_c_glu_w, v_c_glu_b, v_w_branch, v_w_out, v_xa_norm, v_mem_norm, v_xa_w_q, v_xa_w_kv, v_xa_w_o, v_ffn_norm, v_ffn_w_up, v_ffn_conv_w, v_ffn_conv_b, v_ffn_w_down, v_final_norm):
    args = (x, mem, mix_norm, w_in, b_gate, a_conv_w, a_conv_b, a_w_r, a_b_r, a_w_i, a_b_i, a_lam, b_conv_w, b_a_log, b_dt_bias, b_norm, c_lam_re, c_lam_im, c_log_dt, c_b_re, c_b_im, c_c_re, c_c_im, c_d, c_glu_w, c_glu_b, w_branch, w_out, xa_norm, mem_norm, xa_w_q, xa_w_kv, xa_w_o, ffn_norm, ffn_w_up, ffn_conv_w, ffn_conv_b, ffn_w_down, final_norm, loss_target, m_mix_norm, m_w_in, m_b_gate, m_a_conv_w, m_a_conv_b, m_a_w_r, m_a_b_r, m_a_w_i, m_a_b_i, m_a_lam, m_b_conv_w, m_b_a_log, m_b_dt_bias, m_b_norm, m_c_lam_re, m_c_lam_im, m_c_log_dt, m_c_b_re, m_c_b_im, m_c_c_re, m_c_c_im, m_c_d, m_c_glu_w, m_c_glu_b, m_w_branch, m_w_out, m_xa_norm, m_mem_norm, m_xa_w_q, m_xa_w_kv, m_xa_w_o, m_ffn_norm, m_ffn_w_up, m_ffn_conv_w, m_ffn_conv_b, m_ffn_w_down, m_final_norm, v_mix_norm, v_w_in, v_b_gate, v_a_conv_w, v_a_conv_b, v_a_w_r, v_a_b_r, v_a_w_i, v_a_b_i, v_a_lam, v_b_conv_w, v_b_a_log, v_b_dt_bias, v_b_norm, v_c_lam_re, v_c_lam_im, v_c_log_dt, v_c_b_re, v_c_b_im, v_c_c_re, v_c_c_im, v_c_d, v_c_glu_w, v_c_glu_b, v_w_branch, v_w_out, v_xa_norm, v_mem_norm, v_xa_w_q, v_xa_w_kv, v_xa_w_o, v_ffn_norm, v_ffn_w_up, v_ffn_conv_w, v_ffn_conv_b, v_ffn_w_down, v_final_norm)
    nw = len(WEIGHTS)
    x, mem = args[0], args[1]
    w_loc = dict(zip(WEIGHTS, args[2:2 + nw]))
    tgt = args[2 + nw]
    m_loc = dict(zip(WEIGHTS, args[3 + nw:3 + 2 * nw]))
    v_loc = dict(zip(WEIGHTS, args[3 + 2 * nw:3 + 3 * nw]))
    me = 4 * lax.axis_index("x") + 2 * lax.axis_index("y") + lax.axis_index("c")

    lay = {n: _to_layout(n, w_loc[n]) for n in BIG}
    nl = DEPTH
    seg_keys = [(n, l) for n in BIG for l in range(nl)]
    seg_shapes = [lay[n].shape[1:] for n, _ in seg_keys]
    gathered = _gather(_pack_segments([lay[n][l].reshape(-1, LANES) for n, l in seg_keys], BF),
                       name="gather_matmul_weights")
    w = {n: a for n, a in w_loc.items() if n not in BIG}
    for (n, l), st in zip(seg_keys, _unpack_segments(gathered, seg_shapes)):
        t = jnp.moveaxis(st, 0, -3)
        w.setdefault(BIG_LAYOUT.get(n, n), []).append(t.reshape(t.shape[:-3] + (N_DEV * t.shape[-2], t.shape[-1])))
    ss_shapes = [w_loc[n].shape for n in SMALL_SHARDED]
    gathered_s = _gather(_pack([w_loc[n] for n in SMALL_SHARDED], F32), name="gather_conv_weights")
    for n, st in zip(SMALL_SHARDED, _unpack(gathered_s, ss_shapes)):
        w[n] = _join_shards(st, SHARD_AXIS[n])

    final_norm = w.pop('final_norm')
    loss, grad_x, g = local_step(x, mem, tgt, w, final_norm)

    def cut(full):
        t = full.reshape(full.shape[:-2] + (N_DEV, full.shape[-2] // N_DEV, full.shape[-1]))
        return jnp.moveaxis(t, -3, 0).reshape(N_DEV, -1, LANES)

    send = _pack_segments([cut(g[BIG_LAYOUT.get(n, n)][l]) for n, l in seg_keys], BF)
    g_seg = _unpack_segments(_scatter_sum(send, name="scatter_matmul_grads"), seg_shapes)
    g_big = {n: _to_layout(n, jnp.stack(g_seg[i * nl:(i + 1) * nl])) for i, n in enumerate(BIG)}
    small_full_shapes = [g[n].shape for n in SMALL]
    packed = _pack([g[n] for n in SMALL] + [loss[0, :1]], F32)
    everyones = _gather(packed, name="gather_small_grads")
    summed = _unpack(sum_slabs(everyones, name="sum_small_grads"), small_full_shapes + [(1,)])
    loss_total = summed[-1].reshape(())
    g_small = {}
    for n, full in zip(SMALL, summed[:-1]):
        full = full.reshape(small_full_shapes[SMALL.index(n)])
        if n in SMALL_SHARDED:
            ax = SHARD_AXIS[n]
            loc = w_loc[n].shape[ax]
            full = lax.dynamic_slice_in_dim(full, me * loc, loc, axis=ax)
        g_small[n] = full.reshape(w_loc[n].shape)

    grads, delta, new_m, new_v = {}, {}, {}, {}
    grads.update(g_big)
    grads.update(g_small)
    for n in WEIGHTS:
        shp = w_loc[n].shape
        dl, nm, nv = adamw(_as2d(w_loc[n]), _as2d(grads[n]), _as2d(m_loc[n]), _as2d(v_loc[n]), name=f"adamw_{n}")
        delta[n], new_m[n], new_v[n] = dl.reshape(shp), nm.reshape(shp), nv.reshape(shp)
    return (loss_total, grad_x, *[grads[n] for n in WEIGHTS], *[delta[n] for n in WEIGHTS],
            *[new_m[n] for n in WEIGHTS], *[new_v[n] for n in WEIGHTS])
```
